```python
import math
import numpy as np
import jax
import jax.numpy as jnp
from jax import lax

D_MODEL = 1024
BATCH = 2
SEQ = 8192
DEPTH = 2
DEC_BATCH = 128
DEC_SEQ = 8
PAST_LEN = 16384
PAGE_SIZE = 128

N_AB = (DEPTH + 1) // 2
N_CD = DEPTH // 2

H_A = 16
P_A = 64
N_A = 128
G_A = 2
CONV_A = 4
CHUNK_A = 128
D_INNER_A = H_A * P_A
CONV_DIM_A = D_INNER_A + 2 * G_A * N_A

H_B = 8
KV_B = 2
HD_B = 64
WINDOW = 128

H_C = 4
DK_C = 128
DV_C = 128
CHUNK_C = 64

H_D = 16
KV_D = 4
HD_D = 64
Q_BLOCK_D = 128

D_FF = 2816
FFN_CONV = 3

SIZES_AB = (D_INNER_A, CONV_DIM_A, H_A, H_B * HD_B, KV_B * HD_B, KV_B * HD_B)
SIZES_CD = (H_C * DK_C, H_C * DK_C, H_C * DV_C, H_C * DV_C, H_D * HD_D, KV_D * HD_D, KV_D * HD_D, H_D)
MIX_AB = D_INNER_A + H_B * HD_B
MIX_CD = H_C * DV_C + H_D * HD_D

ALPHA = (2 * DEPTH) ** 0.25
BETA = (8 * DEPTH) ** -0.25
SCALE_B = HD_B ** -0.5
SCALE_D = HD_D ** -0.5
LN_EPS = 1e-5
RMS_EPS = 1e-6

kernel_name = 'hybrid_ssd_swa_hgrn2_fox_decode_step'

STATE_NAMES = ('ssm_a', 'conv_a', 'swa_k', 'swa_v', 'hgrn_c', 'fox_k', 'fox_v', 'fox_logf', 'ffn_conv')


def split_cols(a, sizes):
    return jnp.split(a, np.cumsum(sizes)[:-1].tolist(), axis=-1)


def layer_norm(x, g, b):
    xf = x.astype(jnp.float32)
    mu = jnp.mean(xf, axis=-1, keepdims=True)
    var = jnp.mean(jnp.square(xf - mu), axis=-1, keepdims=True)
    return ((xf - mu) * lax.rsqrt(var + LN_EPS) * g + b).astype(x.dtype)


def rms_norm(x, w):
    xf = x.astype(jnp.float32)
    return (xf * lax.rsqrt(jnp.mean(jnp.square(xf), axis=-1, keepdims=True) + RMS_EPS) * w).astype(x.dtype)


def ada_mod(c, w, b):
    m = (jax.nn.silu(c) @ w + b)[:, None, :]
    return jnp.split(m, 3, axis=-1)


def causal_dwconv(x, buf, w, b):
    width = w.shape[0]
    L = x.shape[1]
    xp = jnp.concatenate([buf.astype(x.dtype), x], axis=1)
    y = b + xp[:, width - 1:width - 1 + L] * w[width - 1]
    for tap in range(width - 1):
        y = y + xp[:, tap:tap + L] * w[tap]
    return y, xp[:, L:]


def pad_seq(a, n):
    extra = n - a.shape[1]
    return jnp.pad(a, [(0, 0), (0, extra)] + [(0, 0)] * (a.ndim - 2))


def to_chunks(a, chunk):
    a = a.reshape(a.shape[0], a.shape[1] // chunk, chunk, *a.shape[2:])
    return jnp.moveaxis(a, 1, 0)


def from_chunks(a, n):
    a = jnp.moveaxis(a, 0, 1)
    return a.reshape(a.shape[0], -1, *a.shape[3:])[:, :n]


def ssd_chunked(x, la, bm, cm, s0):
    L = x.shape[1]
    chunk = min(CHUNK_A, L)
    n = -(-L // chunk) * chunk
    xs = tuple(to_chunks(pad_seq(a, n), chunk) for a in (x, la, bm, cm))
    tri = jnp.tril(jnp.ones((chunk, chunk), bool))[None, :, :, None]

    def step(S, inp):
        xc, lc, bc, cc = inp
        cum = jnp.cumsum(lc.astype(jnp.float32), axis=1)
        decay = jnp.exp(jnp.where(tri, cum[:, :, None, :] - cum[:, None, :, :], -jnp.inf))
        scores = jnp.einsum('bthn,bshn->btsh', cc, bc) * decay
        y = (jnp.einsum('btsh,bshp->bthp', scores, xc)
             + jnp.einsum('bthn,bhnp->bthp', cc, S) * jnp.exp(cum)[..., None])
        last = cum[:, -1]
        S = (jnp.exp(last)[:, :, None, None] * S
             + jnp.einsum('bshn,bsh,bshp->bhnp', bc, jnp.exp(last[:, None] - cum), xc))
        return S, y

    S, ys = lax.scan(step, s0.astype(jnp.float32), xs)
    return from_chunks(ys, L), S


def gla_chunked(q, k, v, lf, s0):
    L = q.shape[1]
    chunk = min(CHUNK_C, L)
    n = -(-L // chunk) * chunk
    xs = tuple(to_chunks(pad_seq(a, n), chunk) for a in (q, k, v, lf))
    tri = jnp.tril(jnp.ones((chunk, chunk), bool))[None, :, :, None, None]

    def step(S, inp):
        qc, kc, vc, lc = inp
        cum = jnp.cumsum(lc.astype(jnp.float32), axis=1)
        decay = jnp.exp(jnp.where(tri, cum[:, :, None] - cum[:, None], -jnp.inf))
        scores = jnp.einsum('bthk,btshk,bshk->btsh', qc, decay, kc)
        o = (jnp.einsum('btsh,bshv->bthv', scores, vc)
             + jnp.einsum('bthk,bhkv->bthv', qc * jnp.exp(cum), S))
        last = cum[:, -1]
        S = (jnp.exp(last)[..., None] * S
             + jnp.einsum('bshk,bshv->bhkv', kc * jnp.exp(last[:, None] - cum), vc))
        return S, o

    S, os_ = lax.scan(step, s0.astype(jnp.float32), xs)
    return from_chunks(os_, L), S


def mamba_mixer(z, xbc, dt_raw, conv0, ssm0, conv_w, conv_b, dt_bias, a_log, d_skip, norm_w):
    Bsz, L = z.shape[:2]
    xbc, conv1 = causal_dwconv(xbc, conv0, conv_w, conv_b)
    xbc = jax.nn.silu(xbc)
    xs, bm, cm = jnp.split(xbc, [D_INNER_A, D_INNER_A + G_A * N_A], axis=-1)
    xs = xs.reshape(Bsz, L, H_A, P_A)
    bm = jnp.repeat(bm.reshape(Bsz, L, G_A, N_A), H_A // G_A, axis=2)
    cm = jnp.repeat(cm.reshape(Bsz, L, G_A, N_A), H_A // G_A, axis=2)
    dt = jax.nn.softplus(dt_raw.astype(jnp.float32) + dt_bias)
    a = -jnp.exp(a_log.astype(jnp.float32))
    y, ssm1 = ssd_chunked(xs * dt[..., None], dt * a, bm, cm, ssm0)
    y = (y + d_skip[:, None] * xs).reshape(Bsz, L, D_INNER_A)
    y = rms_norm(y * jax.nn.silu(z), norm_w)
    return y, conv1, ssm1


def sink_softmax(s, sinks):
    kv, g = s.shape[-4], s.shape[-3]
    sink = jnp.broadcast_to(sinks.astype(jnp.float32).reshape(kv, g, 1, 1), s.shape[:-1] + (1,))
    return jax.nn.softmax(jnp.concatenate([s, sink], axis=-1), axis=-1)[..., :-1]


def swa_prompt(q, k, v, sinks):
    Bsz, L = q.shape[:2]
    nb = L // WINDOW
    G = H_B // KV_B
    qb = q.reshape(Bsz, nb, WINDOW, KV_B, G, HD_B)

    def with_prev(a):
        a = a.reshape(Bsz, nb, WINDOW, KV_B, HD_B)
        prev = jnp.concatenate([jnp.zeros_like(a[:, :1]), a[:, :-1]], axis=1)
        return jnp.concatenate([prev, a], axis=2)

    k2, v2 = with_prev(k), with_prev(v)
    s = jnp.einsum('bnqkgd,bnskd->bnkgqs', qb, k2).astype(jnp.float32) * SCALE_B
    rel = jnp.arange(WINDOW)[:, None] + WINDOW - jnp.arange(2 * WINDOW)[None, :]
    band = (rel >= 0) & (rel <= WINDOW)
    has_prev = (jnp.arange(nb)[:, None, None] > 0) | (jnp.arange(2 * WINDOW) >= WINDOW)[None, None, :]
    valid = band[None] & has_prev
    s = jnp.where(valid[None, :, None, None], s, -jnp.inf)
    p = sink_softmax(s, sinks)
    o = jnp.einsum('bnkgqs,bnskd->bnqkgd', p.astype(v2.dtype), v2)
    return o.reshape(Bsz, L, H_B * HD_B)


def swa_sample(q, k, v, buf_k, buf_v, sinks):
    Bsz, L = q.shape[:2]
    W = buf_k.shape[1]
    G = H_B // KV_B
    kk = jnp.concatenate([buf_k.astype(k.dtype), k], axis=1)
    vv = jnp.concatenate([buf_v.astype(v.dtype), v], axis=1)
    s = jnp.einsum('bqkgd,bskd->bkgqs', q.reshape(Bsz, L, KV_B, G, HD_B), kk).astype(jnp.float32) * SCALE_B
    rel = jnp.arange(L)[:, None] + W - jnp.arange(W + L)[None, :]
    s = jnp.where((rel >= 0) & (rel <= WINDOW), s, -jnp.inf)
    p = sink_softmax(s, sinks)
    o = jnp.einsum('bkgqs,bskd->bqkgd', p.astype(vv.dtype), vv)
    return o.reshape(Bsz, L, H_B * HD_B), kk[:, -W:], vv[:, -W:]


def hgrn_mixer(qr, fr, ir, gr, s0, lb, gnorm_w):
    Bsz, L = qr.shape[:2]
    q = jax.nn.silu(qr).reshape(Bsz, L, H_C, DK_C)
    fgate = lb + (1.0 - lb) * jax.nn.sigmoid(fr.astype(jnp.float32))
    logf = jnp.log(fgate).reshape(Bsz, L, H_C, DK_C)
    key = (1.0 - fgate).reshape(Bsz, L, H_C, DK_C)
    o, s1 = gla_chunked(q, key, ir.reshape(Bsz, L, H_C, DV_C), logf, s0)
    o = rms_norm(o, gnorm_w.reshape(H_C, DV_C)).reshape(Bsz, L, H_C * DV_C)
    return o * jax.nn.silu(gr), s1


def fox_prompt(q, k, v, logf):
    Bsz, L = q.shape[:2]
    G = H_D // KV_D
    nb = L // Q_BLOCK_D
    cum = jnp.cumsum(logf, axis=1).reshape(Bsz, L, KV_D, G)
    cum_keys = jnp.transpose(cum, (0, 2, 3, 1))[:, :, :, None, :]
    q_blocks = jnp.moveaxis(q.reshape(Bsz, nb, Q_BLOCK_D, KV_D, G, HD_D), 1, 0)
    c_blocks = jnp.moveaxis(cum.reshape(Bsz, nb, Q_BLOCK_D, KV_D, G), 1, 0)
    key_pos = jnp.arange(L)

    def block(args):
        qb, cb, bi = args
        q_pos = bi * Q_BLOCK_D + jnp.arange(Q_BLOCK_D)
        s = jnp.einsum('bqkgd,bskd->bkgqs', qb, k).astype(jnp.float32) * SCALE_D
        s = s + jnp.transpose(cb, (0, 2, 3, 1))[..., None] - cum_keys
        s = jnp.where(key_pos[None, :] <= q_pos[:, None], s, -jnp.inf)
        p = jax.nn.softmax(s, axis=-1)
        return jnp.einsum('bkgqs,bskd->bqkgd', p.astype(v.dtype), v)

    o = lax.map(block, (q_blocks, c_blocks, jnp.arange(nb)))
    return jnp.moveaxis(o, 0, 1).reshape(Bsz, L, H_D * HD_D)


def fox_sample(q, k, v, logf, pool_k, pool_v, pool_lf, layer, page_table):
    L = q.shape[1]
    n_past = page_table.shape[1] * pool_k.shape[2]
    tot = n_past + L
    G = H_D // KV_D
    valid = jnp.arange(tot)[None, :] <= (n_past + jnp.arange(L))[:, None]

    def one(args):
        qs, ks, vs, lfs, pt = args
        kk = jnp.concatenate([pool_k[layer, pt].reshape(n_past, KV_D, HD_D).astype(ks.dtype), ks], axis=0)
        vv = jnp.concatenate([pool_v[layer, pt].reshape(n_past, KV_D, HD_D).astype(vs.dtype), vs], axis=0)
        lf_all = jnp.concatenate([pool_lf[layer, pt].reshape(n_past, H_D).astype(jnp.float32), lfs], axis=0)
        cum = jnp.cumsum(lf_all, axis=0)
        bias = (cum[n_past:][:, None, :] - cum[None, :, :]).reshape(L, tot, KV_D, G).transpose(2, 3, 0, 1)
        s = jnp.einsum('qkgd,skd->kgqs', qs.reshape(L, KV_D, G, HD_D), kk).astype(jnp.float32) * SCALE_D + bias
        p = jax.nn.softmax(jnp.where(valid, s, -jnp.inf), axis=-1)
        o = jnp.einsum('kgqs,skd->qkgd', p.astype(vv.dtype), vv)
        return o.reshape(L, H_D * HD_D)

    return lax.map(one, (q, k, v, logf, page_table))


def conv_ffn(x, buf, w_in, conv_w, conv_b, w_out):
    u, g = jnp.split(x @ w_in, 2, axis=-1)
    g, buf1 = causal_dwconv(g, buf, conv_w, conv_b)
    return (jax.nn.silu(g) * u) @ w_out, buf1


def trunk(x, c, st, prm, page_table):
    sample = page_table is not None
    f32 = jnp.float32
    Bsz, L, _ = x.shape
    sm = jax.nn.softmax(prm['lb_c'].astype(f32), axis=0)
    lb_all = jnp.cumsum(sm, axis=0) - sm[0]
    new = {name: [] for name in STATE_NAMES}
    for l in range(DEPTH):
        j = l // 2
        shift, scale, gate = ada_mod(c, prm['ada_w'][l, 0], prm['ada_b'][l, 0])
        h = x * (1.0 + scale) + shift
        if l % 2 == 0:
            z, xbc, dt_raw, qb, kb, vb = split_cols(h @ prm['w_in_ab'][j], SIZES_AB)
            conv0 = st['conv_a'][j] if sample else jnp.zeros((Bsz, CONV_A - 1, CONV_DIM_A), x.dtype)
            ssm0 = st['ssm_a'][j] if sample else jnp.zeros((Bsz, H_A, N_A, P_A), f32)
            ya, conv1, ssm1 = mamba_mixer(z, xbc, dt_raw, conv0, ssm0, prm['conv_w_a'][j], prm['conv_b_a'][j],
                                          prm['dt_bias_a'][j], prm['a_log_a'][j], prm['d_skip_a'][j],
                                          prm['norm_w_a'][j])
            q = qb.reshape(Bsz, L, H_B, HD_B)
            k = kb.reshape(Bsz, L, KV_B, HD_B)
            v = vb.reshape(Bsz, L, KV_B, HD_B)
            if sample:
                yb, bk, bv = swa_sample(q, k, v, st['swa_k'][j], st['swa_v'][j], prm['sinks_b'][j])
            else:
                yb, bk, bv = swa_prompt(q, k, v, prm['sinks_b'][j]), k[:, -WINDOW:], v[:, -WINDOW:]
            mix = jnp.concatenate([ya.astype(x.dtype), yb.astype(x.dtype)], axis=-1) @ prm['w_out_ab'][j]
            new['ssm_a'].append(ssm1)
            new['conv_a'].append(conv1)
            new['swa_k'].append(bk)
            new['swa_v'].append(bv)
        else:
            qc, fc, ic, gc, qd, kd, vd, fd = split_cols(h @ prm['w_in_cd'][j], SIZES_CD)
            s0 = st['hgrn_c'][j] if sample else jnp.zeros((Bsz, H_C, DK_C, DV_C), f32)
            yc, s1 = hgrn_mixer(qc, fc, ic, gc, s0, lb_all[l], prm['gnorm_c'][j])
            q = qd.reshape(Bsz, L, H_D, HD_D)
            k = kd.reshape(Bsz, L, KV_D, HD_D)
            v = vd.reshape(Bsz, L, KV_D, HD_D)
            logf = jax.nn.log_sigmoid(fd.astype(f32) + prm['fbias_d'][j])
            if sample:
                yd = fox_sample(q, k, v, logf, st['fox_k'], st['fox_v'], st['fox_logf'], j, page_table)
            else:
                yd = fox_prompt(q, k, v, logf)
            mix = jnp.concatenate([yc.astype(x.dtype), yd.astype(x.dtype)], axis=-1) @ prm['w_out_cd'][j]
            new['hgrn_c'].append(s1)
            new['fox_k'].append(k)
            new['fox_v'].append(v)
            new['fox_logf'].append(logf)
        x = layer_norm(ALPHA * x + (1.0 + gate) * mix, prm['ln_g'][l, 0], prm['ln_b'][l, 0])
        shift, scale, gate = ada_mod(c, prm['ada_w'][l, 1], prm['ada_b'][l, 1])
        h = x * (1.0 + scale) + shift
        buf0 = st['ffn_conv'][l] if sample else jnp.zeros((Bsz, FFN_CONV - 1, D_FF), x.dtype)
        f_out, buf1 = conv_ffn(h, buf0, prm['ffn_w_in'][l], prm['ffn_conv_w'][l], prm['ffn_conv_b'][l],
                               prm['ffn_w_out'][l])
        new['ffn_conv'].append(buf1)
        x = layer_norm(ALPHA * x + (1.0 + gate) * f_out, prm['ln_g'][l, 1], prm['ln_b'][l, 1])
    return x, {name: jnp.stack(rows, axis=0) for name, rows in new.items()}


def setup_inputs(seed: int = 0) -> dict:
    key = jax.random.key(seed)
    keys = iter(jax.random.split(key, 48))
    f32 = jnp.float32

    def nrm(shape, s=1.0):
        return s * jax.random.normal(next(keys), shape, f32)

    n_pages = PAST_LEN // PAGE_SIZE
    n_used = DEC_BATCH * n_pages
    n_pool = n_used + max(1, n_used // 4)
    w_buf = min(WINDOW, PAST_LEN)
    page_table = jax.random.permutation(next(keys), n_pool)[:n_used].reshape(DEC_BATCH, n_pages).astype(jnp.int32)
    dt0 = jnp.exp(jax.random.uniform(next(keys), (N_AB, H_A), f32, math.log(1e-3), math.log(1e-1)))
    return {
        'x_prompt': nrm((BATCH, SEQ, D_MODEL)),
        'x_sample': nrm((DEC_BATCH, DEC_SEQ, D_MODEL)),
        'state_ssm_a': nrm((N_AB, DEC_BATCH, H_A, N_A, P_A), 0.1),
        'state_conv_a': nrm((N_AB, DEC_BATCH, CONV_A - 1, CONV_DIM_A)),
        'cache_swa_k': nrm((N_AB, DEC_BATCH, w_buf, KV_B, HD_B)),
        'cache_swa_v': nrm((N_AB, DEC_BATCH, w_buf, KV_B, HD_B)),
        'state_hgrn_c': nrm((N_CD, DEC_BATCH, H_C, DK_C, DV_C), 0.3),
        'cache_fox_k': nrm((N_CD, n_pool, PAGE_SIZE, KV_D, HD_D)),
        'cache_fox_v': nrm((N_CD, n_pool, PAGE_SIZE, KV_D, HD_D)),
        'cache_fox_logf': jax.nn.log_sigmoid(nrm((N_CD, n_pool, PAGE_SIZE, H_D)) + 2.0),
        'state_ffn_conv': nrm((DEPTH, DEC_BATCH, FFN_CONV - 1, D_FF)),
        'page_table': page_table,
        'c_prompt': nrm((BATCH, D_MODEL)),
        'c_sample': nrm((DEC_BATCH, D_MODEL)),
        'ada_w': nrm((DEPTH, 2, D_MODEL, 3 * D_MODEL), 0.3 * D_MODEL ** -0.5),
        'ada_b': nrm((DEPTH, 2, 3 * D_MODEL), 0.02),
        'ln_g': 1.0 + nrm((DEPTH, 2, D_MODEL), 0.05),
        'ln_b': nrm((DEPTH, 2, D_MODEL), 0.02),
        'w_in_ab': nrm((N_AB, D_MODEL, sum(SIZES_AB)), D_MODEL ** -0.5),
        'w_out_ab': nrm((N_AB, MIX_AB, D_MODEL), BETA * MIX_AB ** -0.5),
        'conv_w_a': nrm((N_AB, CONV_A, CONV_DIM_A), CONV_A ** -0.5),
        'conv_b_a': nrm((N_AB, CONV_DIM_A), 0.02),
        'dt_bias_a': dt0 + jnp.log(-jnp.expm1(-dt0)),
        'a_log_a': jnp.log(jax.random.uniform(next(keys), (N_AB, H_A), f32, 1.0, 16.0)),
        'd_skip_a': 1.0 + nrm((N_AB, H_A), 0.1),
        'norm_w_a': 1.0 + nrm((N_AB, D_INNER_A), 0.05),
        'sinks_b': nrm((N_AB, H_B)),
        'w_in_cd': nrm((N_CD, D_MODEL, sum(SIZES_CD)), D_MODEL ** -0.5),
        'w_out_cd': nrm((N_CD, MIX_CD, D_MODEL), BETA * MIX_CD ** -0.5),
        'lb_c': nrm((DEPTH, H_C * DK_C)),
        'gnorm_c': 1.0 + nrm((N_CD, H_C * DV_C), 0.05),
        'fbias_d': 1.0 + nrm((N_CD, H_D), 0.5),
        'ffn_w_in': nrm((DEPTH, D_MODEL, 2 * D_FF), D_MODEL ** -0.5),
        'ffn_conv_w': nrm((DEPTH, FFN_CONV, D_FF), FFN_CONV ** -0.5),
        'ffn_conv_b': nrm((DEPTH, D_FF), 0.02),
        'ffn_w_out': nrm((DEPTH, D_FF, D_MODEL), BETA * D_FF ** -0.5),
    }


def reference(x_prompt, x_sample, state_ssm_a, state_conv_a, cache_swa_k, cache_swa_v, state_hgrn_c,
              cache_fox_k, cache_fox_v, cache_fox_logf, state_ffn_conv, page_table, c_prompt, c_sample,
              ada_w, ada_b, ln_g, ln_b, w_in_ab, w_out_ab, conv_w_a, conv_b_a, dt_bias_a, a_log_a,
              d_skip_a, norm_w_a, sinks_b, w_in_cd, w_out_cd, lb_c, gnorm_c, fbias_d,
              ffn_w_in, ffn_conv_w, ffn_conv_b, ffn_w_out):
    prm = dict(ada_w=ada_w, ada_b=ada_b, ln_g=ln_g, ln_b=ln_b, w_in_ab=w_in_ab, w_out_ab=w_out_ab,
               conv_w_a=conv_w_a, conv_b_a=conv_b_a, dt_bias_a=dt_bias_a, a_log_a=a_log_a,
               d_skip_a=d_skip_a, norm_w_a=norm_w_a, sinks_b=sinks_b, w_in_cd=w_in_cd, w_out_cd=w_out_cd,
               lb_c=lb_c, gnorm_c=gnorm_c, fbias_d=fbias_d, ffn_w_in=ffn_w_in, ffn_conv_w=ffn_conv_w,
               ffn_conv_b=ffn_conv_b, ffn_w_out=ffn_w_out)
    st = dict(ssm_a=state_ssm_a, conv_a=state_conv_a, swa_k=cache_swa_k, swa_v=cache_swa_v,
              hgrn_c=state_hgrn_c, fox_k=cache_fox_k, fox_v=cache_fox_v, fox_logf=cache_fox_logf,
              ffn_conv=state_ffn_conv)
    y_prompt, sp = trunk(x_prompt, c_prompt, None, prm, None)
    y_sample, ss = trunk(x_sample, c_sample, st, prm, page_table)
    return (y_prompt, y_sample,
            sp['ssm_a'], ss['ssm_a'], sp['conv_a'], ss['conv_a'],
            sp['swa_k'], ss['swa_k'], sp['swa_v'], ss['swa_v'],
            sp['hgrn_c'], ss['hgrn_c'],
            sp['fox_k'], ss['fox_k'], sp['fox_v'], ss['fox_v'], sp['fox_logf'], ss['fox_logf'],
            sp['ffn_conv'], ss['ffn_conv'])
```

```python
import functools
import math

import numpy as np
import jax
import jax.numpy as jnp
from jax import lax
from jax.experimental import pallas as pl
from jax.experimental.pallas import tpu as pltpu

F32 = jnp.float32
BF16 = jnp.bfloat16
NEG_INF = float("-inf")

D_MODEL = 1024
DEPTH = 2
PAGE_SIZE = 128
H_A, P_A, N_A, G_A, CONV_A = 16, 64, 128, 2, 4
D_INNER_A = H_A * P_A
CONV_DIM_A = D_INNER_A + 2 * G_A * N_A
H_B, KV_B, HD_B, WINDOW = 8, 2, 64, 128
H_C, DK_C, DV_C = 4, 128, 128
H_D, KV_D, HD_D = 16, 4, 64
D_FF, FFN_CONV = 2816, 3
ALPHA = (2 * DEPTH) ** 0.25
SCALE_B = HD_B ** -0.5
SCALE_D = HD_D ** -0.5
LN_EPS = 1e-5
RMS_EPS = 1e-6

LANES = 128
SUBLANES = 8
VMEM_LIMIT_BYTES = 56 * 1024 * 1024

SSD_CHUNK = 128
GLA_CHUNK = 64
FOX_TQ = 512
FOX_TK = 512
FOX_CUM_T = 512
FOX_PAGES_PER_STEP = 8
ROW_TILE = 512
FFN_ROW_TILE = 256
SAMPLE_BATCH_TILE = 32


def _params(*sem):
    return pltpu.CompilerParams(dimension_semantics=sem, vmem_limit_bytes=VMEM_LIMIT_BYTES)


def _resident(shape):
    nd = len(shape)
    return pl.BlockSpec(shape, lambda *_: (0,) * nd, pipeline_mode=pl.Buffered(1))


def _silu(x):
    return x * jax.nn.sigmoid(x)


def _softplus(x):
    return jnp.maximum(x, 0.0) + jnp.log1p(jnp.exp(-jnp.abs(x)))


def _dot(a, b):
    return jnp.dot(a, b, preferred_element_type=F32)


def _dot_nt(a, b):
    return lax.dot_general(a, b, (((1,), (1,)), ((), ())), preferred_element_type=F32)


def _dot_tn(a, b):
    return lax.dot_general(a, b, (((0,), (0,)), ((), ())), preferred_element_type=F32)


def _split3(x):
    hi = x.astype(BF16)
    r = x - hi.astype(F32)
    mid = r.astype(BF16)
    lo = (r - mid.astype(F32)).astype(BF16)
    return hi, mid, lo


def _exact_dot_lhs(x, w01):
    hi, mid, lo = _split3(x)
    return _dot(hi, w01) + _dot(mid, w01) + _dot(lo, w01)


def _exact_dot_rhs(w01, x):
    hi, mid, lo = _split3(x)
    return _dot(w01, hi) + _dot(w01, mid) + _dot(w01, lo)


def _lower_tri_bf16(t):
    r = lax.broadcasted_iota(jnp.int32, (t, t), 0)
    c = lax.broadcasted_iota(jnp.int32, (t, t), 1)
    return jnp.where(r >= c, 1.0, 0.0).astype(BF16)


def _layer_norm_rows(r, g, b):
    mu = jnp.mean(r, axis=-1, keepdims=True)
    d = r - mu
    var = jnp.mean(d * d, axis=-1, keepdims=True)
    return d * lax.rsqrt(var + LN_EPS) * g + b


def _ada_kernel(c_ref, w_ref, b_ref, o_ref):
    s = _silu(c_ref[...]).astype(BF16)
    o_ref[0] = _dot(s, w_ref[0].astype(BF16)) + b_ref[0]


def _ada_all(c, ada_w, ada_b):
    bsz = c.shape[0]
    n_sub = ada_w.shape[0]
    n_out = ada_w.shape[2]
    tn = D_MODEL
    return pl.pallas_call(
        _ada_kernel,
        out_shape=jax.ShapeDtypeStruct((n_sub, bsz, n_out), F32),
        grid=(n_sub, n_out // tn),
        in_specs=[pl.BlockSpec((bsz, D_MODEL), lambda s, j: (0, 0)),
                  pl.BlockSpec((1, D_MODEL, tn), lambda s, j: (s, 0, j)),
                  pl.BlockSpec((1, 1, tn), lambda s, j: (s, 0, j))],
        out_specs=pl.BlockSpec((1, bsz, tn), lambda s, j: (s, 0, j)),
        compiler_params=_params("parallel", "parallel"),
        name="ada_mod",
    )(c, ada_w, ada_b)


def _mod_matmul_kernel(x_ref, m_ref, w_ref, *o_refs, bb, tl, sizes):
    d = x_ref.shape[-1]
    m = m_ref[...]
    h = x_ref[...] * (1.0 + m[:, :, d:2 * d]) + m[:, :, 0:d]
    h2 = h.reshape(bb * tl, d).astype(BF16)
    off = 0
    for o_ref, n in zip(o_refs, sizes):
        o_ref[...] = _dot(h2, w_ref[:, off:off + n]).reshape(bb, tl, n)
        off += n


def _mod_matmul(x, m, w_bf16, sizes, bb, tl):
    bsz, seq, d = x.shape
    n_tot = w_bf16.shape[1]
    kern = functools.partial(_mod_matmul_kernel, bb=bb, tl=tl, sizes=tuple(sizes))
    return pl.pallas_call(
        kern,
        out_shape=[jax.ShapeDtypeStruct((bsz, seq, n), F32) for n in sizes],
        grid=(bsz // bb, seq // tl),
        in_specs=[pl.BlockSpec((bb, tl, d), lambda i, j: (i, j, 0)),
                  pl.BlockSpec((bb, 1, 3 * d), lambda i, j: (i, 0, 0)),
                  _resident((d, n_tot))],
        out_specs=[pl.BlockSpec((bb, tl, n), lambda i, j: (i, j, 0)) for n in sizes],
        compiler_params=_params("parallel", "parallel"),
        name="mod_matmul",
    )(x, m, w_bf16)


def _mm_res_ln_kernel(y_ref, w_ref, x_ref, m_ref, g_ref, b_ref, o_ref, *, bb, tl):
    k = y_ref.shape[-1]
    d = x_ref.shape[-1]
    y2 = y_ref[...].reshape(bb * tl, k).astype(BF16)
    mix = _dot(y2, w_ref[...]).reshape(bb, tl, d)
    gate = m_ref[:, :, 2 * d:3 * d]
    r = ALPHA * x_ref[...] + (1.0 + gate) * mix
    o_ref[...] = _layer_norm_rows(r, g_ref[...], b_ref[...])


def _mm_res_ln(y, w_bf16, x, m, g, b, bb, tl):
    bsz, seq, k = y.shape
    d = x.shape[-1]
    kern = functools.partial(_mm_res_ln_kernel, bb=bb, tl=tl)
    return pl.pallas_call(
        kern,
        out_shape=jax.ShapeDtypeStruct((bsz, seq, d), F32),
        grid=(bsz // bb, seq // tl),
        in_specs=[pl.BlockSpec((bb, tl, k), lambda i, j: (i, j, 0)),
                  _resident((k, d)),
                  pl.BlockSpec((bb, tl, d), lambda i, j: (i, j, 0)),
                  pl.BlockSpec((bb, 1, 3 * d), lambda i, j: (i, 0, 0)),
                  pl.BlockSpec((1, d), lambda i, j: (0, 0)),
                  pl.BlockSpec((1, d), lambda i, j: (0, 0))],
        out_specs=pl.BlockSpec((bb, tl, d), lambda i, j: (i, j, 0)),
        compiler_params=_params("parallel", "parallel"),
        name="mm_res_ln",
    )(y, w_bf16, x, m, g, b)


def _ffn_in_kernel(x_ref, m_ref, w_ref, buf0_ref, cw_ref, cb_ref, a_ref, buf1_ref, carry_scr, *, bb, tl, n_col):
    d = x_ref.shape[-1]
    dff = a_ref.shape[-1]
    j = pl.program_id(1)

    @pl.when(j == 0)
    def _():
        carry_scr[...] = buf0_ref[...]

    m = m_ref[...]
    h = x_ref[...] * (1.0 + m[:, :, d:2 * d]) + m[:, :, 0:d]
    h2 = h.reshape(bb * tl, d).astype(BF16)
    rows = bb * tl
    cw = dff // n_col
    t = lax.broadcasted_iota(jnp.int32, (rows, 1), 0) % tl
    for ci in range(n_col):
        lo = ci * cw
        u = _dot(h2, w_ref[:, lo:lo + cw])
        g = _dot(h2, w_ref[:, dff + lo:dff + lo + cw])
        carry = carry_scr[:, :, lo:lo + cw]
        prev1 = jnp.broadcast_to(carry[:, 1:2, :], (bb, tl, cw)).reshape(rows, cw)
        prev0 = jnp.broadcast_to(carry[:, 0:1, :], (bb, tl, cw)).reshape(rows, cw)
        tap1 = jnp.where(t == 0, prev1, pltpu.roll(g, 1, axis=0))
        tap2 = jnp.where(t == 0, prev0, jnp.where(t == 1, prev1, pltpu.roll(g, 2, axis=0)))
        y = cb_ref[:, lo:lo + cw] + g * cw_ref[2:3, lo:lo + cw]
        y = y + tap2 * cw_ref[0:1, lo:lo + cw]
        y = y + tap1 * cw_ref[1:2, lo:lo + cw]
        a_ref[:, :, lo:lo + cw] = (_silu(y) * u).reshape(bb, tl, cw)
        last2 = g.reshape(bb, tl, cw)[:, tl - (FFN_CONV - 1):, :]
        carry_scr[:, :, lo:lo + cw] = last2
        buf1_ref[:, :, lo:lo + cw] = last2


def _ffn_in(x, m, w_bf16, buf0, conv_w, conv_b, bb, tl):
    bsz, seq, d = x.shape
    kern = functools.partial(_ffn_in_kernel, bb=bb, tl=tl, n_col=2)
    return pl.pallas_call(
        kern,
        out_shape=[jax.ShapeDtypeStruct((bsz, seq, D_FF), F32),
                   jax.ShapeDtypeStruct((bsz, FFN_CONV - 1, D_FF), F32)],
        grid=(bsz // bb, seq // tl),
        in_specs=[pl.BlockSpec((bb, tl, d), lambda i, j: (i, j, 0)),
                  pl.BlockSpec((bb, 1, 3 * d), lambda i, j: (i, 0, 0)),
                  _resident((d, 2 * D_FF)),
                  pl.BlockSpec((bb, FFN_CONV - 1, D_FF), lambda i, j: (i, 0, 0)),
                  pl.BlockSpec((FFN_CONV, D_FF), lambda i, j: (0, 0)),
                  pl.BlockSpec((1, D_FF), lambda i, j: (0, 0))],
        out_specs=[pl.BlockSpec((bb, tl, D_FF), lambda i, j: (i, j, 0)),
                   pl.BlockSpec((bb, FFN_CONV - 1, D_FF), lambda i, j: (i, 0, 0))],
        scratch_shapes=[pltpu.VMEM((bb, FFN_CONV - 1, D_FF), F32)],
        compiler_params=_params("parallel", "arbitrary"),
        name="ffn_in_conv",
    )(x, m, w_bf16, buf0, conv_w, conv_b)


def _mamba_kernel(*refs, T, has_state, n_chunks):
    if has_state:
        (z_ref, xbc_ref, dt_ref, conv0_ref, ssm0_ref, cw_ref, cb_ref, dtb_ref, alog_ref, dsk_ref, nw_ref, e_ref,
         y_ref, conv1_ref, ssm1_ref, carry_scr, s_scr) = refs
    else:
        (z_ref, xbc_ref, dt_ref, cw_ref, cb_ref, dtb_ref, alog_ref, dsk_ref, nw_ref, e_ref,
         y_ref, conv1_ref, ssm1_ref, carry_scr, s_scr) = refs
    c = pl.program_id(1)
    n_pair = H_A // 2
    cdim = CONV_DIM_A

    @pl.when(c == 0)
    def _():
        if has_state:
            carry_scr[...] = conv0_ref[0]
            for j in range(n_pair):
                s_scr[j] = jnp.concatenate([ssm0_ref[0, 2 * j], ssm0_ref[0, 2 * j + 1]], axis=1)
        else:
            carry_scr[...] = jnp.zeros_like(carry_scr)
            s_scr[...] = jnp.zeros_like(s_scr)

    xb = xbc_ref[0]
    carry = carry_scr[...]
    row8 = lax.broadcasted_iota(jnp.int32, (SUBLANES, cdim), 0)
    acc = cb_ref[...] + xb * cw_ref[CONV_A - 1:CONV_A, :]
    for dshift in (3, 2, 1):
        rolled = pltpu.roll(xb, dshift, axis=0)
        top = jnp.where(row8 < dshift, pltpu.roll(carry, dshift, axis=0), rolled[0:SUBLANES])
        sh = top if T == SUBLANES else jnp.concatenate([top, rolled[SUBLANES:]], axis=0)
        acc = acc + sh * cw_ref[CONV_A - 1 - dshift:CONV_A - dshift, :]
    last8 = xb[T - SUBLANES:T, :]
    carry_scr[...] = last8
    conv1_ref[0] = last8
    xc = _silu(acc)
    xs = xc[:, 0:D_INNER_A]
    bm = xc[:, D_INNER_A:D_INNER_A + G_A * N_A]
    cm = xc[:, D_INNER_A + G_A * N_A:cdim]

    dt = _softplus(dt_ref[0] + dtb_ref[...])
    la = dt * (-jnp.exp(alog_ref[...]))
    tri = _lower_tri_bf16(T)
    cum = _exact_dot_rhs(tri, la)
    e = e_ref[...]
    dtx = _exact_dot_lhs(dt, e)
    cumx = _exact_dot_lhs(cum, e)
    x = xs * dtx
    decx = jnp.exp(cumx)
    lastx = cumx[T - 1:T, :]
    xd = (x * jnp.exp(lastx - cumx)).astype(BF16)
    elastx = jnp.exp(lastx)
    cum_t = cum.T

    rr = lax.broadcasted_iota(jnp.int32, (T, T), 0)
    cc = lax.broadcasted_iota(jnp.int32, (T, T), 1)
    causal = rr >= cc
    lane = lax.broadcasted_iota(jnp.int32, (1, 2 * P_A), 1)
    low = lane < P_A

    cb_g, c_g, b_g = [], [], []
    for g in range(G_A):
        cg = cm[:, g * N_A:(g + 1) * N_A].astype(BF16)
        bg = bm[:, g * N_A:(g + 1) * N_A].astype(BF16)
        c_g.append(cg)
        b_g.append(bg)
        cb_g.append(_dot_nt(cg, bg))

    heads_per_group = H_A // G_A
    outs = []
    for j in range(n_pair):
        g = (2 * j) // heads_per_group
        sl = slice(j * 2 * P_A, (j + 1) * 2 * P_A)
        xj = x[:, sl]
        yj = None
        for eidx in (0, 1):
            h = 2 * j + eidx
            col = cum[:, h:h + 1]
            row = cum_t[h:h + 1, :]
            decay = jnp.exp(jnp.where(causal, col - row, NEG_INF))
            scores = (cb_g[g] * decay).astype(BF16)
            xm = jnp.where(low if eidx == 0 else jnp.logical_not(low), xj, 0.0).astype(BF16)
            part = _dot(scores, xm)
            yj = part if yj is None else yj + part
        s_old = s_scr[j]
        yj = yj + _dot(c_g[g], s_old.astype(BF16)) * decx[:, sl]
        s_scr[j] = elastx[:, sl] * s_old + _dot_tn(b_g[g], xd[:, sl])
        outs.append(yj)
    y = jnp.concatenate(outs, axis=1) + dsk_ref[...] * xs
    y = y * _silu(z_ref[0])
    y_ref[0] = y * lax.rsqrt(jnp.mean(y * y, axis=-1, keepdims=True) + RMS_EPS) * nw_ref[...]

    @pl.when(c == n_chunks - 1)
    def _():
        for j in range(n_pair):
            sj = s_scr[j]
            ssm1_ref[0, 2 * j] = sj[:, 0:P_A]
            ssm1_ref[0, 2 * j + 1] = sj[:, P_A:2 * P_A]


def _mamba(z, xbc, dtp, conv0p, ssm0, conv_w, conv_b, dt_bias, a_log, d_skip, norm_w, T):
    bsz, seq, _ = z.shape
    n_chunks = seq // T
    has_state = conv0p is not None
    dtb = jnp.zeros((1, LANES), F32).at[0, :H_A].set(dt_bias)
    alog = jnp.zeros((1, LANES), F32).at[0, :H_A].set(a_log)
    dsk = jnp.repeat(d_skip, P_A)[None, :]
    head_of_col = np.arange(D_INNER_A) // P_A
    expand = jnp.asarray((np.arange(LANES)[:, None] == head_of_col[None, :]).astype(np.float32), dtype=BF16)
    kern = functools.partial(_mamba_kernel, T=T, has_state=has_state, n_chunks=n_chunks)
    tok = lambda w: pl.BlockSpec((1, T, w), lambda b, c: (b, c, 0))
    const = lambda shape: pl.BlockSpec(shape, lambda b, c: (0,) * len(shape))
    in_specs = [tok(D_INNER_A), tok(CONV_DIM_A), tok(LANES)]
    args = [z, xbc, dtp]
    if has_state:
        in_specs += [pl.BlockSpec((1, SUBLANES, CONV_DIM_A), lambda b, c: (b, 0, 0)),
                     pl.BlockSpec((1, H_A, N_A, P_A), lambda b, c: (b, 0, 0, 0))]
        args += [conv0p, ssm0]
    in_specs += [const((CONV_A, CONV_DIM_A)), const((1, CONV_DIM_A)), const((1, LANES)), const((1, LANES)),
                 const((1, D_INNER_A)), const((1, D_INNER_A)), const((LANES, D_INNER_A))]
    args += [conv_w, conv_b[None, :], dtb, alog, dsk, norm_w[None, :], expand]
    return pl.pallas_call(
        kern,
        out_shape=[jax.ShapeDtypeStruct((bsz, seq, D_INNER_A), F32),
                   jax.ShapeDtypeStruct((bsz, SUBLANES, CONV_DIM_A), F32),
                   jax.ShapeDtypeStruct((bsz, H_A, N_A, P_A), F32)],
        grid=(bsz, n_chunks),
        in_specs=in_specs,
        out_specs=[tok(D_INNER_A),
                   pl.BlockSpec((1, SUBLANES, CONV_DIM_A), lambda b, c: (b, 0, 0)),
                   pl.BlockSpec((1, H_A, N_A, P_A), lambda b, c: (b, 0, 0, 0))],
        scratch_shapes=[pltpu.VMEM((SUBLANES, CONV_DIM_A), F32),
                        pltpu.VMEM((H_A // 2, N_A, 2 * P_A), F32)],
        compiler_params=_params("parallel", "arbitrary"),
        name="mamba_ssd",
    )(*args)


def _swa_kernel(sinks_ref, q_ref, kp_ref, kc_ref, vp_ref, vc_ref, o_ref, *, TQ, first_has_prev):
    n = pl.program_id(1)
    q = q_ref[0]
    kp, kc, vp, vc = kp_ref[0], kc_ref[0], vp_ref[0], vc_ref[0]
    lane = lax.broadcasted_iota(jnp.int32, (1, 2 * HD_B), 1)
    low = lane < HD_B
    roll64 = lambda a: pltpu.roll(a, HD_B, axis=1)
    same = [a.astype(BF16) for a in (kp, kc, vp, vc)]
    rolled = [roll64(a).astype(BF16) for a in (kp, kc, vp, vc)]
    has_prev = jnp.logical_or(n > 0, first_has_prev)
    ip = lax.broadcasted_iota(jnp.int32, (TQ, WINDOW), 0)
    jp = lax.broadcasted_iota(jnp.int32, (TQ, WINDOW), 1)
    valid_prev = jnp.logical_and(jp >= ip, has_prev)
    ic = lax.broadcasted_iota(jnp.int32, (TQ, TQ), 0)
    jc = lax.broadcasted_iota(jnp.int32, (TQ, TQ), 1)
    valid_cur = jc <= ic
    group = H_B // KV_B
    outs = []
    for j in range(H_B // 2):
        qp = q[:, j * 2 * HD_B:(j + 1) * 2 * HD_B] * SCALE_B
        res = []
        for eidx in (0, 1):
            h = 2 * j + eidx
            kv = h // group
            qm = jnp.where(low if eidx == 0 else jnp.logical_not(low), qp, 0.0).astype(BF16)
            kpv, kcv, vpv, vcv = same if kv == eidx else rolled
            sp = jnp.where(valid_prev, _dot_nt(qm, kpv), NEG_INF)
            sc = jnp.where(valid_cur, _dot_nt(qm, kcv), NEG_INF)
            sink = sinks_ref[h]
            mx = jnp.maximum(jnp.maximum(jnp.max(sp, axis=-1, keepdims=True),
                                         jnp.max(sc, axis=-1, keepdims=True)), sink)
            pp = jnp.exp(sp - mx)
            pc = jnp.exp(sc - mx)
            den = (jnp.sum(pp, axis=-1, keepdims=True) + jnp.sum(pc, axis=-1, keepdims=True)
                   + jnp.exp(sink - mx))
            o = _dot(pp.astype(BF16), vpv) + _dot(pc.astype(BF16), vcv)
            res.append(o / den)
        outs.append(jnp.where(low, res[0], res[1]))
    o_ref[0] = jnp.concatenate(outs, axis=1)


def _swa(q, k_prev, k_cur, v_prev, v_cur, sinks, TQ, prev_is_same_array):
    bsz, seq, _ = q.shape
    nb = seq // TQ
    kvw = KV_B * HD_B
    if prev_is_same_array:
        prev_map = lambda b, n: (b, jnp.maximum(n - 1, 0), 0)
    else:
        prev_map = lambda b, n: (b, 0, 0)
    kern = functools.partial(_swa_kernel, TQ=TQ, first_has_prev=not prev_is_same_array)
    return pl.pallas_call(
        kern,
        out_shape=jax.ShapeDtypeStruct((bsz, seq, H_B * HD_B), F32),
        grid=(bsz, nb),
        in_specs=[pl.BlockSpec(memory_space=pltpu.SMEM),
                  pl.BlockSpec((1, TQ, H_B * HD_B), lambda b, n: (b, n, 0)),
                  pl.BlockSpec((1, WINDOW, kvw), prev_map),
                  pl.BlockSpec((1, TQ, kvw), lambda b, n: (b, n, 0)),
                  pl.BlockSpec((1, WINDOW, kvw), prev_map),
                  pl.BlockSpec((1, TQ, kvw), lambda b, n: (b, n, 0))],
        out_specs=pl.BlockSpec((1, TQ, H_B * HD_B), lambda b, n: (b, n, 0)),
        compiler_params=_params("parallel", "parallel"),
        name="swa_attn",
    )(sinks, q, k_prev, k_cur, v_prev, v_cur)


def _gla_kernel(*refs, T, layer, has_state, n_chunks):
    if has_state:
        (qr_ref, fr_ref, ir_ref, gr_ref, lbc_ref, gw_ref, s0_ref, o_ref, s1_ref, st_scr, cum_scr, k_scr, v_scr) = refs
    else:
        (qr_ref, fr_ref, ir_ref, gr_ref, lbc_ref, gw_ref, o_ref, s1_ref, st_scr, cum_scr, k_scr, v_scr) = refs
    c = pl.program_id(1)

    @pl.when(c == 0)
    def _():
        if has_state:
            for h in range(H_C):
                st_scr[h] = s0_ref[0, h].T
        else:
            st_scr[...] = jnp.zeros_like(st_scr)

    lbc = lbc_ref[...]
    ex = jnp.exp(lbc - jnp.max(lbc, axis=0, keepdims=True))
    sm = ex / jnp.sum(ex, axis=0, keepdims=True)
    lb = jnp.sum(sm[1:layer + 1], axis=0, keepdims=True) if layer >= 1 else jnp.zeros_like(sm[0:1])
    tri = _lower_tri_bf16(T)
    t_idx = lax.broadcasted_iota(jnp.int32, (T, DK_C), 0)
    for h in range(H_C):
        sl = slice(h * DK_C, (h + 1) * DK_C)
        q = _silu(qr_ref[0, :, sl])
        fg = lb[:, sl] + (1.0 - lb[:, sl]) * jax.nn.sigmoid(fr_ref[0, :, sl])
        lf = jnp.log(fg)
        key = 1.0 - fg
        v = ir_ref[0, :, sl]
        cum = _exact_dot_rhs(tri, lf)
        cum_scr[...] = cum
        k_scr[...] = key
        v_scr[...] = v

        def body(s, o, q=q, cum=cum):
            row_c = cum_scr[pl.ds(s, 1), :]
            row_k = k_scr[pl.ds(s, 1), :]
            row_v = v_scr[pl.ds(s, 1), :]
            dec = jnp.exp(jnp.where(t_idx >= s, cum - row_c, NEG_INF))
            p = jnp.sum(q * dec * row_k, axis=-1, keepdims=True)
            return o + p * row_v

        o = lax.fori_loop(0, T, body, jnp.zeros((T, DV_C), F32))
        st = st_scr[h]
        o = o + _dot_nt((q * jnp.exp(cum)).astype(BF16), st.astype(BF16))
        last = cum[T - 1:T, :]
        kd = (key * jnp.exp(last - cum)).astype(BF16)
        st_scr[h] = jnp.exp(last) * st + _dot_tn(v.astype(BF16), kd)
        o = o * lax.rsqrt(jnp.mean(o * o, axis=-1, keepdims=True) + RMS_EPS) * gw_ref[:, sl]
        o_ref[0, :, sl] = o * _silu(gr_ref[0, :, sl])

    @pl.when(c == n_chunks - 1)
    def _():
        for h in range(H_C):
            s1_ref[0, h] = st_scr[h].T


def _gla(qr, fr, ir, gr, lb_c, gnorm_w, s0, layer, T):
    bsz, seq, w = qr.shape
    n_chunks = seq // T
    has_state = s0 is not None
    kern = functools.partial(_gla_kernel, T=T, layer=layer, has_state=has_state, n_chunks=n_chunks)
    tok = pl.BlockSpec((1, T, w), lambda b, c: (b, c, 0))
    st_spec = pl.BlockSpec((1, H_C, DK_C, DV_C), lambda b, c: (b, 0, 0, 0))
    in_specs = [tok, tok, tok, tok,
                pl.BlockSpec((DEPTH, w), lambda b, c: (0, 0)),
                pl.BlockSpec((1, w), lambda b, c: (0, 0))]
    args = [qr, fr, ir, gr, lb_c, gnorm_w[None, :]]
    if has_state:
        in_specs.append(st_spec)
        args.append(s0)
    return pl.pallas_call(
        kern,
        out_shape=[jax.ShapeDtypeStruct((bsz, seq, w), F32),
                   jax.ShapeDtypeStruct((bsz, H_C, DK_C, DV_C), F32)],
        grid=(bsz, n_chunks),
        in_specs=in_specs,
        out_specs=[tok, st_spec],
        scratch_shapes=[pltpu.VMEM((H_C, DV_C, DK_C), F32),
                        pltpu.VMEM((T, DK_C), F32), pltpu.VMEM((T, DK_C), F32), pltpu.VMEM((T, DV_C), F32)],
        compiler_params=_params("parallel", "arbitrary"),
        name="gla_hgrn2",
    )(*args)


def _fox_cum_kernel(fd_ref, fb_ref, lf_ref, cum_ref, cumt_ref, carry_scr, *, T):
    c = pl.program_id(1)

    @pl.when(c == 0)
    def _():
        carry_scr[...] = jnp.zeros_like(carry_scr)

    lf = -_softplus(-(fd_ref[0] + fb_ref[...]))
    lf_ref[0] = lf
    cum = _exact_dot_rhs(_lower_tri_bf16(T), lf) + carry_scr[...]
    cum_ref[0] = cum
    cumt_ref[0] = cum.T[0:H_D, :]
    carry_scr[...] = cum[T - 1:T, :]


def _fox_cum(fd, fbias_pad, T):
    bsz, seq, _ = fd.shape
    kern = functools.partial(_fox_cum_kernel, T=T)
    tok = pl.BlockSpec((1, T, LANES), lambda b, c: (b, c, 0))
    return pl.pallas_call(
        kern,
        out_shape=[jax.ShapeDtypeStruct((bsz, seq, LANES), F32),
                   jax.ShapeDtypeStruct((bsz, seq, LANES), F32),
                   jax.ShapeDtypeStruct((bsz, H_D, seq), F32)],
        grid=(bsz, seq // T),
        in_specs=[tok, pl.BlockSpec((1, LANES), lambda b, c: (0, 0))],
        out_specs=[tok, tok, pl.BlockSpec((1, H_D, T), lambda b, c: (b, 0, c))],
        scratch_shapes=[pltpu.VMEM((1, LANES), F32)],
        compiler_params=_params("parallel", "arbitrary"),
        name="fox_logf_cum",
    )(fd, fbias_pad)


def _fox_flash_kernel(qi_ref, ki_ref, q_ref, k_ref, v_ref, cum_ref, cumt_ref, o_ref,
                      qm_scr, cq_scr, m_scr, l_scr, acc_scr, *, TQ, TK):
    s_id = pl.program_id(1)
    qi = qi_ref[s_id]
    ki = ki_ref[s_id]
    group = H_D // KV_D
    lane = lax.broadcasted_iota(jnp.int32, (1, 2 * HD_D), 1)
    low = lane < HD_D

    @pl.when(ki == 0)
    def _():
        cum = cum_ref[0]
        for h in range(H_D):
            half = q_ref[0, :, (h // 2) * 2 * HD_D:(h // 2 + 1) * 2 * HD_D] * SCALE_D
            qm = jnp.where(low if h % 2 == 0 else jnp.logical_not(low), half, 0.0)
            qm_scr[h * TQ:(h + 1) * TQ, :] = qm.astype(BF16)
            cq_scr[h * TQ:(h + 1) * TQ, :] = cum[:, h:h + 1]
        m_scr[...] = jnp.full_like(m_scr, NEG_INF)
        l_scr[...] = jnp.zeros_like(l_scr)
        acc_scr[...] = jnp.zeros_like(acc_scr)

    qpos = qi * TQ + lax.broadcasted_iota(jnp.int32, (TQ, TK), 0)
    kpos = ki * TK + lax.broadcasted_iota(jnp.int32, (TQ, TK), 1)
    causal = kpos <= qpos
    for kv in range(KV_D):
        blk = slice((kv // 2) * 2 * HD_D, (kv // 2 + 1) * 2 * HD_D)
        kb = k_ref[0, :, blk]
        vb = v_ref[0, :, blk]
        take_same = low if kv % 2 == 0 else jnp.logical_not(low)
        k2 = jnp.where(take_same, kb, pltpu.roll(kb, HD_D, axis=1)).astype(BF16)
        v2 = jnp.where(take_same, vb, pltpu.roll(vb, HD_D, axis=1)).astype(BF16)
        for a in range(group):
            h = kv * group + a
            rows = slice(h * TQ, (h + 1) * TQ)
            s = _dot_nt(qm_scr[rows, :], k2)
            s = s + (cq_scr[rows, :] - cumt_ref[0, h:h + 1, :])
            s = jnp.where(causal, s, NEG_INF)
            m_prev = m_scr[rows, :]
            m_new = jnp.maximum(m_prev, jnp.max(s, axis=-1, keepdims=True))
            alpha = jnp.exp(m_prev - m_new)
            p = jnp.exp(s - m_new)
            l_scr[rows, :] = alpha * l_scr[rows, :] + jnp.sum(p, axis=-1, keepdims=True)
            acc_scr[rows, :] = alpha * acc_scr[rows, :] + _dot(p.astype(BF16), v2)
            m_scr[rows, :] = m_new

    @pl.when(ki == qi)
    def _():
        for j in range(H_D // 2):
            r0 = slice(2 * j * TQ, (2 * j + 1) * TQ)
            r1 = slice((2 * j + 1) * TQ, (2 * j + 2) * TQ)
            o0 = acc_scr[r0, :] / l_scr[r0, :]
            o1 = acc_scr[r1, :] / l_scr[r1, :]
            o_ref[0, :, j * 2 * HD_D:(j + 1) * 2 * HD_D] = jnp.where(low, o0, o1)


def _fox_flash(q, k, v, cum, cumt, TQ, TK):
    bsz, seq, _ = q.shape
    assert TQ == TK
    nq = seq // TQ
    pairs = [(a, b) for a in range(nq) for b in range(a + 1)]
    qi_tab = jnp.asarray([p[0] for p in pairs], jnp.int32)
    ki_tab = jnp.asarray([p[1] for p in pairs], jnp.int32)
    kern = functools.partial(_fox_flash_kernel, TQ=TQ, TK=TK)
    qw, kw = H_D * HD_D, KV_D * HD_D
    grid_spec = pltpu.PrefetchScalarGridSpec(
        num_scalar_prefetch=2,
        grid=(bsz, len(pairs)),
        in_specs=[pl.BlockSpec((1, TQ, qw), lambda b, s, qt, kt: (b, qt[s], 0)),
                  pl.BlockSpec((1, TK, kw), lambda b, s, qt, kt: (b, kt[s], 0)),
                  pl.BlockSpec((1, TK, kw), lambda b, s, qt, kt: (b, kt[s], 0)),
                  pl.BlockSpec((1, TQ, LANES), lambda b, s, qt, kt: (b, qt[s], 0)),
                  pl.BlockSpec((1, H_D, TK), lambda b, s, qt, kt: (b, 0, kt[s]))],
        out_specs=pl.BlockSpec((1, TQ, qw), lambda b, s, qt, kt: (b, qt[s], 0)),
        scratch_shapes=[pltpu.VMEM((H_D * TQ, 2 * HD_D), BF16),
                        pltpu.VMEM((H_D * TQ, 1), F32),
                        pltpu.VMEM((H_D * TQ, 1), F32),
                        pltpu.VMEM((H_D * TQ, 1), F32),
                        pltpu.VMEM((H_D * TQ, 2 * HD_D), F32)])
    return pl.pallas_call(
        kern,
        out_shape=jax.ShapeDtypeStruct((bsz, seq, qw), F32),
        grid_spec=grid_spec,
        compiler_params=_params("parallel", "arbitrary"),
        name="fox_flash",
    )(qi_tab, ki_tab, q, k, v, cum, cumt)


def _fox_paged_kernel(pt_ref, q_ref, kn_ref, vn_ref, fdn_ref, fb_ref, *rest, PP, NG):
    k_refs = rest[0:PP]
    v_refs = rest[PP:2 * PP]
    lf_refs = rest[2 * PP:3 * PP]
    o_ref, lfn_ref, qall_scr, m_scr, l_scr, acc_scr, carry_scr = rest[3 * PP:]
    del pt_ref
    g = pl.program_id(1)
    L = q_ref.shape[1]
    rows = H_D * L
    group = H_D // KV_D
    kvw = KV_D * HD_D
    W = PP * PAGE_SIZE

    @pl.when(g == 0)
    def _():
        q = q_ref[0] * SCALE_D
        for h in range(H_D):
            kv = h // group
            piece = q[:, h * HD_D:(h + 1) * HD_D]
            parts = []
            if kv > 0:
                parts.append(jnp.zeros((L, kv * HD_D), F32))
            parts.append(piece)
            if kv < KV_D - 1:
                parts.append(jnp.zeros((L, (KV_D - 1 - kv) * HD_D), F32))
            qall_scr[h * L:(h + 1) * L, :] = jnp.concatenate(parts, axis=1).astype(BF16)
        lfn = -_softplus(-(fdn_ref[0] + fb_ref[...]))
        lfn_ref[0] = lfn
        cumn = _exact_dot_rhs(_lower_tri_bf16(L), lfn)
        cumn_t = cumn.T[0:H_D, :]
        bias = jnp.broadcast_to((-cumn_t)[:, None, :], (H_D, L, L)).reshape(rows, L)
        s = _dot_nt(qall_scr[...], kn_ref[0].astype(BF16)) + bias
        qidx = lax.broadcasted_iota(jnp.int32, (rows, L), 0) % L
        kidx = lax.broadcasted_iota(jnp.int32, (rows, L), 1)
        s = jnp.where(kidx <= qidx, s, NEG_INF)
        mx = jnp.max(s, axis=-1, keepdims=True)
        p = jnp.exp(s - mx)
        m_scr[...] = mx
        l_scr[...] = jnp.sum(p, axis=-1, keepdims=True)
        acc_scr[...] = _dot(p.astype(BF16), vn_ref[0].astype(BF16))
        carry_scr[...] = jnp.zeros_like(carry_scr)

    kcat = jnp.concatenate([r[0] for r in k_refs], axis=0).astype(BF16)
    vcat = jnp.concatenate([r[0] for r in v_refs], axis=0).astype(BF16)
    s = _dot_nt(qall_scr[...], kcat)
    lft = jnp.concatenate([r[0].T for r in lf_refs], axis=1)
    lane_in_page = lax.broadcasted_iota(jnp.int32, (1, W), 1) % PAGE_SIZE
    y = lft
    step = 1
    while step < PAGE_SIZE:
        y = y + jnp.where(lane_in_page < PAGE_SIZE - step, pltpu.roll(y, W - step, axis=1), 0.0)
        step *= 2
    d_local = y - lft
    carry = carry_scr[...]
    pieces = []
    for i in range(PP):
        pieces.append(d_local[:, i * PAGE_SIZE:(i + 1) * PAGE_SIZE] + carry)
        carry = carry + y[:, i * PAGE_SIZE:i * PAGE_SIZE + 1]
    carry_scr[...] = carry
    bias_t = jnp.concatenate(pieces, axis=1)
    s = s + jnp.broadcast_to(bias_t[:, None, :], (H_D, L, W)).reshape(rows, W)
    m_prev = m_scr[...]
    m_new = jnp.maximum(m_prev, jnp.max(s, axis=-1, keepdims=True))
    alpha = jnp.exp(m_prev - m_new)
    p = jnp.exp(s - m_new)
    l_scr[...] = alpha * l_scr[...] + jnp.sum(p, axis=-1, keepdims=True)
    acc_scr[...] = alpha * acc_scr[...] + _dot(p.astype(BF16), vcat)
    m_scr[...] = m_new

    @pl.when(g == NG - 1)
    def _():
        o = acc_scr[...] / l_scr[...]
        parts = []
        for h in range(H_D):
            kv = h // group
            parts.append(o[h * L:(h + 1) * L, kv * HD_D:(kv + 1) * HD_D])
        o_ref[0] = jnp.concatenate(parts, axis=1)


def _fox_paged(q, k_new, v_new, fd_new, fbias_pad, pool_k, pool_v, pool_lf, page_table, layer, PP):
    bsz, L, qw = q.shape
    n_pool = pool_k.shape[1]
    n_pages = page_table.shape[1]
    NG = n_pages // PP
    kvw = KV_D * HD_D
    pk = pool_k.reshape(pool_k.shape[0] * n_pool, PAGE_SIZE, kvw)
    pv = pool_v.reshape(pool_v.shape[0] * n_pool, PAGE_SIZE, kvw)
    plf = pool_lf.reshape(pool_lf.shape[0] * n_pool, PAGE_SIZE, H_D)
    base = layer * n_pool

    def page_map(i):
        return lambda b, g, pt: (base + pt[b, n_pages - 1 - (g * PP + i)], 0, 0)

    seq_map = lambda b, g, pt: (b, 0, 0)
    in_specs = [pl.BlockSpec((1, L, qw), seq_map),
                pl.BlockSpec((1, L, kvw), seq_map),
                pl.BlockSpec((1, L, kvw), seq_map),
                pl.BlockSpec((1, L, LANES), seq_map),
                pl.BlockSpec((1, LANES), lambda b, g, pt: (0, 0))]
    in_specs += [pl.BlockSpec((1, PAGE_SIZE, kvw), page_map(i)) for i in range(PP)]
    in_specs += [pl.BlockSpec((1, PAGE_SIZE, kvw), page_map(i)) for i in range(PP)]
    in_specs += [pl.BlockSpec((1, PAGE_SIZE, H_D), page_map(i)) for i in range(PP)]
    kern = functools.partial(_fox_paged_kernel, PP=PP, NG=NG)
    grid_spec = pltpu.PrefetchScalarGridSpec(
        num_scalar_prefetch=1,
        grid=(bsz, NG),
        in_specs=in_specs,
        out_specs=[pl.BlockSpec((1, L, qw), seq_map), pl.BlockSpec((1, L, LANES), seq_map)],
        scratch_shapes=[pltpu.VMEM((H_D * L, kvw), BF16),
                        pltpu.VMEM((H_D * L, 1), F32),
                        pltpu.VMEM((H_D * L, 1), F32),
                        pltpu.VMEM((H_D * L, kvw), F32),
                        pltpu.VMEM((H_D, 1), F32)])
    return pl.pallas_call(
        kern,
        out_shape=[jax.ShapeDtypeStruct((bsz, L, qw), F32), jax.ShapeDtypeStruct((bsz, L, LANES), F32)],
        grid_spec=grid_spec,
        compiler_params=_params("parallel", "arbitrary"),
        name="fox_paged",
    )(page_table, q, k_new, v_new, fd_new, fbias_pad, *([pk] * PP), *([pv] * PP), *([plf] * PP))


def _pad_cols(w, n):
    return jnp.pad(w, ((0, 0), (0, n - w.shape[1])))


def _prep_w_ab(w):
    z, xbc, dt, q, k, v = jnp.split(w, np.cumsum((D_INNER_A, CONV_DIM_A, H_A, H_B * HD_B, KV_B * HD_B, KV_B * HD_B))[:-1].tolist(), axis=1)
    return jnp.concatenate([z, xbc, q, k, v, _pad_cols(dt, LANES)], axis=1).astype(BF16)


AB_SIZES = (D_INNER_A, CONV_DIM_A, H_B * HD_B, KV_B * HD_B, KV_B * HD_B, LANES)


def _prep_w_cd(w):
    sizes = (H_C * DK_C, H_C * DK_C, H_C * DV_C, H_C * DV_C, H_D * HD_D, KV_D * HD_D, KV_D * HD_D, H_D)
    parts = jnp.split(w, np.cumsum(sizes)[:-1].tolist(), axis=1)
    parts[-1] = _pad_cols(parts[-1], LANES)
    return jnp.concatenate(parts, axis=1).astype(BF16)


CD_SIZES = (H_C * DK_C, H_C * DK_C, H_C * DV_C, H_C * DV_C, H_D * HD_D, KV_D * HD_D, KV_D * HD_D, LANES)


def _trunk(x, c, st, prm, page_table):
    sample = page_table is not None
    bsz, seq, _ = x.shape
    if sample:
        bb, tl, ffn_tl = min(SAMPLE_BATCH_TILE, bsz), seq, seq
    else:
        bb, tl, ffn_tl = 1, min(ROW_TILE, seq), min(FFN_ROW_TILE, seq)
    n_sub = DEPTH * 2
    m_all = _ada_all(c, prm['ada_w'].reshape(n_sub, D_MODEL, 3 * D_MODEL), prm['ada_b'].reshape(n_sub, 1, 3 * D_MODEL))
    new = {}
    for l in range(DEPTH):
        j = l // 2
        m_mix = m_all[2 * l][:, None, :]
        m_ffn = m_all[2 * l + 1][:, None, :]
        if l % 2 == 0:
            z, xbc, qb, kb, vb, dtp = _mod_matmul(x, m_mix, _prep_w_ab(prm['w_in_ab'][j]), AB_SIZES, bb, tl)
            if sample:
                conv0p = jnp.pad(st['conv_a'][j], ((0, 0), (SUBLANES - (CONV_A - 1), 0), (0, 0)))
                ssm0 = st['ssm_a'][j]
            else:
                conv0p, ssm0 = None, None
            ya, conv1p, ssm1 = _mamba(z, xbc, dtp, conv0p, ssm0, prm['conv_w_a'][j], prm['conv_b_a'][j],
                                      prm['dt_bias_a'][j], prm['a_log_a'][j], prm['d_skip_a'][j],
                                      prm['norm_w_a'][j], T=min(SSD_CHUNK, seq))
            if sample:
                kbuf = st['swa_k'][j].reshape(bsz, WINDOW, KV_B * HD_B)
                vbuf = st['swa_v'][j].reshape(bsz, WINDOW, KV_B * HD_B)
                yb = _swa(qb, kbuf, kb, vbuf, vb, prm['sinks_b'][j], TQ=seq, prev_is_same_array=False)
                bk = jnp.concatenate([kbuf[:, seq:], kb], axis=1)
                bv = jnp.concatenate([vbuf[:, seq:], vb], axis=1)
            else:
                yb = _swa(qb, kb, kb, vb, vb, prm['sinks_b'][j], TQ=WINDOW, prev_is_same_array=True)
                bk, bv = kb[:, -WINDOW:], vb[:, -WINDOW:]
            mix_in = jnp.concatenate([ya, yb], axis=-1)
            x = _mm_res_ln(mix_in, prm['w_out_ab'][j].astype(BF16), x, m_mix,
                           prm['ln_g'][l, 0][None, :], prm['ln_b'][l, 0][None, :], bb, tl)
            new.setdefault('ssm_a', []).append(ssm1)
            new.setdefault('conv_a', []).append(conv1p[:, SUBLANES - (CONV_A - 1):, :])
            new.setdefault('swa_k', []).append(bk.reshape(bsz, WINDOW, KV_B, HD_B))
            new.setdefault('swa_v', []).append(bv.reshape(bsz, WINDOW, KV_B, HD_B))
        else:
            qc, fc, ic, gc, qd, kd, vd, fdp = _mod_matmul(x, m_mix, _prep_w_cd(prm['w_in_cd'][j]), CD_SIZES, bb, tl)
            s0 = st['hgrn_c'][j] if sample else None
            yc, s1 = _gla(qc, fc, ic, gc, prm['lb_c'], prm['gnorm_c'][j], s0, layer=l, T=min(GLA_CHUNK, seq))
            fbias_pad = jnp.zeros((1, LANES), F32).at[0, :H_D].set(prm['fbias_d'][j])
            if sample:
                yd, lfp = _fox_paged(qd, kd, vd, fdp, fbias_pad, st['fox_k'], st['fox_v'], st['fox_logf'],
                                     page_table, j, FOX_PAGES_PER_STEP)
            else:
                lfp, cum, cumt = _fox_cum(fdp, fbias_pad, FOX_CUM_T)
                yd = _fox_flash(qd, kd, vd, cum, cumt, FOX_TQ, FOX_TK)
            mix_in = jnp.concatenate([yc, yd], axis=-1)
            x = _mm_res_ln(mix_in, prm['w_out_cd'][j].astype(BF16), x, m_mix,
                           prm['ln_g'][l, 0][None, :], prm['ln_b'][l, 0][None, :], bb, tl)
            new.setdefault('hgrn_c', []).append(s1)
            new.setdefault('fox_k', []).append(kd.reshape(bsz, seq, KV_D, HD_D))
            new.setdefault('fox_v', []).append(vd.reshape(bsz, seq, KV_D, HD_D))
            new.setdefault('fox_logf', []).append(lfp[:, :, :H_D])
        buf0 = st['ffn_conv'][l] if sample else jnp.zeros((bsz, FFN_CONV - 1, D_FF), F32)
        a, buf1 = _ffn_in(x, m_ffn, prm['ffn_w_in'][l].astype(BF16), buf0, prm['ffn_conv_w'][l],
                          prm['ffn_conv_b'][l][None, :], bb, ffn_tl)
        x = _mm_res_ln(a, prm['ffn_w_out'][l].astype(BF16), x, m_ffn,
                       prm['ln_g'][l, 1][None, :], prm['ln_b'][l, 1][None, :], bb, ffn_tl)
        new.setdefault('ffn_conv', []).append(buf1)
    return x, {name: jnp.stack(rows, axis=0) for name, rows in new.items()}


def kernel(x_prompt, x_sample, state_ssm_a, state_conv_a, cache_swa_k, cache_swa_v, state_hgrn_c, cache_fox_k, cache_fox_v, cache_fox_logf, state_ffn_conv, page_table, c_prompt, c_sample, ada_w, ada_b, ln_g, ln_b, w_in_ab, w_out_ab, conv_w_a, conv_b_a, dt_bias_a, a_log_a, d_skip_a, norm_w_a, sinks_b, w_in_cd, w_out_cd, lb_c, gnorm_c, fbias_d, ffn_w_in, ffn_conv_w, ffn_conv_b, ffn_w_out):
    prm = dict(ada_w=ada_w, ada_b=ada_b, ln_g=ln_g, ln_b=ln_b, w_in_ab=w_in_ab, w_out_ab=w_out_ab,
               conv_w_a=conv_w_a, conv_b_a=conv_b_a, dt_bias_a=dt_bias_a, a_log_a=a_log_a,
               d_skip_a=d_skip_a, norm_w_a=norm_w_a, sinks_b=sinks_b, w_in_cd=w_in_cd, w_out_cd=w_out_cd,
               lb_c=lb_c, gnorm_c=gnorm_c, fbias_d=fbias_d, ffn_w_in=ffn_w_in, ffn_conv_w=ffn_conv_w,
               ffn_conv_b=ffn_conv_b, ffn_w_out=ffn_w_out)
    st = dict(ssm_a=state_ssm_a, conv_a=state_conv_a, swa_k=cache_swa_k, swa_v=cache_swa_v,
              hgrn_c=state_hgrn_c, fox_k=cache_fox_k, fox_v=cache_fox_v, fox_logf=cache_fox_logf,
              ffn_conv=state_ffn_conv)
    y_p, sp = _trunk(x_prompt, c_prompt, None, prm, None)
    y_s, ss = _trunk(x_sample, c_sample, st, prm, page_table)
    return (y_p, y_s,
            sp['ssm_a'], ss['ssm_a'], sp['conv_a'], ss['conv_a'],
            sp['swa_k'], ss['swa_k'], sp['swa_v'], ss['swa_v'],
            sp['hgrn_c'], ss['hgrn_c'],
            sp['fox_k'], ss['fox_k'], sp['fox_v'], ss['fox_v'], sp['fox_logf'], ss['fox_logf'],
            sp['ffn_conv'], ss['ffn_conv'])
```

```python
import functools
import math

import numpy as np
import jax
import jax.numpy as jnp
from jax import lax
from jax.experimental import pallas as pl
from jax.experimental.pallas import tpu as pltpu

F32 = jnp.float32
BF16 = jnp.bfloat16
NEG_INF = float("-inf")

D_MODEL = 1024
DEPTH = 2
PAGE_SIZE = 128
H_A, P_A, N_A, G_A, CONV_A = 16, 64, 128, 2, 4
D_INNER_A = H_A * P_A
CONV_DIM_A = D_INNER_A + 2 * G_A * N_A
H_B, KV_B, HD_B, WINDOW = 8, 2, 64, 128
H_C, DK_C, DV_C = 4, 128, 128
H_D, KV_D, HD_D = 16, 4, 64
D_FF, FFN_CONV = 2816, 3
ALPHA = (2 * DEPTH) ** 0.25
SCALE_B = HD_B ** -0.5
SCALE_D = HD_D ** -0.5
LN_EPS = 1e-5
RMS_EPS = 1e-6

LANES = 128
SUBLANES = 8
VMEM_LIMIT_BYTES = 56 * 1024 * 1024

SSD_CHUNK = 128
GLA_CHUNK = 256
GLA_BLOCK = 16
FOX_TQ = 512
FOX_TK = 512
FOX_PREP_T = 512
FOX_PAGES_PER_STEP = 16
ROW_TILE = 512
FFN_ROW_TILE = 256
SAMPLE_BATCH_TILE = 32


def _params(*sem):
    return pltpu.CompilerParams(dimension_semantics=sem, vmem_limit_bytes=VMEM_LIMIT_BYTES)


def _resident(shape):
    nd = len(shape)
    return pl.BlockSpec(shape, lambda *_: (0,) * nd, pipeline_mode=pl.Buffered(1))


def _silu(x):
    return x * jax.nn.sigmoid(x)


def _softplus(x):
    return jnp.maximum(x, 0.0) + jnp.log1p(jnp.exp(-jnp.abs(x)))


def _dot(a, b):
    return jnp.dot(a, b, preferred_element_type=F32)


def _dot_nt(a, b):
    return lax.dot_general(a, b, (((1,), (1,)), ((), ())), preferred_element_type=F32)


def _dot_tn(a, b):
    return lax.dot_general(a, b, (((0,), (0,)), ((), ())), preferred_element_type=F32)


def _split3(x):
    hi = x.astype(BF16)
    r = x - hi.astype(F32)
    mid = r.astype(BF16)
    lo = (r - mid.astype(F32)).astype(BF16)
    return hi, mid, lo


def _exact_dot_lhs(x, w01):
    hi, mid, lo = _split3(x)
    return _dot(hi, w01) + _dot(mid, w01) + _dot(lo, w01)


def _exact_dot_rhs(w01, x):
    hi, mid, lo = _split3(x)
    return _dot(w01, hi) + _dot(w01, mid) + _dot(w01, lo)


def _lower_tri_bf16(t):
    r = lax.broadcasted_iota(jnp.int32, (t, t), 0)
    c = lax.broadcasted_iota(jnp.int32, (t, t), 1)
    return jnp.where(r >= c, 1.0, 0.0).astype(BF16)


def _layer_norm_rows(r, g, b):
    mu = jnp.mean(r, axis=-1, keepdims=True)
    d = r - mu
    var = jnp.mean(d * d, axis=-1, keepdims=True)
    return d * lax.rsqrt(var + LN_EPS) * g + b


def _ada_kernel(c_ref, w_ref, b_ref, o_ref):
    s = _silu(c_ref[...]).astype(BF16)
    o_ref[0] = _dot(s, w_ref[0].astype(BF16)) + b_ref[0]


def _ada_all(c, ada_w, ada_b):
    bsz = c.shape[0]
    n_sub = ada_w.shape[0]
    n_out = ada_w.shape[2]
    tn = D_MODEL
    return pl.pallas_call(
        _ada_kernel,
        out_shape=jax.ShapeDtypeStruct((n_sub, bsz, n_out), F32),
        grid=(n_sub, n_out // tn),
        in_specs=[pl.BlockSpec((bsz, D_MODEL), lambda s, j: (0, 0)),
                  pl.BlockSpec((1, D_MODEL, tn), lambda s, j: (s, 0, j)),
                  pl.BlockSpec((1, 1, tn), lambda s, j: (s, 0, j))],
        out_specs=pl.BlockSpec((1, bsz, tn), lambda s, j: (s, 0, j)),
        compiler_params=_params("parallel", "parallel"),
        name="ada_mod",
    )(c, ada_w, ada_b)


def _mod_matmul_kernel(x_ref, m_ref, w_ref, *o_refs, bb, tl, sizes):
    d = x_ref.shape[-1]
    m = m_ref[...]
    h = x_ref[...] * (1.0 + m[:, :, d:2 * d]) + m[:, :, 0:d]
    h2 = h.reshape(bb * tl, d).astype(BF16)
    off = 0
    for o_ref, n in zip(o_refs, sizes):
        o_ref[...] = _dot(h2, w_ref[:, off:off + n]).reshape(bb, tl, n)
        off += n


def _mod_matmul(x, m, w_bf16, sizes, bb, tl):
    bsz, seq, d = x.shape
    n_tot = w_bf16.shape[1]
    kern = functools.partial(_mod_matmul_kernel, bb=bb, tl=tl, sizes=tuple(sizes))
    return pl.pallas_call(
        kern,
        out_shape=[jax.ShapeDtypeStruct((bsz, seq, n), F32) for n in sizes],
        grid=(bsz // bb, seq // tl),
        in_specs=[pl.BlockSpec((bb, tl, d), lambda i, j: (i, j, 0)),
                  pl.BlockSpec((bb, 1, 3 * d), lambda i, j: (i, 0, 0)),
                  _resident((d, n_tot))],
        out_specs=[pl.BlockSpec((bb, tl, n), lambda i, j: (i, j, 0)) for n in sizes],
        compiler_params=_params("parallel", "parallel"),
        name="mod_matmul",
    )(x, m, w_bf16)


def _mm_res_ln_kernel(*refs, bb, tl, n_in):
    y_refs = refs[0:n_in]
    w_refs = refs[n_in:2 * n_in]
    x_ref, m_ref, g_ref, b_ref, o_ref = refs[2 * n_in:]
    d = x_ref.shape[-1]
    mix = None
    for y_ref, w_ref in zip(y_refs, w_refs):
        y2 = y_ref[...].reshape(bb * tl, y_ref.shape[-1]).astype(BF16)
        part = _dot(y2, w_ref[...])
        mix = part if mix is None else mix + part
    gate = m_ref[:, :, 2 * d:3 * d]
    r = ALPHA * x_ref[...] + (1.0 + gate) * mix.reshape(bb, tl, d)
    o_ref[...] = _layer_norm_rows(r, g_ref[...], b_ref[...])


def _mm_res_ln(ys, ws_bf16, x, m, g, b, bb, tl):
    bsz, seq, d = x.shape
    kern = functools.partial(_mm_res_ln_kernel, bb=bb, tl=tl, n_in=len(ys))
    return pl.pallas_call(
        kern,
        out_shape=jax.ShapeDtypeStruct((bsz, seq, d), F32),
        grid=(bsz // bb, seq // tl),
        in_specs=([pl.BlockSpec((bb, tl, y.shape[-1]), lambda i, j: (i, j, 0)) for y in ys]
                  + [_resident(w.shape) for w in ws_bf16]
                  + [pl.BlockSpec((bb, tl, d), lambda i, j: (i, j, 0)),
                     pl.BlockSpec((bb, 1, 3 * d), lambda i, j: (i, 0, 0)),
                     pl.BlockSpec((1, d), lambda i, j: (0, 0)),
                     pl.BlockSpec((1, d), lambda i, j: (0, 0))]),
        out_specs=pl.BlockSpec((bb, tl, d), lambda i, j: (i, j, 0)),
        compiler_params=_params("parallel", "parallel"),
        name="mm_res_ln",
    )(*ys, *ws_bf16, x, m, g, b)


def _ffn_in_kernel(x_ref, m_ref, w_ref, buf0_ref, cw_ref, cb_ref, a_ref, buf1_ref, carry_scr, *, bb, tl, n_col):
    d = x_ref.shape[-1]
    dff = a_ref.shape[-1]
    j = pl.program_id(1)

    @pl.when(j == 0)
    def _():
        carry_scr[...] = buf0_ref[...]

    m = m_ref[...]
    h = x_ref[...] * (1.0 + m[:, :, d:2 * d]) + m[:, :, 0:d]
    h2 = h.reshape(bb * tl, d).astype(BF16)
    rows = bb * tl
    cw = dff // n_col
    t = lax.broadcasted_iota(jnp.int32, (rows, 1), 0) % tl
    for ci in range(n_col):
        lo = ci * cw
        u = _dot(h2, w_ref[:, lo:lo + cw])
        g = _dot(h2, w_ref[:, dff + lo:dff + lo + cw])
        carry = carry_scr[:, :, lo:lo + cw]
        prev1 = jnp.broadcast_to(carry[:, 1:2, :], (bb, tl, cw)).reshape(rows, cw)
        prev0 = jnp.broadcast_to(carry[:, 0:1, :], (bb, tl, cw)).reshape(rows, cw)
        tap1 = jnp.where(t == 0, prev1, pltpu.roll(g, 1, axis=0))
        tap2 = jnp.where(t == 0, prev0, jnp.where(t == 1, prev1, pltpu.roll(g, 2, axis=0)))
        y = cb_ref[:, lo:lo + cw] + g * cw_ref[2:3, lo:lo + cw]
        y = y + tap2 * cw_ref[0:1, lo:lo + cw]
        y = y + tap1 * cw_ref[1:2, lo:lo + cw]
        a_ref[:, :, lo:lo + cw] = (_silu(y) * u).reshape(bb, tl, cw)
        last2 = g.reshape(bb, tl, cw)[:, tl - (FFN_CONV - 1):, :]
        carry_scr[:, :, lo:lo + cw] = last2
        buf1_ref[:, :, lo:lo + cw] = last2


def _ffn_in(x, m, w_bf16, buf0, conv_w, conv_b, bb, tl):
    bsz, seq, d = x.shape
    kern = functools.partial(_ffn_in_kernel, bb=bb, tl=tl, n_col=2)
    return pl.pallas_call(
        kern,
        out_shape=[jax.ShapeDtypeStruct((bsz, seq, D_FF), F32),
                   jax.ShapeDtypeStruct((bsz, FFN_CONV - 1, D_FF), F32)],
        grid=(bsz // bb, seq // tl),
        in_specs=[pl.BlockSpec((bb, tl, d), lambda i, j: (i, j, 0)),
                  pl.BlockSpec((bb, 1, 3 * d), lambda i, j: (i, 0, 0)),
                  _resident((d, 2 * D_FF)),
                  pl.BlockSpec((bb, FFN_CONV - 1, D_FF), lambda i, j: (i, 0, 0)),
                  pl.BlockSpec((FFN_CONV, D_FF), lambda i, j: (0, 0)),
                  pl.BlockSpec((1, D_FF), lambda i, j: (0, 0))],
        out_specs=[pl.BlockSpec((bb, tl, D_FF), lambda i, j: (i, j, 0)),
                   pl.BlockSpec((bb, FFN_CONV - 1, D_FF), lambda i, j: (i, 0, 0))],
        scratch_shapes=[pltpu.VMEM((bb, FFN_CONV - 1, D_FF), F32)],
        compiler_params=_params("parallel", "arbitrary"),
        name="ffn_in_conv",
    )(x, m, w_bf16, buf0, conv_w, conv_b)


def _mamba_kernel(*refs, T, has_state, n_chunks):
    if has_state:
        (z_ref, xbc_ref, dt_ref, conv0_ref, ssm0_ref, cw_ref, cb_ref, dtb_ref, alog_ref, dsk_ref, nw_ref, e_ref,
         y_ref, conv1_ref, ssm1_ref, carry_scr, s_scr) = refs
    else:
        (z_ref, xbc_ref, dt_ref, cw_ref, cb_ref, dtb_ref, alog_ref, dsk_ref, nw_ref, e_ref,
         y_ref, conv1_ref, ssm1_ref, carry_scr, s_scr) = refs
    c = pl.program_id(1)
    n_pair = H_A // 2
    cdim = CONV_DIM_A

    @pl.when(c == 0)
    def _():
        if has_state:
            carry_scr[...] = conv0_ref[0]
            for j in range(n_pair):
                s_scr[j] = jnp.concatenate([ssm0_ref[0, 2 * j], ssm0_ref[0, 2 * j + 1]], axis=1)
        else:
            carry_scr[...] = jnp.zeros_like(carry_scr)
            s_scr[...] = jnp.zeros_like(s_scr)

    xb = xbc_ref[0]
    carry = carry_scr[...]
    row8 = lax.broadcasted_iota(jnp.int32, (SUBLANES, cdim), 0)
    acc = cb_ref[...] + xb * cw_ref[CONV_A - 1:CONV_A, :]
    for dshift in (3, 2, 1):
        rolled = pltpu.roll(xb, dshift, axis=0)
        top = jnp.where(row8 < dshift, pltpu.roll(carry, dshift, axis=0), rolled[0:SUBLANES])
        sh = top if T == SUBLANES else jnp.concatenate([top, rolled[SUBLANES:]], axis=0)
        acc = acc + sh * cw_ref[CONV_A - 1 - dshift:CONV_A - dshift, :]
    last8 = xb[T - SUBLANES:T, :]
    carry_scr[...] = last8
    conv1_ref[0] = last8
    xc = _silu(acc)
    xs = xc[:, 0:D_INNER_A]
    bm = xc[:, D_INNER_A:D_INNER_A + G_A * N_A]
    cm = xc[:, D_INNER_A + G_A * N_A:cdim]

    dt = _softplus(dt_ref[0] + dtb_ref[...])
    la = dt * (-jnp.exp(alog_ref[...]))
    tri = _lower_tri_bf16(T)
    cum = _exact_dot_rhs(tri, la)
    e = e_ref[...]
    dtx = _exact_dot_lhs(dt, e)
    cumx = _exact_dot_lhs(cum, e)
    x = xs * dtx
    decx = jnp.exp(cumx)
    lastx = cumx[T - 1:T, :]
    xd = (x * jnp.exp(lastx - cumx)).astype(BF16)
    elastx = jnp.exp(lastx)
    cum_t = cum.T

    rr = lax.broadcasted_iota(jnp.int32, (T, T), 0)
    cc = lax.broadcasted_iota(jnp.int32, (T, T), 1)
    causal = rr >= cc
    lane = lax.broadcasted_iota(jnp.int32, (1, 2 * P_A), 1)
    low = lane < P_A

    cb_g, c_g, b_g = [], [], []
    for g in range(G_A):
        cg = cm[:, g * N_A:(g + 1) * N_A].astype(BF16)
        bg = bm[:, g * N_A:(g + 1) * N_A].astype(BF16)
        c_g.append(cg)
        b_g.append(bg)
        cb_g.append(_dot_nt(cg, bg))

    heads_per_group = H_A // G_A
    outs = []
    for j in range(n_pair):
        g = (2 * j) // heads_per_group
        sl = slice(j * 2 * P_A, (j + 1) * 2 * P_A)
        xj = x[:, sl]
        yj = None
        for eidx in (0, 1):
            h = 2 * j + eidx
            col = cum[:, h:h + 1]
            row = cum_t[h:h + 1, :]
            decay = jnp.exp(jnp.where(causal, col - row, NEG_INF))
            scores = (cb_g[g] * decay).astype(BF16)
            xm = jnp.where(low if eidx == 0 else jnp.logical_not(low), xj, 0.0).astype(BF16)
            part = _dot(scores, xm)
            yj = part if yj is None else yj + part
        s_old = s_scr[j]
        yj = yj + _dot(c_g[g], s_old.astype(BF16)) * decx[:, sl]
        s_scr[j] = elastx[:, sl] * s_old + _dot_tn(b_g[g], xd[:, sl])
        outs.append(yj)
    y = jnp.concatenate(outs, axis=1) + dsk_ref[...] * xs
    y = y * _silu(z_ref[0])
    y_ref[0] = y * lax.rsqrt(jnp.mean(y * y, axis=-1, keepdims=True) + RMS_EPS) * nw_ref[...]

    @pl.when(c == n_chunks - 1)
    def _():
        for j in range(n_pair):
            sj = s_scr[j]
            ssm1_ref[0, 2 * j] = sj[:, 0:P_A]
            ssm1_ref[0, 2 * j + 1] = sj[:, P_A:2 * P_A]


def _mamba(z, xbc, dtp, conv0p, ssm0, conv_w, conv_b, dt_bias, a_log, d_skip, norm_w, T):
    bsz, seq, _ = z.shape
    n_chunks = seq // T
    has_state = conv0p is not None
    dtb = jnp.zeros((1, LANES), F32).at[0, :H_A].set(dt_bias)
    alog = jnp.zeros((1, LANES), F32).at[0, :H_A].set(a_log)
    dsk = jnp.repeat(d_skip, P_A)[None, :]
    head_of_col = np.arange(D_INNER_A) // P_A
    expand = jnp.asarray((np.arange(LANES)[:, None] == head_of_col[None, :]).astype(np.float32), dtype=BF16)
    kern = functools.partial(_mamba_kernel, T=T, has_state=has_state, n_chunks=n_chunks)
    tok = lambda w: pl.BlockSpec((1, T, w), lambda b, c: (b, c, 0))
    const = lambda shape: pl.BlockSpec(shape, lambda b, c: (0,) * len(shape))
    in_specs = [tok(D_INNER_A), tok(CONV_DIM_A), tok(LANES)]
    args = [z, xbc, dtp]
    if has_state:
        in_specs += [pl.BlockSpec((1, SUBLANES, CONV_DIM_A), lambda b, c: (b, 0, 0)),
                     pl.BlockSpec((1, H_A, N_A, P_A), lambda b, c: (b, 0, 0, 0))]
        args += [conv0p, ssm0]
    in_specs += [const((CONV_A, CONV_DIM_A)), const((1, CONV_DIM_A)), const((1, LANES)), const((1, LANES)),
                 const((1, D_INNER_A)), const((1, D_INNER_A)), const((LANES, D_INNER_A))]
    args += [conv_w, conv_b[None, :], dtb, alog, dsk, norm_w[None, :], expand]
    return pl.pallas_call(
        kern,
        out_shape=[jax.ShapeDtypeStruct((bsz, seq, D_INNER_A), F32),
                   jax.ShapeDtypeStruct((bsz, SUBLANES, CONV_DIM_A), F32),
                   jax.ShapeDtypeStruct((bsz, H_A, N_A, P_A), F32)],
        grid=(bsz, n_chunks),
        in_specs=in_specs,
        out_specs=[tok(D_INNER_A),
                   pl.BlockSpec((1, SUBLANES, CONV_DIM_A), lambda b, c: (b, 0, 0)),
                   pl.BlockSpec((1, H_A, N_A, P_A), lambda b, c: (b, 0, 0, 0))],
        scratch_shapes=[pltpu.VMEM((SUBLANES, CONV_DIM_A), F32),
                        pltpu.VMEM((H_A // 2, N_A, 2 * P_A), F32)],
        compiler_params=_params("parallel", "arbitrary"),
        name="mamba_ssd",
    )(*args)


def _swa_kernel(sinks_ref, q_ref, kp_ref, kc_ref, vp_ref, vc_ref, o_ref, *, TQ, first_has_prev):
    n = pl.program_id(1)
    q = q_ref[0]
    kp, kc, vp, vc = kp_ref[0], kc_ref[0], vp_ref[0], vc_ref[0]
    lane = lax.broadcasted_iota(jnp.int32, (1, 2 * HD_B), 1)
    low = lane < HD_B
    roll64 = lambda a: pltpu.roll(a, HD_B, axis=1)
    same = [a.astype(BF16) for a in (kp, kc, vp, vc)]
    rolled = [roll64(a).astype(BF16) for a in (kp, kc, vp, vc)]
    has_prev = jnp.logical_or(n > 0, first_has_prev)
    ip = lax.broadcasted_iota(jnp.int32, (TQ, WINDOW), 0)
    jp = lax.broadcasted_iota(jnp.int32, (TQ, WINDOW), 1)
    valid_prev = jnp.logical_and(jp >= ip, has_prev)
    ic = lax.broadcasted_iota(jnp.int32, (TQ, TQ), 0)
    jc = lax.broadcasted_iota(jnp.int32, (TQ, TQ), 1)
    valid_cur = jc <= ic
    group = H_B // KV_B
    outs = []
    for j in range(H_B // 2):
        qp = q[:, j * 2 * HD_B:(j + 1) * 2 * HD_B] * SCALE_B
        res = []
        for eidx in (0, 1):
            h = 2 * j + eidx
            kv = h // group
            qm = jnp.where(low if eidx == 0 else jnp.logical_not(low), qp, 0.0).astype(BF16)
            kpv, kcv, vpv, vcv = same if kv == eidx else rolled
            sp = jnp.where(valid_prev, _dot_nt(qm, kpv), NEG_INF)
            sc = jnp.where(valid_cur, _dot_nt(qm, kcv), NEG_INF)
            sink = sinks_ref[h]
            mx = jnp.maximum(jnp.maximum(jnp.max(sp, axis=-1, keepdims=True),
                                         jnp.max(sc, axis=-1, keepdims=True)), sink)
            pp = jnp.exp(sp - mx)
            pc = jnp.exp(sc - mx)
            den = (jnp.sum(pp, axis=-1, keepdims=True) + jnp.sum(pc, axis=-1, keepdims=True)
                   + jnp.exp(sink - mx))
            o = _dot(pp.astype(BF16), vpv) + _dot(pc.astype(BF16), vcv)
            res.append(o / den)
        outs.append(jnp.where(low, res[0], res[1]))
    o_ref[0] = jnp.concatenate(outs, axis=1)


def _swa(q, k_prev, k_cur, v_prev, v_cur, sinks, TQ, prev_is_same_array):
    bsz, seq, _ = q.shape
    nb = seq // TQ
    kvw = KV_B * HD_B
    if prev_is_same_array:
        prev_map = lambda b, n: (b, jnp.maximum(n - 1, 0), 0)
    else:
        prev_map = lambda b, n: (b, 0, 0)
    kern = functools.partial(_swa_kernel, TQ=TQ, first_has_prev=not prev_is_same_array)
    return pl.pallas_call(
        kern,
        out_shape=jax.ShapeDtypeStruct((bsz, seq, H_B * HD_B), F32),
        grid=(bsz, nb),
        in_specs=[pl.BlockSpec(memory_space=pltpu.SMEM),
                  pl.BlockSpec((1, TQ, H_B * HD_B), lambda b, n: (b, n, 0)),
                  pl.BlockSpec((1, WINDOW, kvw), prev_map),
                  pl.BlockSpec((1, TQ, kvw), lambda b, n: (b, n, 0)),
                  pl.BlockSpec((1, WINDOW, kvw), prev_map),
                  pl.BlockSpec((1, TQ, kvw), lambda b, n: (b, n, 0))],
        out_specs=pl.BlockSpec((1, TQ, H_B * HD_B), lambda b, n: (b, n, 0)),
        compiler_params=_params("parallel", "parallel"),
        name="swa_attn",
    )(sinks, q, k_prev, k_cur, v_prev, v_cur)


def _gla_kernel(*refs, T, blk, layer, has_state, n_chunks):
    if has_state:
        (qr_ref, fr_ref, ir_ref, gr_ref, lbc_ref, gw_ref, s0_ref, o_ref, s1_ref,
         st_scr, qd_scr, kd_scr, el_scr, v_scr, od_scr, oi_scr) = refs
    else:
        (qr_ref, fr_ref, ir_ref, gr_ref, lbc_ref, gw_ref, o_ref, s1_ref,
         st_scr, qd_scr, kd_scr, el_scr, v_scr, od_scr, oi_scr) = refs
    c = pl.program_id(1)
    n_blk = T // blk

    @pl.when(c == 0)
    def _():
        if has_state:
            for h in range(H_C):
                st_scr[h] = s0_ref[0, h].T
        else:
            st_scr[...] = jnp.zeros_like(st_scr)

    lbc = lbc_ref[...]
    ex = jnp.exp(lbc - jnp.max(lbc, axis=0, keepdims=True))
    sm = ex / jnp.sum(ex, axis=0, keepdims=True)
    lb = jnp.sum(sm[1:layer + 1], axis=0, keepdims=True) if layer >= 1 else jnp.zeros_like(sm[0:1])
    rr = lax.broadcasted_iota(jnp.int32, (T, T), 0)
    cc = lax.broadcasted_iota(jnp.int32, (T, T), 1)
    same_blk = (rr // blk) == (cc // blk)
    tri_blk = jnp.where(jnp.logical_and(same_blk, rr >= cc), 1.0, 0.0).astype(BF16)
    ones_blk = jnp.where(same_blk, 1.0, 0.0).astype(BF16)
    t_in_blk = lax.broadcasted_iota(jnp.int32, (T, 1), 0) % blk

    for h in range(H_C):
        sl = slice(h * DK_C, (h + 1) * DK_C)
        q = _silu(qr_ref[0, :, sl])
        fg = lb[:, sl] + (1.0 - lb[:, sl]) * jax.nn.sigmoid(fr_ref[0, :, sl])
        lf = jnp.log(fg)
        key = 1.0 - fg
        v = ir_ref[0, :, sl]
        cr = _exact_dot_rhs(tri_blk, lf)
        last = _exact_dot_rhs(ones_blk, lf)
        od = jnp.sum(q * key, axis=-1, keepdims=True) * v
        for off in range(1, blk):
            dec = jnp.exp(jnp.where(t_in_blk >= off, cr - pltpu.roll(cr, off, axis=0), NEG_INF))
            p = jnp.sum(q * dec * pltpu.roll(key, off, axis=0), axis=-1, keepdims=True)
            od = od + p * pltpu.roll(v, off, axis=0)
        od_scr[:, sl] = od
        qd_scr[:, sl] = q * jnp.exp(cr)
        kd_scr[:, sl] = key * jnp.exp(last - cr)
        el_scr[:, sl] = jnp.exp(last)
        v_scr[:, sl] = v

    def blk_step(i):
        r0 = i * blk if isinstance(i, int) else pl.multiple_of(i * blk, blk)
        rows = pl.ds(r0, blk)
        for h in range(H_C):
            sl = slice(h * DK_C, (h + 1) * DK_C)
            st = st_scr[h]
            oi_scr[rows, sl] = _dot_nt(qd_scr[rows, sl].astype(BF16), st.astype(BF16))
            st_scr[h] = (el_scr[pl.ds(r0, 1), sl] * st
                         + _dot_tn(v_scr[rows, sl].astype(BF16), kd_scr[rows, sl].astype(BF16)))

    if n_blk == 1:
        blk_step(0)
    else:
        def body(i, carry):
            blk_step(i)
            return carry
        lax.fori_loop(0, n_blk, body, 0)

    for h in range(H_C):
        sl = slice(h * DK_C, (h + 1) * DK_C)
        o = od_scr[:, sl] + oi_scr[:, sl]
        o = o * lax.rsqrt(jnp.mean(o * o, axis=-1, keepdims=True) + RMS_EPS) * gw_ref[:, sl]
        o_ref[0, :, sl] = o * _silu(gr_ref[0, :, sl])

    @pl.when(c == n_chunks - 1)
    def _():
        for h in range(H_C):
            s1_ref[0, h] = st_scr[h].T


def _gla(qr, fr, ir, gr, lb_c, gnorm_w, s0, layer, T):
    bsz, seq, w = qr.shape
    n_chunks = seq // T
    has_state = s0 is not None
    blk = min(GLA_BLOCK, T)
    kern = functools.partial(_gla_kernel, T=T, blk=blk, layer=layer, has_state=has_state, n_chunks=n_chunks)
    tok = pl.BlockSpec((1, T, w), lambda b, c: (b, c, 0))
    st_spec = pl.BlockSpec((1, H_C, DK_C, DV_C), lambda b, c: (b, 0, 0, 0))
    in_specs = [tok, tok, tok, tok,
                pl.BlockSpec((DEPTH, w), lambda b, c: (0, 0)),
                pl.BlockSpec((1, w), lambda b, c: (0, 0))]
    args = [qr, fr, ir, gr, lb_c, gnorm_w[None, :]]
    if has_state:
        in_specs.append(st_spec)
        args.append(s0)
    return pl.pallas_call(
        kern,
        out_shape=[jax.ShapeDtypeStruct((bsz, seq, w), F32),
                   jax.ShapeDtypeStruct((bsz, H_C, DK_C, DV_C), F32)],
        grid=(bsz, n_chunks),
        in_specs=in_specs,
        out_specs=[tok, st_spec],
        scratch_shapes=[pltpu.VMEM((H_C, DV_C, DK_C), F32)] + [pltpu.VMEM((T, w), F32) for _ in range(6)],
        compiler_params=_params("parallel", "arbitrary"),
        name="gla_hgrn2",
    )(*args)


BIAS_TERMS = 3
BIAS_SELF_LANE = BIAS_TERMS * H_D


def _fox_prep_kernel(q_ref, k_ref, v_ref, fd_ref, fb_ref, lf_ref, qm_ref, k2_ref, v2_ref, carry_scr, *, T):
    c = pl.program_id(1)

    @pl.when(c == 0)
    def _():
        carry_scr[...] = jnp.zeros_like(carry_scr)

    lf = -_softplus(-(fd_ref[0] + fb_ref[...]))
    lf_ref[0] = lf
    cum = _exact_dot_rhs(_lower_tri_bf16(T), lf) + carry_scr[...]
    carry_scr[...] = cum[T - 1:T, :]
    hi, mid, lo = [t.astype(F32) for t in _split3(cum)]

    lane = lax.broadcasted_iota(jnp.int32, (1, 2 * HD_D), 1)
    low = lane < HD_D
    roll_half = lambda a: pltpu.roll(a, HD_D, axis=1)
    xk_low = jnp.where(lane < H_D, -hi,
                       jnp.where(lane < 2 * H_D, pltpu.roll(-mid, H_D, axis=1),
                                 jnp.where(lane < BIAS_SELF_LANE, pltpu.roll(-lo, 2 * H_D, axis=1),
                                           jnp.where(lane < BIAS_SELF_LANE + BIAS_TERMS, 1.0, 0.0))))
    xk_high = roll_half(xk_low)
    for i in range(KV_D // 2):
        blk = slice(i * 2 * HD_D, (i + 1) * 2 * HD_D)
        kb = k_ref[0, :, blk]
        vb = v_ref[0, :, blk]
        kr = roll_half(kb)
        vr = roll_half(vb)
        for par in (0, 1):
            kv = 2 * i + par
            k_low, k_high = (kb, kr) if par == 0 else (kr, kb)
            k2_ref[0, 2 * kv] = jnp.where(low, k_low, xk_high).astype(BF16)
            k2_ref[0, 2 * kv + 1] = jnp.where(low, xk_low, k_high).astype(BF16)
            v2_ref[0, kv] = (jnp.where(low, vb, vr) if par == 0 else jnp.where(low, vr, vb)).astype(BF16)
    for h in range(H_D):
        qp = q_ref[0, :, (h // 2) * 2 * HD_D:(h // 2 + 1) * 2 * HD_D] * SCALE_D
        own = jnp.logical_or(jnp.logical_or(lane == h, lane == H_D + h), lane == 2 * H_D + h)
        xq = jnp.where(own, 1.0,
                       jnp.where(lane == BIAS_SELF_LANE, hi[:, h:h + 1],
                                 jnp.where(lane == BIAS_SELF_LANE + 1, mid[:, h:h + 1],
                                           jnp.where(lane == BIAS_SELF_LANE + 2, lo[:, h:h + 1], 0.0))))
        qm = jnp.where(low, qp, roll_half(xq)) if h % 2 == 0 else jnp.where(low, xq, qp)
        qm_ref[0, h] = qm.astype(BF16)


def _fox_prep(q, k, v, fd, fbias_pad, T):
    bsz, seq, _ = q.shape
    kern = functools.partial(_fox_prep_kernel, T=T)
    tok = lambda w: pl.BlockSpec((1, T, w), lambda b, c: (b, c, 0))
    heads = lambda n: pl.BlockSpec((1, n, T, 2 * HD_D), lambda b, c: (b, 0, c, 0))
    return pl.pallas_call(
        kern,
        out_shape=[jax.ShapeDtypeStruct((bsz, seq, LANES), F32),
                   jax.ShapeDtypeStruct((bsz, H_D, seq, 2 * HD_D), BF16),
                   jax.ShapeDtypeStruct((bsz, 2 * KV_D, seq, 2 * HD_D), BF16),
                   jax.ShapeDtypeStruct((bsz, KV_D, seq, 2 * HD_D), BF16)],
        grid=(bsz, seq // T),
        in_specs=[tok(H_D * HD_D), tok(KV_D * HD_D), tok(KV_D * HD_D), tok(LANES),
                  pl.BlockSpec((1, LANES), lambda b, c: (0, 0))],
        out_specs=[tok(LANES), heads(H_D), heads(2 * KV_D), heads(KV_D)],
        scratch_shapes=[pltpu.VMEM((1, LANES), F32)],
        compiler_params=_params("parallel", "arbitrary"),
        name="fox_prep",
    )(q, k, v, fd, fbias_pad)


def _fox_flash_kernel(qi_ref, ki_ref, qm_ref, k2_ref, v2_ref, o_ref, m_scr, l_scr, acc_scr, *, TQ, TK):
    s_id = pl.program_id(1)
    qi = qi_ref[s_id]
    ki = ki_ref[s_id]
    group = H_D // KV_D

    @pl.when(ki == 0)
    def _():
        m_scr[...] = jnp.full_like(m_scr, NEG_INF)
        l_scr[...] = jnp.zeros_like(l_scr)
        acc_scr[...] = jnp.zeros_like(acc_scr)

    def head_step(h, masked):
        kv = h // group
        s = _dot_nt(qm_ref[0, h], k2_ref[0, 2 * kv + h % 2])
        if masked:
            r = lax.broadcasted_iota(jnp.int32, (TQ, TK), 0)
            c = lax.broadcasted_iota(jnp.int32, (TQ, TK), 1)
            s = jnp.where(c <= r, s, NEG_INF)
        m_prev = m_scr[h]
        m_new = jnp.maximum(m_prev, jnp.max(s, axis=-1, keepdims=True))
        alpha = jnp.exp(m_prev - m_new)
        p = jnp.exp(s - pltpu.repeat(m_new, TK // LANES, axis=1))
        l_scr[h] = alpha * l_scr[h] + jnp.sum(p, axis=-1, keepdims=True)
        acc_scr[h] = alpha * acc_scr[h] + _dot(p.astype(BF16), v2_ref[0, kv])
        m_scr[h] = m_new

    def all_heads(masked):
        def body(h, carry):
            head_step(h, masked)
            return carry
        lax.fori_loop(0, H_D, body, 0, unroll=2)

    @pl.when(ki < qi)
    def _():
        all_heads(False)

    @pl.when(ki == qi)
    def _():
        all_heads(True)
        lane = lax.broadcasted_iota(jnp.int32, (1, 2 * HD_D), 1)
        low = lane < HD_D
        for j in range(H_D // 2):
            o0 = acc_scr[2 * j] / l_scr[2 * j]
            o1 = acc_scr[2 * j + 1] / l_scr[2 * j + 1]
            o_ref[0, :, j * 2 * HD_D:(j + 1) * 2 * HD_D] = jnp.where(low, o0, o1)


def _fox_flash(qm, k2, v2, TQ, TK):
    bsz, _, seq, _ = qm.shape
    assert TQ == TK
    nq = seq // TQ
    pairs = [(a, b) for a in range(nq) for b in range(a + 1)]
    qi_tab = jnp.asarray([p[0] for p in pairs], jnp.int32)
    ki_tab = jnp.asarray([p[1] for p in pairs], jnp.int32)
    kern = functools.partial(_fox_flash_kernel, TQ=TQ, TK=TK)
    qw = H_D * HD_D
    grid_spec = pltpu.PrefetchScalarGridSpec(
        num_scalar_prefetch=2,
        grid=(bsz, len(pairs)),
        in_specs=[pl.BlockSpec((1, H_D, TQ, 2 * HD_D), lambda b, s, qt, kt: (b, 0, qt[s], 0)),
                  pl.BlockSpec((1, 2 * KV_D, TK, 2 * HD_D), lambda b, s, qt, kt: (b, 0, kt[s], 0)),
                  pl.BlockSpec((1, KV_D, TK, 2 * HD_D), lambda b, s, qt, kt: (b, 0, kt[s], 0))],
        out_specs=pl.BlockSpec((1, TQ, qw), lambda b, s, qt, kt: (b, qt[s], 0)),
        scratch_shapes=[pltpu.VMEM((H_D, TQ, LANES), F32),
                        pltpu.VMEM((H_D, TQ, LANES), F32),
                        pltpu.VMEM((H_D, TQ, 2 * HD_D), F32)])
    return pl.pallas_call(
        kern,
        out_shape=jax.ShapeDtypeStruct((bsz, seq, qw), F32),
        grid_spec=grid_spec,
        compiler_params=_params("parallel", "arbitrary"),
        name="fox_flash",
    )(qi_tab, ki_tab, qm, k2, v2)


def _fox_paged_kernel(pt_ref, q_ref, kn_ref, vn_ref, fdn_ref, fb_ref, *rest, PP, NG):
    kt_refs = rest[0:PP]
    vt_refs = rest[PP:2 * PP]
    lft_refs = rest[2 * PP:3 * PP]
    o_ref, lfn_ref, qall_scr, m_scr, l_scr, acc_scr, carry_scr = rest[3 * PP:]
    del pt_ref
    g = pl.program_id(1)
    L = q_ref.shape[1]
    rows = H_D * L
    group = H_D // KV_D
    W = PP * PAGE_SIZE

    @pl.when(g == 0)
    def _():
        q = q_ref[0] * SCALE_D
        for h in range(H_D):
            kv = h // group
            piece = q[:, h * HD_D:(h + 1) * HD_D]
            parts = []
            if kv > 0:
                parts.append(jnp.zeros((L, kv * HD_D), F32))
            parts.append(piece)
            if kv < KV_D - 1:
                parts.append(jnp.zeros((L, (KV_D - 1 - kv) * HD_D), F32))
            qall_scr[h * L:(h + 1) * L, :] = jnp.concatenate(parts, axis=1).astype(BF16)
        lfn = -_softplus(-(fdn_ref[0] + fb_ref[...]))
        lfn_ref[0] = lfn
        cumn = _exact_dot_rhs(_lower_tri_bf16(L), lfn)
        cumn_t = cumn.T[0:H_D, :]
        bias = jnp.broadcast_to((-cumn_t)[:, None, :], (H_D, L, L)).reshape(rows, L)
        s = _dot_nt(qall_scr[...], kn_ref[0].astype(BF16)) + bias
        qidx = lax.broadcasted_iota(jnp.int32, (rows, L), 0) % L
        kidx = lax.broadcasted_iota(jnp.int32, (rows, L), 1)
        s = jnp.where(kidx <= qidx, s, NEG_INF)
        mx = jnp.max(s, axis=-1, keepdims=True)
        p = jnp.exp(s - mx)
        m_scr[...] = mx
        l_scr[...] = jnp.sum(p, axis=-1, keepdims=True)
        acc_scr[...] = _dot(p.astype(BF16), vn_ref[0].astype(BF16))
        carry_scr[...] = jnp.zeros_like(carry_scr)

    kcat = jnp.concatenate([r[0] for r in kt_refs], axis=1).astype(BF16)
    vcat = jnp.concatenate([r[0] for r in vt_refs], axis=1).astype(BF16)
    s = _dot(qall_scr[...], kcat)
    lft = jnp.concatenate([r[0] for r in lft_refs], axis=1)
    lane_in_page = lax.broadcasted_iota(jnp.int32, (1, W), 1) % PAGE_SIZE
    y = lft
    step = 1
    while step < PAGE_SIZE:
        y = y + jnp.where(lane_in_page < PAGE_SIZE - step, pltpu.roll(y, W - step, axis=1), 0.0)
        step *= 2
    d_local = y - lft
    carry = carry_scr[...]
    pieces = []
    for i in range(PP):
        pieces.append(d_local[:, i * PAGE_SIZE:(i + 1) * PAGE_SIZE] + carry)
        carry = carry + y[:, i * PAGE_SIZE:i * PAGE_SIZE + 1]
    carry_scr[...] = carry
    bias_t = jnp.concatenate(pieces, axis=1)
    s = s + jnp.broadcast_to(bias_t[:, None, :], (H_D, L, W)).reshape(rows, W)
    m_prev = m_scr[...]
    m_new = jnp.maximum(m_prev, jnp.max(s, axis=-1, keepdims=True))
    alpha = jnp.exp(m_prev - m_new)
    p = jnp.exp(s - m_new)
    l_scr[...] = alpha * l_scr[...] + jnp.sum(p, axis=-1, keepdims=True)
    acc_scr[...] = alpha * acc_scr[...] + _dot_nt(p.astype(BF16), vcat)
    m_scr[...] = m_new

    @pl.when(g == NG - 1)
    def _():
        o = acc_scr[...] / l_scr[...]
        parts = []
        for h in range(H_D):
            kv = h // group
            parts.append(o[h * L:(h + 1) * L, kv * HD_D:(kv + 1) * HD_D])
        o_ref[0] = jnp.concatenate(parts, axis=1)


def _fox_paged(q, k_new, v_new, fd_new, fbias_pad, pool_k, pool_v, pool_lf, page_table, layer, PP):
    bsz, L, qw = q.shape
    n_pool = pool_k.shape[1]
    n_pages = page_table.shape[1]
    NG = n_pages // PP
    kvw = KV_D * HD_D
    pkt = jnp.transpose(pool_k, (0, 1, 3, 4, 2)).reshape(pool_k.shape[0] * n_pool, kvw, PAGE_SIZE)
    pvt = jnp.transpose(pool_v, (0, 1, 3, 4, 2)).reshape(pool_v.shape[0] * n_pool, kvw, PAGE_SIZE)
    plft = jnp.transpose(pool_lf, (0, 1, 3, 2)).reshape(pool_lf.shape[0] * n_pool, H_D, PAGE_SIZE)
    base = layer * n_pool

    def page_map(i):
        return lambda b, g, pt: (base + pt[b, n_pages - 1 - (g * PP + i)], 0, 0)

    seq_map = lambda b, g, pt: (b, 0, 0)
    in_specs = [pl.BlockSpec((1, L, qw), seq_map),
                pl.BlockSpec((1, L, kvw), seq_map),
                pl.BlockSpec((1, L, kvw), seq_map),
                pl.BlockSpec((1, L, LANES), seq_map),
                pl.BlockSpec((1, LANES), lambda b, g, pt: (0, 0))]
    in_specs += [pl.BlockSpec((1, kvw, PAGE_SIZE), page_map(i)) for i in range(PP)]
    in_specs += [pl.BlockSpec((1, kvw, PAGE_SIZE), page_map(i)) for i in range(PP)]
    in_specs += [pl.BlockSpec((1, H_D, PAGE_SIZE), page_map(i)) for i in range(PP)]
    kern = functools.partial(_fox_paged_kernel, PP=PP, NG=NG)
    grid_spec = pltpu.PrefetchScalarGridSpec(
        num_scalar_prefetch=1,
        grid=(bsz, NG),
        in_specs=in_specs,
        out_specs=[pl.BlockSpec((1, L, qw), seq_map), pl.BlockSpec((1, L, LANES), seq_map)],
        scratch_shapes=[pltpu.VMEM((H_D * L, kvw), BF16),
                        pltpu.VMEM((H_D * L, 1), F32),
                        pltpu.VMEM((H_D * L, 1), F32),
                        pltpu.VMEM((H_D * L, kvw), F32),
                        pltpu.VMEM((H_D, 1), F32)])
    return pl.pallas_call(
        kern,
        out_shape=[jax.ShapeDtypeStruct((bsz, L, qw), F32), jax.ShapeDtypeStruct((bsz, L, LANES), F32)],
        grid_spec=grid_spec,
        compiler_params=_params("parallel", "arbitrary"),
        name="fox_paged",
    )(page_table, q, k_new, v_new, fd_new, fbias_pad, *([pkt] * PP), *([pvt] * PP), *([plft] * PP))


def _pad_cols(w, n):
    return jnp.pad(w, ((0, 0), (0, n - w.shape[1])))


def _prep_w_ab(w):
    z, xbc, dt, q, k, v = jnp.split(w, np.cumsum((D_INNER_A, CONV_DIM_A, H_A, H_B * HD_B, KV_B * HD_B, KV_B * HD_B))[:-1].tolist(), axis=1)
    return jnp.concatenate([z, xbc, q, k, v, _pad_cols(dt, LANES)], axis=1).astype(BF16)


AB_SIZES = (D_INNER_A, CONV_DIM_A, H_B * HD_B, KV_B * HD_B, KV_B * HD_B, LANES)


def _prep_w_cd(w):
    sizes = (H_C * DK_C, H_C * DK_C, H_C * DV_C, H_C * DV_C, H_D * HD_D, KV_D * HD_D, KV_D * HD_D, H_D)
    parts = jnp.split(w, np.cumsum(sizes)[:-1].tolist(), axis=1)
    parts[-1] = _pad_cols(parts[-1], LANES)
    return jnp.concatenate(parts, axis=1).astype(BF16)


CD_SIZES = (H_C * DK_C, H_C * DK_C, H_C * DV_C, H_C * DV_C, H_D * HD_D, KV_D * HD_D, KV_D * HD_D, LANES)


def _trunk(x, c, st, prm, page_table):
    sample = page_table is not None
    bsz, seq, _ = x.shape
    if sample:
        bb, tl, ffn_tl = min(SAMPLE_BATCH_TILE, bsz), seq, seq
    else:
        bb, tl, ffn_tl = 1, min(ROW_TILE, seq), min(FFN_ROW_TILE, seq)
    n_sub = DEPTH * 2
    m_all = _ada_all(c, prm['ada_w'].reshape(n_sub, D_MODEL, 3 * D_MODEL), prm['ada_b'].reshape(n_sub, 1, 3 * D_MODEL))
    new = {}
    for l in range(DEPTH):
        j = l // 2
        m_mix = m_all[2 * l][:, None, :]
        m_ffn = m_all[2 * l + 1][:, None, :]
        if l % 2 == 0:
            z, xbc, qb, kb, vb, dtp = _mod_matmul(x, m_mix, _prep_w_ab(prm['w_in_ab'][j]), AB_SIZES, bb, tl)
            if sample:
                conv0p = jnp.pad(st['conv_a'][j], ((0, 0), (SUBLANES - (CONV_A - 1), 0), (0, 0)))
                ssm0 = st['ssm_a'][j]
            else:
                conv0p, ssm0 = None, None
            ya, conv1p, ssm1 = _mamba(z, xbc, dtp, conv0p, ssm0, prm['conv_w_a'][j], prm['conv_b_a'][j],
                                      prm['dt_bias_a'][j], prm['a_log_a'][j], prm['d_skip_a'][j],
                                      prm['norm_w_a'][j], T=min(SSD_CHUNK, seq))
            if sample:
                kbuf = st['swa_k'][j].reshape(bsz, WINDOW, KV_B * HD_B)
                vbuf = st['swa_v'][j].reshape(bsz, WINDOW, KV_B * HD_B)
                yb = _swa(qb, kbuf, kb, vbuf, vb, prm['sinks_b'][j], TQ=seq, prev_is_same_array=False)
                bk = jnp.concatenate([kbuf[:, seq:], kb], axis=1)
                bv = jnp.concatenate([vbuf[:, seq:], vb], axis=1)
            else:
                yb = _swa(qb, kb, kb, vb, vb, prm['sinks_b'][j], TQ=WINDOW, prev_is_same_array=True)
                bk, bv = kb[:, -WINDOW:], vb[:, -WINDOW:]
            w_out = prm['w_out_ab'][j].astype(BF16)
            x = _mm_res_ln([ya, yb], [w_out[:D_INNER_A], w_out[D_INNER_A:]], x, m_mix,
                           prm['ln_g'][l, 0][None, :], prm['ln_b'][l, 0][None, :], bb, tl)
            new.setdefault('ssm_a', []).append(ssm1)
            new.setdefault('conv_a', []).append(conv1p[:, SUBLANES - (CONV_A - 1):, :])
            new.setdefault('swa_k', []).append(bk.reshape(bsz, WINDOW, KV_B, HD_B))
            new.setdefault('swa_v', []).append(bv.reshape(bsz, WINDOW, KV_B, HD_B))
        else:
            qc, fc, ic, gc, qd, kd, vd, fdp = _mod_matmul(x, m_mix, _prep_w_cd(prm['w_in_cd'][j]), CD_SIZES, bb, tl)
            s0 = st['hgrn_c'][j] if sample else None
            yc, s1 = _gla(qc, fc, ic, gc, prm['lb_c'], prm['gnorm_c'][j], s0, layer=l, T=min(GLA_CHUNK, seq))
            fbias_pad = jnp.zeros((1, LANES), F32).at[0, :H_D].set(prm['fbias_d'][j])
            if sample:
                yd, lfp = _fox_paged(qd, kd, vd, fdp, fbias_pad, st['fox_k'], st['fox_v'], st['fox_logf'],
                                     page_table, j, FOX_PAGES_PER_STEP)
            else:
                lfp, qm, k2, v2 = _fox_prep(qd, kd, vd, fdp, fbias_pad, min(FOX_PREP_T, seq))
                yd = _fox_flash(qm, k2, v2, min(FOX_TQ, seq), min(FOX_TK, seq))
            w_out = prm['w_out_cd'][j].astype(BF16)
            x = _mm_res_ln([yc, yd], [w_out[:H_C * DV_C], w_out[H_C * DV_C:]], x, m_mix,
                           prm['ln_g'][l, 0][None, :], prm['ln_b'][l, 0][None, :], bb, tl)
            new.setdefault('hgrn_c', []).append(s1)
            new.setdefault('fox_k', []).append(kd.reshape(bsz, seq, KV_D, HD_D))
            new.setdefault('fox_v', []).append(vd.reshape(bsz, seq, KV_D, HD_D))
            new.setdefault('fox_logf', []).append(lfp[:, :, :H_D])
        buf0 = st['ffn_conv'][l] if sample else jnp.zeros((bsz, FFN_CONV - 1, D_FF), F32)
        a, buf1 = _ffn_in(x, m_ffn, prm['ffn_w_in'][l].astype(BF16), buf0, prm['ffn_conv_w'][l],
                          prm['ffn_conv_b'][l][None, :], bb, ffn_tl)
        x = _mm_res_ln([a], [prm['ffn_w_out'][l].astype(BF16)], x, m_ffn,
                       prm['ln_g'][l, 1][None, :], prm['ln_b'][l, 1][None, :], bb, ffn_tl)
        new.setdefault('ffn_conv', []).append(buf1)
    return x, {name: jnp.stack(rows, axis=0) for name, rows in new.items()}


def kernel(x_prompt, x_sample, state_ssm_a, state_conv_a, cache_swa_k, cache_swa_v, state_hgrn_c, cache_fox_k, cache_fox_v, cache_fox_logf, state_ffn_conv, page_table, c_prompt, c_sample, ada_w, ada_b, ln_g, ln_b, w_in_ab, w_out_ab, conv_w_a, conv_b_a, dt_bias_a, a_log_a, d_skip_a, norm_w_a, sinks_b, w_in_cd, w_out_cd, lb_c, gnorm_c, fbias_d, ffn_w_in, ffn_conv_w, ffn_conv_b, ffn_w_out):
    prm = dict(ada_w=ada_w, ada_b=ada_b, ln_g=ln_g, ln_b=ln_b, w_in_ab=w_in_ab, w_out_ab=w_out_ab,
               conv_w_a=conv_w_a, conv_b_a=conv_b_a, dt_bias_a=dt_bias_a, a_log_a=a_log_a,
               d_skip_a=d_skip_a, norm_w_a=norm_w_a, sinks_b=sinks_b, w_in_cd=w_in_cd, w_out_cd=w_out_cd,
               lb_c=lb_c, gnorm_c=gnorm_c, fbias_d=fbias_d, ffn_w_in=ffn_w_in, ffn_conv_w=ffn_conv_w,
               ffn_conv_b=ffn_conv_b, ffn_w_out=ffn_w_out)
    st = dict(ssm_a=state_ssm_a, conv_a=state_conv_a, swa_k=cache_swa_k, swa_v=cache_swa_v,
              hgrn_c=state_hgrn_c, fox_k=cache_fox_k, fox_v=cache_fox_v, fox_logf=cache_fox_logf,
              ffn_conv=state_ffn_conv)
    y_p, sp = _trunk(x_prompt, c_prompt, None, prm, None)
    y_s, ss = _trunk(x_sample, c_sample, st, prm, page_table)
    return (y_p, y_s,
            sp['ssm_a'], ss['ssm_a'], sp['conv_a'], ss['conv_a'],
            sp['swa_k'], ss['swa_k'], sp['swa_v'], ss['swa_v'],
            sp['hgrn_c'], ss['hgrn_c'],
            sp['fox_k'], ss['fox_k'], sp['fox_v'], ss['fox_v'], sp['fox_logf'], ss['fox_logf'],
            sp['ffn_conv'], ss['ffn_conv'])
```

```python
import functools
import math

import numpy as np
import jax
import jax.numpy as jnp
from jax import lax
from jax.experimental import pallas as pl
from jax.experimental.pallas import tpu as pltpu

F32 = jnp.float32
BF16 = jnp.bfloat16
NEG_INF = float("-inf")

D_MODEL = 1024
DEPTH = 2
PAGE_SIZE = 128
H_A, P_A, N_A, G_A, CONV_A = 16, 64, 128, 2, 4
D_INNER_A = H_A * P_A
CONV_DIM_A = D_INNER_A + 2 * G_A * N_A
H_B, KV_B, HD_B, WINDOW = 8, 2, 64, 128
H_C, DK_C, DV_C = 4, 128, 128
H_D, KV_D, HD_D = 16, 4, 64
D_FF, FFN_CONV = 2816, 3
ALPHA = (2 * DEPTH) ** 0.25
SCALE_B = HD_B ** -0.5
SCALE_D = HD_D ** -0.5
LN_EPS = 1e-5
RMS_EPS = 1e-6

LANES = 128
SUBLANES = 8
VMEM_LIMIT_BYTES = 56 * 1024 * 1024

SSD_CHUNK = 128
GLA_CHUNK = 256
GLA_BLOCK = 16
FOX_TQ = 512
FOX_TK = 512
FOX_PREP_T = 512
FOX_PAGES_PER_STEP = 16
ROW_TILE = 512
FFN_ROW_TILE = 256
SAMPLE_BATCH_TILE = 32
SWA_SAMPLE_SEQS = 8
FOX_HEAD_UNROLL = 4


def _params(*sem):
    return pltpu.CompilerParams(dimension_semantics=sem, vmem_limit_bytes=VMEM_LIMIT_BYTES)


def _resident(shape):
    nd = len(shape)
    return pl.BlockSpec(shape, lambda *_: (0,) * nd, pipeline_mode=pl.Buffered(1))


def _silu(x):
    return x * jax.nn.sigmoid(x)


def _softplus(x):
    return jnp.maximum(x, 0.0) + jnp.log1p(jnp.exp(-jnp.abs(x)))


def _dot(a, b):
    return jnp.dot(a, b, preferred_element_type=F32)


def _dot_nt(a, b):
    return lax.dot_general(a, b, (((1,), (1,)), ((), ())), preferred_element_type=F32)


def _dot_tn(a, b):
    return lax.dot_general(a, b, (((0,), (0,)), ((), ())), preferred_element_type=F32)


def _split3(x):
    hi = x.astype(BF16)
    r = x - hi.astype(F32)
    mid = r.astype(BF16)
    lo = (r - mid.astype(F32)).astype(BF16)
    return hi, mid, lo


def _exact_dot_lhs(x, w01):
    hi, mid, lo = _split3(x)
    return _dot(hi, w01) + _dot(mid, w01) + _dot(lo, w01)


def _exact_dot_rhs(w01, x):
    hi, mid, lo = _split3(x)
    return _dot(w01, hi) + _dot(w01, mid) + _dot(w01, lo)


def _lower_tri_bf16(t):
    r = lax.broadcasted_iota(jnp.int32, (t, t), 0)
    c = lax.broadcasted_iota(jnp.int32, (t, t), 1)
    return jnp.where(r >= c, 1.0, 0.0).astype(BF16)


def _layer_norm_rows(r, g, b):
    mu = jnp.mean(r, axis=-1, keepdims=True)
    d = r - mu
    var = jnp.mean(d * d, axis=-1, keepdims=True)
    return d * lax.rsqrt(var + LN_EPS) * g + b


def _ada_kernel(c_ref, w_ref, b_ref, o_ref):
    s = _silu(c_ref[...]).astype(BF16)
    o_ref[0] = _dot(s, w_ref[0].astype(BF16)) + b_ref[0]


def _ada_all(c, ada_w, ada_b):
    bsz = c.shape[0]
    n_sub = ada_w.shape[0]
    n_out = ada_w.shape[2]
    tn = D_MODEL
    return pl.pallas_call(
        _ada_kernel,
        out_shape=jax.ShapeDtypeStruct((n_sub, bsz, n_out), F32),
        grid=(n_sub, n_out // tn),
        in_specs=[pl.BlockSpec((bsz, D_MODEL), lambda s, j: (0, 0)),
                  pl.BlockSpec((1, D_MODEL, tn), lambda s, j: (s, 0, j)),
                  pl.BlockSpec((1, 1, tn), lambda s, j: (s, 0, j))],
        out_specs=pl.BlockSpec((1, bsz, tn), lambda s, j: (s, 0, j)),
        compiler_params=_params("parallel", "parallel"),
        name="ada_mod",
    )(c, ada_w, ada_b)


def _mod_matmul_kernel(x_ref, m_ref, w_ref, *o_refs, bb, tl, sizes):
    d = x_ref.shape[-1]
    m = m_ref[...]
    h = x_ref[...] * (1.0 + m[:, :, d:2 * d]) + m[:, :, 0:d]
    h2 = h.reshape(bb * tl, d).astype(BF16)
    off = 0
    for o_ref, n in zip(o_refs, sizes):
        o_ref[...] = _dot(h2, w_ref[:, off:off + n]).reshape(bb, tl, n)
        off += n


def _mod_matmul(x, m, w_bf16, sizes, bb, tl):
    bsz, seq, d = x.shape
    n_tot = w_bf16.shape[1]
    kern = functools.partial(_mod_matmul_kernel, bb=bb, tl=tl, sizes=tuple(sizes))
    return pl.pallas_call(
        kern,
        out_shape=[jax.ShapeDtypeStruct((bsz, seq, n), F32) for n in sizes],
        grid=(bsz // bb, seq // tl),
        in_specs=[pl.BlockSpec((bb, tl, d), lambda i, j: (i, j, 0)),
                  pl.BlockSpec((bb, 1, 3 * d), lambda i, j: (i, 0, 0)),
                  _resident((d, n_tot))],
        out_specs=[pl.BlockSpec((bb, tl, n), lambda i, j: (i, j, 0)) for n in sizes],
        compiler_params=_params("parallel", "parallel"),
        name="mod_matmul",
    )(x, m, w_bf16)


def _mm_res_ln_kernel(*refs, bb, tl, n_in):
    y_refs = refs[0:n_in]
    w_refs = refs[n_in:2 * n_in]
    x_ref, m_ref, g_ref, b_ref, o_ref = refs[2 * n_in:]
    d = x_ref.shape[-1]
    mix = None
    for y_ref, w_ref in zip(y_refs, w_refs):
        y2 = y_ref[...].reshape(bb * tl, y_ref.shape[-1]).astype(BF16)
        part = _dot(y2, w_ref[...])
        mix = part if mix is None else mix + part
    gate = m_ref[:, :, 2 * d:3 * d]
    r = ALPHA * x_ref[...] + (1.0 + gate) * mix.reshape(bb, tl, d)
    o_ref[...] = _layer_norm_rows(r, g_ref[...], b_ref[...])


def _mm_res_ln(ys, ws_bf16, x, m, g, b, bb, tl):
    bsz, seq, d = x.shape
    kern = functools.partial(_mm_res_ln_kernel, bb=bb, tl=tl, n_in=len(ys))
    return pl.pallas_call(
        kern,
        out_shape=jax.ShapeDtypeStruct((bsz, seq, d), F32),
        grid=(bsz // bb, seq // tl),
        in_specs=([pl.BlockSpec((bb, tl, y.shape[-1]), lambda i, j: (i, j, 0)) for y in ys]
                  + [_resident(w.shape) for w in ws_bf16]
                  + [pl.BlockSpec((bb, tl, d), lambda i, j: (i, j, 0)),
                     pl.BlockSpec((bb, 1, 3 * d), lambda i, j: (i, 0, 0)),
                     pl.BlockSpec((1, d), lambda i, j: (0, 0)),
                     pl.BlockSpec((1, d), lambda i, j: (0, 0))]),
        out_specs=pl.BlockSpec((bb, tl, d), lambda i, j: (i, j, 0)),
        compiler_params=_params("parallel", "parallel"),
        name="mm_res_ln",
    )(*ys, *ws_bf16, x, m, g, b)


def _ffn_in_kernel(x_ref, m_ref, w_ref, buf0_ref, cw_ref, cb_ref, a_ref, buf1_ref, carry_scr, *, bb, tl, n_col):
    d = x_ref.shape[-1]
    dff = a_ref.shape[-1]
    j = pl.program_id(1)

    @pl.when(j == 0)
    def _():
        carry_scr[...] = buf0_ref[...]

    m = m_ref[...]
    h = x_ref[...] * (1.0 + m[:, :, d:2 * d]) + m[:, :, 0:d]
    h2 = h.reshape(bb * tl, d).astype(BF16)
    rows = bb * tl
    cw = dff // n_col
    t = lax.broadcasted_iota(jnp.int32, (rows, 1), 0) % tl
    for ci in range(n_col):
        lo = ci * cw
        u = _dot(h2, w_ref[:, lo:lo + cw])
        g = _dot(h2, w_ref[:, dff + lo:dff + lo + cw])
        carry = carry_scr[:, :, lo:lo + cw]
        prev1 = jnp.broadcast_to(carry[:, 1:2, :], (bb, tl, cw)).reshape(rows, cw)
        prev0 = jnp.broadcast_to(carry[:, 0:1, :], (bb, tl, cw)).reshape(rows, cw)
        tap1 = jnp.where(t == 0, prev1, pltpu.roll(g, 1, axis=0))
        tap2 = jnp.where(t == 0, prev0, jnp.where(t == 1, prev1, pltpu.roll(g, 2, axis=0)))
        y = cb_ref[:, lo:lo + cw] + g * cw_ref[2:3, lo:lo + cw]
        y = y + tap2 * cw_ref[0:1, lo:lo + cw]
        y = y + tap1 * cw_ref[1:2, lo:lo + cw]
        a_ref[:, :, lo:lo + cw] = (_silu(y) * u).reshape(bb, tl, cw)
        last2 = g.reshape(bb, tl, cw)[:, tl - (FFN_CONV - 1):, :]
        carry_scr[:, :, lo:lo + cw] = last2
        buf1_ref[:, :, lo:lo + cw] = last2


def _ffn_in(x, m, w_bf16, buf0, conv_w, conv_b, bb, tl):
    bsz, seq, d = x.shape
    kern = functools.partial(_ffn_in_kernel, bb=bb, tl=tl, n_col=2)
    return pl.pallas_call(
        kern,
        out_shape=[jax.ShapeDtypeStruct((bsz, seq, D_FF), F32),
                   jax.ShapeDtypeStruct((bsz, FFN_CONV - 1, D_FF), F32)],
        grid=(bsz // bb, seq // tl),
        in_specs=[pl.BlockSpec((bb, tl, d), lambda i, j: (i, j, 0)),
                  pl.BlockSpec((bb, 1, 3 * d), lambda i, j: (i, 0, 0)),
                  _resident((d, 2 * D_FF)),
                  pl.BlockSpec((bb, FFN_CONV - 1, D_FF), lambda i, j: (i, 0, 0)),
                  pl.BlockSpec((FFN_CONV, D_FF), lambda i, j: (0, 0)),
                  pl.BlockSpec((1, D_FF), lambda i, j: (0, 0))],
        out_specs=[pl.BlockSpec((bb, tl, D_FF), lambda i, j: (i, j, 0)),
                   pl.BlockSpec((bb, FFN_CONV - 1, D_FF), lambda i, j: (i, 0, 0))],
        scratch_shapes=[pltpu.VMEM((bb, FFN_CONV - 1, D_FF), F32)],
        compiler_params=_params("parallel", "arbitrary"),
        name="ffn_in_conv",
    )(x, m, w_bf16, buf0, conv_w, conv_b)


def _mamba_kernel(*refs, T, has_state, n_chunks):
    if has_state:
        (z_ref, xbc_ref, dt_ref, conv0_ref, ssm0_ref, cw_ref, cb_ref, dtb_ref, alog_ref, dsk_ref, nw_ref, e_ref,
         y_ref, conv1_ref, ssm1_ref, carry_scr, s_scr) = refs
    else:
        (z_ref, xbc_ref, dt_ref, cw_ref, cb_ref, dtb_ref, alog_ref, dsk_ref, nw_ref, e_ref,
         y_ref, conv1_ref, ssm1_ref, carry_scr, s_scr) = refs
    c = pl.program_id(1)
    n_pair = H_A // 2
    cdim = CONV_DIM_A

    @pl.when(c == 0)
    def _():
        if has_state:
            carry_scr[...] = conv0_ref[0]
            for j in range(n_pair):
                s_scr[j] = jnp.concatenate([ssm0_ref[0, 2 * j], ssm0_ref[0, 2 * j + 1]], axis=1)
        else:
            carry_scr[...] = jnp.zeros_like(carry_scr)
            s_scr[...] = jnp.zeros_like(s_scr)

    xb = xbc_ref[0]
    carry = carry_scr[...]
    row8 = lax.broadcasted_iota(jnp.int32, (SUBLANES, cdim), 0)
    acc = cb_ref[...] + xb * cw_ref[CONV_A - 1:CONV_A, :]
    for dshift in (3, 2, 1):
        rolled = pltpu.roll(xb, dshift, axis=0)
        top = jnp.where(row8 < dshift, pltpu.roll(carry, dshift, axis=0), rolled[0:SUBLANES])
        sh = top if T == SUBLANES else jnp.concatenate([top, rolled[SUBLANES:]], axis=0)
        acc = acc + sh * cw_ref[CONV_A - 1 - dshift:CONV_A - dshift, :]
    last8 = xb[T - SUBLANES:T, :]
    carry_scr[...] = last8
    conv1_ref[0] = last8
    xc = _silu(acc)
    xs = xc[:, 0:D_INNER_A]
    bm = xc[:, D_INNER_A:D_INNER_A + G_A * N_A]
    cm = xc[:, D_INNER_A + G_A * N_A:cdim]

    dt = _softplus(dt_ref[0] + dtb_ref[...])
    la = dt * (-jnp.exp(alog_ref[...]))
    tri = _lower_tri_bf16(T)
    cum = _exact_dot_rhs(tri, la)
    e = e_ref[...]
    dtx = _exact_dot_lhs(dt, e)
    cumx = _exact_dot_lhs(cum, e)
    x = xs * dtx
    decx = jnp.exp(cumx)
    lastx = cumx[T - 1:T, :]
    xd = (x * jnp.exp(lastx - cumx)).astype(BF16)
    elastx = jnp.exp(lastx)
    cum_t = cum.T

    rr = lax.broadcasted_iota(jnp.int32, (T, T), 0)
    cc = lax.broadcasted_iota(jnp.int32, (T, T), 1)
    causal = rr >= cc
    lane = lax.broadcasted_iota(jnp.int32, (1, 2 * P_A), 1)
    low = lane < P_A

    cb_g, c_g, b_g = [], [], []
    for g in range(G_A):
        cg = cm[:, g * N_A:(g + 1) * N_A].astype(BF16)
        bg = bm[:, g * N_A:(g + 1) * N_A].astype(BF16)
        c_g.append(cg)
        b_g.append(bg)
        cb_g.append(_dot_nt(cg, bg))

    heads_per_group = H_A // G_A
    outs = []
    for j in range(n_pair):
        g = (2 * j) // heads_per_group
        sl = slice(j * 2 * P_A, (j + 1) * 2 * P_A)
        xj = x[:, sl]
        yj = None
        for eidx in (0, 1):
            h = 2 * j + eidx
            col = cum[:, h:h + 1]
            row = cum_t[h:h + 1, :]
            decay = jnp.exp(jnp.where(causal, col - row, NEG_INF))
            scores = (cb_g[g] * decay).astype(BF16)
            xm = jnp.where(low if eidx == 0 else jnp.logical_not(low), xj, 0.0).astype(BF16)
            part = _dot(scores, xm)
            yj = part if yj is None else yj + part
        s_old = s_scr[j]
        yj = yj + _dot(c_g[g], s_old.astype(BF16)) * decx[:, sl]
        s_scr[j] = elastx[:, sl] * s_old + _dot_tn(b_g[g], xd[:, sl])
        outs.append(yj)
    y = jnp.concatenate(outs, axis=1) + dsk_ref[...] * xs
    y = y * _silu(z_ref[0])
    y_ref[0] = y * lax.rsqrt(jnp.mean(y * y, axis=-1, keepdims=True) + RMS_EPS) * nw_ref[...]

    @pl.when(c == n_chunks - 1)
    def _():
        for j in range(n_pair):
            sj = s_scr[j]
            ssm1_ref[0, 2 * j] = sj[:, 0:P_A]
            ssm1_ref[0, 2 * j + 1] = sj[:, P_A:2 * P_A]


def _mamba(z, xbc, dtp, conv0p, ssm0, conv_w, conv_b, dt_bias, a_log, d_skip, norm_w, T):
    bsz, seq, _ = z.shape
    n_chunks = seq // T
    has_state = conv0p is not None
    dtb = jnp.zeros((1, LANES), F32).at[0, :H_A].set(dt_bias)
    alog = jnp.zeros((1, LANES), F32).at[0, :H_A].set(a_log)
    dsk = jnp.repeat(d_skip, P_A)[None, :]
    head_of_col = np.arange(D_INNER_A) // P_A
    expand = jnp.asarray((np.arange(LANES)[:, None] == head_of_col[None, :]).astype(np.float32), dtype=BF16)
    kern = functools.partial(_mamba_kernel, T=T, has_state=has_state, n_chunks=n_chunks)
    tok = lambda w: pl.BlockSpec((1, T, w), lambda b, c: (b, c, 0))
    const = lambda shape: pl.BlockSpec(shape, lambda b, c: (0,) * len(shape))
    in_specs = [tok(D_INNER_A), tok(CONV_DIM_A), tok(LANES)]
    args = [z, xbc, dtp]
    if has_state:
        in_specs += [pl.BlockSpec((1, SUBLANES, CONV_DIM_A), lambda b, c: (b, 0, 0)),
                     pl.BlockSpec((1, H_A, N_A, P_A), lambda b, c: (b, 0, 0, 0))]
        args += [conv0p, ssm0]
    in_specs += [const((CONV_A, CONV_DIM_A)), const((1, CONV_DIM_A)), const((1, LANES)), const((1, LANES)),
                 const((1, D_INNER_A)), const((1, D_INNER_A)), const((LANES, D_INNER_A))]
    args += [conv_w, conv_b[None, :], dtb, alog, dsk, norm_w[None, :], expand]
    return pl.pallas_call(
        kern,
        out_shape=[jax.ShapeDtypeStruct((bsz, seq, D_INNER_A), F32),
                   jax.ShapeDtypeStruct((bsz, SUBLANES, CONV_DIM_A), F32),
                   jax.ShapeDtypeStruct((bsz, H_A, N_A, P_A), F32)],
        grid=(bsz, n_chunks),
        in_specs=in_specs,
        out_specs=[tok(D_INNER_A),
                   pl.BlockSpec((1, SUBLANES, CONV_DIM_A), lambda b, c: (b, 0, 0)),
                   pl.BlockSpec((1, H_A, N_A, P_A), lambda b, c: (b, 0, 0, 0))],
        scratch_shapes=[pltpu.VMEM((SUBLANES, CONV_DIM_A), F32),
                        pltpu.VMEM((H_A // 2, N_A, 2 * P_A), F32)],
        compiler_params=_params("parallel", "arbitrary"),
        name="mamba_ssd",
    )(*args)


def _swa_kernel(sinks_ref, q_ref, kp_ref, kc_ref, vp_ref, vc_ref, o_ref, *, SB, TQ, first_has_prev):
    n = pl.program_id(1)
    lane = lax.broadcasted_iota(jnp.int32, (1, 2 * HD_B), 1)
    low = lane < HD_B
    roll64 = lambda a: pltpu.roll(a, HD_B, axis=1)
    has_prev = jnp.logical_or(n > 0, first_has_prev)
    group = H_B // KV_B
    classes = [[h for h in range(H_B) if ((h // group) == (h % 2)) == flag] for flag in (True, False)]
    R = len(classes[0]) * TQ
    ip = lax.broadcasted_iota(jnp.int32, (R, WINDOW), 0) % TQ
    jp = lax.broadcasted_iota(jnp.int32, (R, WINDOW), 1)
    valid_prev = jnp.logical_and(jp >= ip, has_prev)
    ic = lax.broadcasted_iota(jnp.int32, (R, TQ), 0) % TQ
    jc = lax.broadcasted_iota(jnp.int32, (R, TQ), 1)
    valid_cur = jc <= ic
    for sb in range(SB):
        q = q_ref[sb] * SCALE_B
        kv_arrays = (kp_ref[sb], kc_ref[sb], vp_ref[sb], vc_ref[sb])
        res = {}
        for cls, heads in enumerate(classes):
            kpv, kcv, vpv, vcv = [(a if cls == 0 else roll64(a)).astype(BF16) for a in kv_arrays]
            qs = jnp.concatenate(
                [jnp.where(low if h % 2 == 0 else jnp.logical_not(low),
                           q[:, (h // 2) * 2 * HD_B:(h // 2 + 1) * 2 * HD_B], 0.0) for h in heads],
                axis=0).astype(BF16)
            sink = jnp.concatenate([jnp.full((TQ, 1), sinks_ref[h], F32) for h in heads], axis=0)
            sp = jnp.where(valid_prev, _dot_nt(qs, kpv), NEG_INF)
            sc = jnp.where(valid_cur, _dot_nt(qs, kcv), NEG_INF)
            mx = jnp.maximum(jnp.maximum(jnp.max(sp, axis=-1, keepdims=True),
                                         jnp.max(sc, axis=-1, keepdims=True)), sink)
            pp = jnp.exp(sp - mx)
            pc = jnp.exp(sc - mx)
            den = (jnp.sum(pp, axis=-1, keepdims=True) + jnp.sum(pc, axis=-1, keepdims=True)
                   + jnp.exp(sink - mx))
            o = (_dot(pp.astype(BF16), vpv) + _dot(pc.astype(BF16), vcv)) / den
            for idx, h in enumerate(heads):
                res[h] = o[idx * TQ:(idx + 1) * TQ]
        o_ref[sb] = jnp.concatenate([jnp.where(low, res[2 * j], res[2 * j + 1]) for j in range(H_B // 2)], axis=1)


def _swa(q, k_prev, k_cur, v_prev, v_cur, sinks, TQ, SB, prev_is_same_array):
    bsz, seq, _ = q.shape
    nb = seq // TQ
    kvw = KV_B * HD_B
    if prev_is_same_array:
        prev_map = lambda b, n: (b, jnp.maximum(n - 1, 0), 0)
    else:
        prev_map = lambda b, n: (b, 0, 0)
    kern = functools.partial(_swa_kernel, SB=SB, TQ=TQ, first_has_prev=not prev_is_same_array)
    return pl.pallas_call(
        kern,
        out_shape=jax.ShapeDtypeStruct((bsz, seq, H_B * HD_B), F32),
        grid=(bsz // SB, nb),
        in_specs=[pl.BlockSpec(memory_space=pltpu.SMEM),
                  pl.BlockSpec((SB, TQ, H_B * HD_B), lambda b, n: (b, n, 0)),
                  pl.BlockSpec((SB, WINDOW, kvw), prev_map),
                  pl.BlockSpec((SB, TQ, kvw), lambda b, n: (b, n, 0)),
                  pl.BlockSpec((SB, WINDOW, kvw), prev_map),
                  pl.BlockSpec((SB, TQ, kvw), lambda b, n: (b, n, 0))],
        out_specs=pl.BlockSpec((SB, TQ, H_B * HD_B), lambda b, n: (b, n, 0)),
        compiler_params=_params("parallel", "parallel"),
        name="swa_attn",
    )(sinks, q, k_prev, k_cur, v_prev, v_cur)


def _gla_kernel(*refs, T, blk, layer, has_state, n_chunks):
    if has_state:
        (qr_ref, fr_ref, ir_ref, gr_ref, lbc_ref, gw_ref, s0_ref, o_ref, s1_ref,
         st_scr, qd_scr, kd_scr, el_scr, v_scr, od_scr, oi_scr) = refs
    else:
        (qr_ref, fr_ref, ir_ref, gr_ref, lbc_ref, gw_ref, o_ref, s1_ref,
         st_scr, qd_scr, kd_scr, el_scr, v_scr, od_scr, oi_scr) = refs
    c = pl.program_id(1)
    n_blk = T // blk

    @pl.when(c == 0)
    def _():
        if has_state:
            for h in range(H_C):
                st_scr[h] = s0_ref[0, h].T
        else:
            st_scr[...] = jnp.zeros_like(st_scr)

    lbc = lbc_ref[...]
    ex = jnp.exp(lbc - jnp.max(lbc, axis=0, keepdims=True))
    sm = ex / jnp.sum(ex, axis=0, keepdims=True)
    lb = jnp.sum(sm[1:layer + 1], axis=0, keepdims=True) if layer >= 1 else jnp.zeros_like(sm[0:1])
    rr = lax.broadcasted_iota(jnp.int32, (T, T), 0)
    cc = lax.broadcasted_iota(jnp.int32, (T, T), 1)
    same_blk = (rr // blk) == (cc // blk)
    tri_blk = jnp.where(jnp.logical_and(same_blk, rr >= cc), 1.0, 0.0).astype(BF16)
    ones_blk = jnp.where(same_blk, 1.0, 0.0).astype(BF16)
    t_in_blk = lax.broadcasted_iota(jnp.int32, (T, 1), 0) % blk

    for h in range(H_C):
        sl = slice(h * DK_C, (h + 1) * DK_C)
        q = _silu(qr_ref[0, :, sl])
        fg = lb[:, sl] + (1.0 - lb[:, sl]) * jax.nn.sigmoid(fr_ref[0, :, sl])
        lf = jnp.log(fg)
        key = 1.0 - fg
        v = ir_ref[0, :, sl]
        cr = _exact_dot_rhs(tri_blk, lf)
        last = _exact_dot_rhs(ones_blk, lf)
        od = jnp.sum(q * key, axis=-1, keepdims=True) * v
        for off in range(1, blk):
            dec = jnp.exp(jnp.where(t_in_blk >= off, cr - pltpu.roll(cr, off, axis=0), NEG_INF))
            p = jnp.sum(q * dec * pltpu.roll(key, off, axis=0), axis=-1, keepdims=True)
            od = od + p * pltpu.roll(v, off, axis=0)
        od_scr[:, sl] = od
        qd_scr[:, sl] = q * jnp.exp(cr)
        kd_scr[:, sl] = key * jnp.exp(last - cr)
        el_scr[:, sl] = jnp.exp(last)
        v_scr[:, sl] = v

    def blk_step(i):
        r0 = i * blk if isinstance(i, int) else pl.multiple_of(i * blk, blk)
        rows = pl.ds(r0, blk)
        for h in range(H_C):
            sl = slice(h * DK_C, (h + 1) * DK_C)
            st = st_scr[h]
            oi_scr[rows, sl] = _dot_nt(qd_scr[rows, sl].astype(BF16), st.astype(BF16))
            st_scr[h] = (el_scr[pl.ds(r0, 1), sl] * st
                         + _dot_tn(v_scr[rows, sl].astype(BF16), kd_scr[rows, sl].astype(BF16)))

    if n_blk == 1:
        blk_step(0)
    else:
        def body(i, carry):
            blk_step(i)
            return carry
        lax.fori_loop(0, n_blk, body, 0)

    for h in range(H_C):
        sl = slice(h * DK_C, (h + 1) * DK_C)
        o = od_scr[:, sl] + oi_scr[:, sl]
        o = o * lax.rsqrt(jnp.mean(o * o, axis=-1, keepdims=True) + RMS_EPS) * gw_ref[:, sl]
        o_ref[0, :, sl] = o * _silu(gr_ref[0, :, sl])

    @pl.when(c == n_chunks - 1)
    def _():
        for h in range(H_C):
            s1_ref[0, h] = st_scr[h].T


def _gla(qr, fr, ir, gr, lb_c, gnorm_w, s0, layer, T):
    bsz, seq, w = qr.shape
    n_chunks = seq // T
    has_state = s0 is not None
    blk = min(GLA_BLOCK, T)
    kern = functools.partial(_gla_kernel, T=T, blk=blk, layer=layer, has_state=has_state, n_chunks=n_chunks)
    tok = pl.BlockSpec((1, T, w), lambda b, c: (b, c, 0))
    st_spec = pl.BlockSpec((1, H_C, DK_C, DV_C), lambda b, c: (b, 0, 0, 0))
    in_specs = [tok, tok, tok, tok,
                pl.BlockSpec((DEPTH, w), lambda b, c: (0, 0)),
                pl.BlockSpec((1, w), lambda b, c: (0, 0))]
    args = [qr, fr, ir, gr, lb_c, gnorm_w[None, :]]
    if has_state:
        in_specs.append(st_spec)
        args.append(s0)
    return pl.pallas_call(
        kern,
        out_shape=[jax.ShapeDtypeStruct((bsz, seq, w), F32),
                   jax.ShapeDtypeStruct((bsz, H_C, DK_C, DV_C), F32)],
        grid=(bsz, n_chunks),
        in_specs=in_specs,
        out_specs=[tok, st_spec],
        scratch_shapes=[pltpu.VMEM((H_C, DV_C, DK_C), F32)] + [pltpu.VMEM((T, w), F32) for _ in range(6)],
        compiler_params=_params("parallel", "arbitrary"),
        name="gla_hgrn2",
    )(*args)


BIAS_TERMS = 3
BIAS_SELF_LANE = BIAS_TERMS * H_D


def _fox_prep_kernel(q_ref, k_ref, v_ref, fd_ref, fb_ref, lf_ref, qm_ref, k2_ref, v2_ref, carry_scr, *, T):
    c = pl.program_id(1)

    @pl.when(c == 0)
    def _():
        carry_scr[...] = jnp.zeros_like(carry_scr)

    lf = -_softplus(-(fd_ref[0] + fb_ref[...]))
    lf_ref[0] = lf
    cum = _exact_dot_rhs(_lower_tri_bf16(T), lf) + carry_scr[...]
    carry_scr[...] = cum[T - 1:T, :]
    hi, mid, lo = [t.astype(F32) for t in _split3(cum)]

    lane = lax.broadcasted_iota(jnp.int32, (1, 2 * HD_D), 1)
    low = lane < HD_D
    roll_half = lambda a: pltpu.roll(a, HD_D, axis=1)
    xk_low = jnp.where(lane < H_D, -hi,
                       jnp.where(lane < 2 * H_D, pltpu.roll(-mid, H_D, axis=1),
                                 jnp.where(lane < BIAS_SELF_LANE, pltpu.roll(-lo, 2 * H_D, axis=1),
                                           jnp.where(lane < BIAS_SELF_LANE + BIAS_TERMS, 1.0, 0.0))))
    xk_high = roll_half(xk_low)
    for i in range(KV_D // 2):
        blk = slice(i * 2 * HD_D, (i + 1) * 2 * HD_D)
        kb = k_ref[0, :, blk]
        vb = v_ref[0, :, blk]
        kr = roll_half(kb)
        vr = roll_half(vb)
        for par in (0, 1):
            kv = 2 * i + par
            k_low, k_high = (kb, kr) if par == 0 else (kr, kb)
            k2_ref[0, 2 * kv] = jnp.where(low, k_low, xk_high).astype(BF16)
            k2_ref[0, 2 * kv + 1] = jnp.where(low, xk_low, k_high).astype(BF16)
            v2_ref[0, kv] = (jnp.where(low, vb, vr) if par == 0 else jnp.where(low, vr, vb)).astype(BF16)
    for h in range(H_D):
        qp = q_ref[0, :, (h // 2) * 2 * HD_D:(h // 2 + 1) * 2 * HD_D] * SCALE_D
        own = jnp.logical_or(jnp.logical_or(lane == h, lane == H_D + h), lane == 2 * H_D + h)
        xq = jnp.where(own, 1.0,
                       jnp.where(lane == BIAS_SELF_LANE, hi[:, h:h + 1],
                                 jnp.where(lane == BIAS_SELF_LANE + 1, mid[:, h:h + 1],
                                           jnp.where(lane == BIAS_SELF_LANE + 2, lo[:, h:h + 1], 0.0))))
        qm = jnp.where(low, qp, roll_half(xq)) if h % 2 == 0 else jnp.where(low, xq, qp)
        qm_ref[0, h] = qm.astype(BF16)


def _fox_prep(q, k, v, fd, fbias_pad, T):
    bsz, seq, _ = q.shape
    kern = functools.partial(_fox_prep_kernel, T=T)
    tok = lambda w: pl.BlockSpec((1, T, w), lambda b, c: (b, c, 0))
    heads = lambda n: pl.BlockSpec((1, n, T, 2 * HD_D), lambda b, c: (b, 0, c, 0))
    return pl.pallas_call(
        kern,
        out_shape=[jax.ShapeDtypeStruct((bsz, seq, LANES), F32),
                   jax.ShapeDtypeStruct((bsz, H_D, seq, 2 * HD_D), BF16),
                   jax.ShapeDtypeStruct((bsz, 2 * KV_D, seq, 2 * HD_D), BF16),
                   jax.ShapeDtypeStruct((bsz, KV_D, seq, 2 * HD_D), BF16)],
        grid=(bsz, seq // T),
        in_specs=[tok(H_D * HD_D), tok(KV_D * HD_D), tok(KV_D * HD_D), tok(LANES),
                  pl.BlockSpec((1, LANES), lambda b, c: (0, 0))],
        out_specs=[tok(LANES), heads(H_D), heads(2 * KV_D), heads(KV_D)],
        scratch_shapes=[pltpu.VMEM((1, LANES), F32)],
        compiler_params=_params("parallel", "arbitrary"),
        name="fox_prep",
    )(q, k, v, fd, fbias_pad)


def _fox_flash_kernel(qi_ref, ki_ref, qm_ref, k2_ref, v2_ref, o_ref, m_scr, l_scr, acc_scr, *, TQ, TK):
    s_id = pl.program_id(1)
    qi = qi_ref[s_id]
    ki = ki_ref[s_id]
    group = H_D // KV_D

    @pl.when(ki == 0)
    def _():
        m_scr[...] = jnp.full_like(m_scr, NEG_INF)
        l_scr[...] = jnp.zeros_like(l_scr)
        acc_scr[...] = jnp.zeros_like(acc_scr)

    def head_step(h, masked):
        kv = h // group
        s = _dot_nt(qm_ref[0, h], k2_ref[0, 2 * kv + h % 2])
        if masked:
            r = lax.broadcasted_iota(jnp.int32, (TQ, TK), 0)
            c = lax.broadcasted_iota(jnp.int32, (TQ, TK), 1)
            s = jnp.where(c <= r, s, NEG_INF)
        m_prev = m_scr[h]
        m_new = jnp.maximum(m_prev, jnp.max(s, axis=-1, keepdims=True))
        alpha = jnp.exp(m_prev - m_new)
        p = jnp.exp(s - jnp.concatenate([m_new] * (TK // LANES), axis=1))
        l_scr[h] = alpha * l_scr[h] + jnp.sum(p, axis=-1, keepdims=True)
        acc_scr[h] = alpha * acc_scr[h] + _dot(p.astype(BF16), v2_ref[0, kv])
        m_scr[h] = m_new

    def all_heads(masked):
        def body(h, carry):
            head_step(h, masked)
            return carry
        lax.fori_loop(0, H_D, body, 0, unroll=FOX_HEAD_UNROLL)

    @pl.when(ki < qi)
    def _():
        all_heads(False)

    @pl.when(ki == qi)
    def _():
        all_heads(True)
        lane = lax.broadcasted_iota(jnp.int32, (1, 2 * HD_D), 1)
        low = lane < HD_D
        for j in range(H_D // 2):
            o0 = acc_scr[2 * j] / l_scr[2 * j]
            o1 = acc_scr[2 * j + 1] / l_scr[2 * j + 1]
            o_ref[0, :, j * 2 * HD_D:(j + 1) * 2 * HD_D] = jnp.where(low, o0, o1)


def _fox_flash(qm, k2, v2, TQ, TK):
    bsz, _, seq, _ = qm.shape
    assert TQ == TK
    nq = seq // TQ
    pairs = [(a, b) for a in range(nq) for b in range(a + 1)]
    qi_tab = jnp.asarray([p[0] for p in pairs], jnp.int32)
    ki_tab = jnp.asarray([p[1] for p in pairs], jnp.int32)
    kern = functools.partial(_fox_flash_kernel, TQ=TQ, TK=TK)
    qw = H_D * HD_D
    grid_spec = pltpu.PrefetchScalarGridSpec(
        num_scalar_prefetch=2,
        grid=(bsz, len(pairs)),
        in_specs=[pl.BlockSpec((1, H_D, TQ, 2 * HD_D), lambda b, s, qt, kt: (b, 0, qt[s], 0)),
                  pl.BlockSpec((1, 2 * KV_D, TK, 2 * HD_D), lambda b, s, qt, kt: (b, 0, kt[s], 0)),
                  pl.BlockSpec((1, KV_D, TK, 2 * HD_D), lambda b, s, qt, kt: (b, 0, kt[s], 0))],
        out_specs=pl.BlockSpec((1, TQ, qw), lambda b, s, qt, kt: (b, qt[s], 0)),
        scratch_shapes=[pltpu.VMEM((H_D, TQ, LANES), F32),
                        pltpu.VMEM((H_D, TQ, LANES), F32),
                        pltpu.VMEM((H_D, TQ, 2 * HD_D), F32)])
    return pl.pallas_call(
        kern,
        out_shape=jax.ShapeDtypeStruct((bsz, seq, qw), F32),
        grid_spec=grid_spec,
        compiler_params=_params("parallel", "arbitrary"),
        name="fox_flash",
    )(qi_tab, ki_tab, qm, k2, v2)


def _fox_paged_kernel(pt_ref, q_ref, kn_ref, vn_ref, fdn_ref, fb_ref, kt_hbm, vt_hbm, lft_hbm, o_ref, lfn_ref,
                      kbuf, vbuf, lbuf, sem, qall_scr, m_scr, l_scr, acc_scr, carry_scr, *, PP, NG, base, n_pages):
    b = pl.program_id(0)
    g = pl.program_id(1)
    n_steps = pl.num_programs(0) * NG
    t = b * NG + g
    slot = t % 2
    L = q_ref.shape[1]
    rows = H_D * L
    group = H_D // KV_D
    W = PP * PAGE_SIZE

    def page_copies(bb, gg, sl):
        cps = []
        for i in range(PP):
            pg = base + pt_ref[bb, n_pages - 1 - (gg * PP + i)]
            cps.append(pltpu.make_async_copy(kt_hbm.at[pg], kbuf.at[sl, i], sem.at[0, sl]))
            cps.append(pltpu.make_async_copy(vt_hbm.at[pg], vbuf.at[sl, i], sem.at[1, sl]))
            cps.append(pltpu.make_async_copy(lft_hbm.at[pg], lbuf.at[sl, i], sem.at[2, sl]))
        return cps

    @pl.when(t == 0)
    def _():
        for cp in page_copies(b, g, slot):
            cp.start()

    @pl.when(t + 1 < n_steps)
    def _():
        t1 = t + 1
        for cp in page_copies(t1 // NG, t1 % NG, 1 - slot):
            cp.start()

    @pl.when(g == 0)
    def _():
        q = q_ref[0] * SCALE_D
        for h in range(H_D):
            kv = h // group
            piece = q[:, h * HD_D:(h + 1) * HD_D]
            parts = []
            if kv > 0:
                parts.append(jnp.zeros((L, kv * HD_D), F32))
            parts.append(piece)
            if kv < KV_D - 1:
                parts.append(jnp.zeros((L, (KV_D - 1 - kv) * HD_D), F32))
            qall_scr[h * L:(h + 1) * L, :] = jnp.concatenate(parts, axis=1).astype(BF16)
        lfn = -_softplus(-(fdn_ref[0] + fb_ref[...]))
        lfn_ref[0] = lfn
        cumn = _exact_dot_rhs(_lower_tri_bf16(L), lfn)
        cumn_t = cumn.T[0:H_D, :]
        bias = jnp.broadcast_to((-cumn_t)[:, None, :], (H_D, L, L)).reshape(rows, L)
        s = _dot_nt(qall_scr[...], kn_ref[0].astype(BF16)) + bias
        qidx = lax.broadcasted_iota(jnp.int32, (rows, L), 0) % L
        kidx = lax.broadcasted_iota(jnp.int32, (rows, L), 1)
        s = jnp.where(kidx <= qidx, s, NEG_INF)
        mx = jnp.max(s, axis=-1, keepdims=True)
        p = jnp.exp(s - mx)
        m_scr[...] = mx
        l_scr[...] = jnp.sum(p, axis=-1, keepdims=True)
        acc_scr[...] = _dot(p.astype(BF16), vn_ref[0].astype(BF16))
        carry_scr[...] = jnp.zeros_like(carry_scr)

    for cp in page_copies(b, g, slot):
        cp.wait()
    kcat = jnp.concatenate([kbuf[slot, i] for i in range(PP)], axis=1).astype(BF16)
    vcat = jnp.concatenate([vbuf[slot, i] for i in range(PP)], axis=1).astype(BF16)
    s = _dot(qall_scr[...], kcat)
    lft = jnp.concatenate([lbuf[slot, i] for i in range(PP)], axis=1)
    lane_in_page = lax.broadcasted_iota(jnp.int32, (1, W), 1) % PAGE_SIZE
    y = lft
    step = 1
    while step < PAGE_SIZE:
        y = y + jnp.where(lane_in_page < PAGE_SIZE - step, pltpu.roll(y, W - step, axis=1), 0.0)
        step *= 2
    d_local = y - lft
    carry = carry_scr[...]
    pieces = []
    for i in range(PP):
        pieces.append(d_local[:, i * PAGE_SIZE:(i + 1) * PAGE_SIZE] + carry)
        carry = carry + y[:, i * PAGE_SIZE:i * PAGE_SIZE + 1]
    carry_scr[...] = carry
    bias_t = jnp.concatenate(pieces, axis=1)
    s = s + jnp.broadcast_to(bias_t[:, None, :], (H_D, L, W)).reshape(rows, W)
    m_prev = m_scr[...]
    m_new = jnp.maximum(m_prev, jnp.max(s, axis=-1, keepdims=True))
    alpha = jnp.exp(m_prev - m_new)
    p = jnp.exp(s - m_new)
    l_scr[...] = alpha * l_scr[...] + jnp.sum(p, axis=-1, keepdims=True)
    acc_scr[...] = alpha * acc_scr[...] + _dot_nt(p.astype(BF16), vcat)
    m_scr[...] = m_new

    @pl.when(g == NG - 1)
    def _():
        o = acc_scr[...] / l_scr[...]
        parts = []
        for h in range(H_D):
            kv = h // group
            parts.append(o[h * L:(h + 1) * L, kv * HD_D:(kv + 1) * HD_D])
        o_ref[0] = jnp.concatenate(parts, axis=1)


def _fox_paged(q, k_new, v_new, fd_new, fbias_pad, pool_k, pool_v, pool_lf, page_table, layer, PP):
    bsz, L, qw = q.shape
    n_pool = pool_k.shape[1]
    n_pages = page_table.shape[1]
    NG = n_pages // PP
    kvw = KV_D * HD_D
    pkt = jnp.transpose(pool_k, (0, 1, 3, 4, 2)).reshape(pool_k.shape[0] * n_pool, kvw, PAGE_SIZE)
    pvt = jnp.transpose(pool_v, (0, 1, 3, 4, 2)).reshape(pool_v.shape[0] * n_pool, kvw, PAGE_SIZE)
    plft = jnp.transpose(pool_lf, (0, 1, 3, 2)).reshape(pool_lf.shape[0] * n_pool, H_D, PAGE_SIZE)
    base = layer * n_pool

    seq_map = lambda b, g, pt: (b, 0, 0)
    in_specs = [pl.BlockSpec((1, L, qw), seq_map),
                pl.BlockSpec((1, L, kvw), seq_map),
                pl.BlockSpec((1, L, kvw), seq_map),
                pl.BlockSpec((1, L, LANES), seq_map),
                pl.BlockSpec((1, LANES), lambda b, g, pt: (0, 0)),
                pl.BlockSpec(memory_space=pl.ANY),
                pl.BlockSpec(memory_space=pl.ANY),
                pl.BlockSpec(memory_space=pl.ANY)]
    kern = functools.partial(_fox_paged_kernel, PP=PP, NG=NG, base=base, n_pages=n_pages)
    grid_spec = pltpu.PrefetchScalarGridSpec(
        num_scalar_prefetch=1,
        grid=(bsz, NG),
        in_specs=in_specs,
        out_specs=[pl.BlockSpec((1, L, qw), seq_map), pl.BlockSpec((1, L, LANES), seq_map)],
        scratch_shapes=[pltpu.VMEM((2, PP, kvw, PAGE_SIZE), F32),
                        pltpu.VMEM((2, PP, kvw, PAGE_SIZE), F32),
                        pltpu.VMEM((2, PP, H_D, PAGE_SIZE), F32),
                        pltpu.SemaphoreType.DMA((3, 2)),
                        pltpu.VMEM((H_D * L, kvw), BF16),
                        pltpu.VMEM((H_D * L, 1), F32),
                        pltpu.VMEM((H_D * L, 1), F32),
                        pltpu.VMEM((H_D * L, kvw), F32),
                        pltpu.VMEM((H_D, 1), F32)])
    return pl.pallas_call(
        kern,
        out_shape=[jax.ShapeDtypeStruct((bsz, L, qw), F32), jax.ShapeDtypeStruct((bsz, L, LANES), F32)],
        grid_spec=grid_spec,
        compiler_params=_params("arbitrary", "arbitrary"),
        name="fox_paged",
    )(page_table, q, k_new, v_new, fd_new, fbias_pad, pkt, pvt, plft)


def _pad_cols(w, n):
    return jnp.pad(w, ((0, 0), (0, n - w.shape[1])))


def _prep_w_ab(w):
    z, xbc, dt, q, k, v = jnp.split(w, np.cumsum((D_INNER_A, CONV_DIM_A, H_A, H_B * HD_B, KV_B * HD_B, KV_B * HD_B))[:-1].tolist(), axis=1)
    return jnp.concatenate([z, xbc, q, k, v, _pad_cols(dt, LANES)], axis=1).astype(BF16)


AB_SIZES = (D_INNER_A, CONV_DIM_A, H_B * HD_B, KV_B * HD_B, KV_B * HD_B, LANES)


def _prep_w_cd(w):
    sizes = (H_C * DK_C, H_C * DK_C, H_C * DV_C, H_C * DV_C, H_D * HD_D, KV_D * HD_D, KV_D * HD_D, H_D)
    parts = jnp.split(w, np.cumsum(sizes)[:-1].tolist(), axis=1)
    parts[-1] = _pad_cols(parts[-1], LANES)
    return jnp.concatenate(parts, axis=1).astype(BF16)


CD_SIZES = (H_C * DK_C, H_C * DK_C, H_C * DV_C, H_C * DV_C, H_D * HD_D, KV_D * HD_D, KV_D * HD_D, LANES)


def _trunk(x, c, st, prm, page_table):
    sample = page_table is not None
    bsz, seq, _ = x.shape
    if sample:
        bb, tl, ffn_tl = min(SAMPLE_BATCH_TILE, bsz), seq, seq
    else:
        bb, tl, ffn_tl = 1, min(ROW_TILE, seq), min(FFN_ROW_TILE, seq)
    n_sub = DEPTH * 2
    m_all = _ada_all(c, prm['ada_w'].reshape(n_sub, D_MODEL, 3 * D_MODEL), prm['ada_b'].reshape(n_sub, 1, 3 * D_MODEL))
    new = {}
    for l in range(DEPTH):
        j = l // 2
        m_mix = m_all[2 * l][:, None, :]
        m_ffn = m_all[2 * l + 1][:, None, :]
        if l % 2 == 0:
            z, xbc, qb, kb, vb, dtp = _mod_matmul(x, m_mix, _prep_w_ab(prm['w_in_ab'][j]), AB_SIZES, bb, tl)
            if sample:
                conv0p = jnp.pad(st['conv_a'][j], ((0, 0), (SUBLANES - (CONV_A - 1), 0), (0, 0)))
                ssm0 = st['ssm_a'][j]
            else:
                conv0p, ssm0 = None, None
            ya, conv1p, ssm1 = _mamba(z, xbc, dtp, conv0p, ssm0, prm['conv_w_a'][j], prm['conv_b_a'][j],
                                      prm['dt_bias_a'][j], prm['a_log_a'][j], prm['d_skip_a'][j],
                                      prm['norm_w_a'][j], T=min(SSD_CHUNK, seq))
            if sample:
                kbuf = st['swa_k'][j].reshape(bsz, WINDOW, KV_B * HD_B)
                vbuf = st['swa_v'][j].reshape(bsz, WINDOW, KV_B * HD_B)
                yb = _swa(qb, kbuf, kb, vbuf, vb, prm['sinks_b'][j], TQ=seq, SB=min(SWA_SAMPLE_SEQS, bsz),
                          prev_is_same_array=False)
                bk = jnp.concatenate([kbuf[:, seq:], kb], axis=1)
                bv = jnp.concatenate([vbuf[:, seq:], vb], axis=1)
            else:
                yb = _swa(qb, kb, kb, vb, vb, prm['sinks_b'][j], TQ=WINDOW, SB=1, prev_is_same_array=True)
                bk, bv = kb[:, -WINDOW:], vb[:, -WINDOW:]
            w_out = prm['w_out_ab'][j].astype(BF16)
            x = _mm_res_ln([ya, yb], [w_out[:D_INNER_A], w_out[D_INNER_A:]], x, m_mix,
                           prm['ln_g'][l, 0][None, :], prm['ln_b'][l, 0][None, :], bb, tl)
            new.setdefault('ssm_a', []).append(ssm1)
            new.setdefault('conv_a', []).append(conv1p[:, SUBLANES - (CONV_A - 1):, :])
            new.setdefault('swa_k', []).append(bk.reshape(bsz, WINDOW, KV_B, HD_B))
            new.setdefault('swa_v', []).append(bv.reshape(bsz, WINDOW, KV_B, HD_B))
        else:
            qc, fc, ic, gc, qd, kd, vd, fdp = _mod_matmul(x, m_mix, _prep_w_cd(prm['w_in_cd'][j]), CD_SIZES, bb, tl)
            s0 = st['hgrn_c'][j] if sample else None
            yc, s1 = _gla(qc, fc, ic, gc, prm['lb_c'], prm['gnorm_c'][j], s0, layer=l, T=min(GLA_CHUNK, seq))
            fbias_pad = jnp.zeros((1, LANES), F32).at[0, :H_D].set(prm['fbias_d'][j])
            if sample:
                yd, lfp = _fox_paged(qd, kd, vd, fdp, fbias_pad, st['fox_k'], st['fox_v'], st['fox_logf'],
                                     page_table, j, FOX_PAGES_PER_STEP)
            else:
                lfp, qm, k2, v2 = _fox_prep(qd, kd, vd, fdp, fbias_pad, min(FOX_PREP_T, seq))
                yd = _fox_flash(qm, k2, v2, min(FOX_TQ, seq), min(FOX_TK, seq))
            w_out = prm['w_out_cd'][j].astype(BF16)
            x = _mm_res_ln([yc, yd], [w_out[:H_C * DV_C], w_out[H_C * DV_C:]], x, m_mix,
                           prm['ln_g'][l, 0][None, :], prm['ln_b'][l, 0][None, :], bb, tl)
            new.setdefault('hgrn_c', []).append(s1)
            new.setdefault('fox_k', []).append(kd.reshape(bsz, seq, KV_D, HD_D))
            new.setdefault('fox_v', []).append(vd.reshape(bsz, seq, KV_D, HD_D))
            new.setdefault('fox_logf', []).append(lfp[:, :, :H_D])
        buf0 = st['ffn_conv'][l] if sample else jnp.zeros((bsz, FFN_CONV - 1, D_FF), F32)
        a, buf1 = _ffn_in(x, m_ffn, prm['ffn_w_in'][l].astype(BF16), buf0, prm['ffn_conv_w'][l],
                          prm['ffn_conv_b'][l][None, :], bb, ffn_tl)
        x = _mm_res_ln([a], [prm['ffn_w_out'][l].astype(BF16)], x, m_ffn,
                       prm['ln_g'][l, 1][None, :], prm['ln_b'][l, 1][None, :], bb, ffn_tl)
        new.setdefault('ffn_conv', []).append(buf1)
    return x, {name: jnp.stack(rows, axis=0) for name, rows in new.items()}


def kernel(x_prompt, x_sample, state_ssm_a, state_conv_a, cache_swa_k, cache_swa_v, state_hgrn_c, cache_fox_k, cache_fox_v, cache_fox_logf, state_ffn_conv, page_table, c_prompt, c_sample, ada_w, ada_b, ln_g, ln_b, w_in_ab, w_out_ab, conv_w_a, conv_b_a, dt_bias_a, a_log_a, d_skip_a, norm_w_a, sinks_b, w_in_cd, w_out_cd, lb_c, gnorm_c, fbias_d, ffn_w_in, ffn_conv_w, ffn_conv_b, ffn_w_out):
    prm = dict(ada_w=ada_w, ada_b=ada_b, ln_g=ln_g, ln_b=ln_b, w_in_ab=w_in_ab, w_out_ab=w_out_ab,
               conv_w_a=conv_w_a, conv_b_a=conv_b_a, dt_bias_a=dt_bias_a, a_log_a=a_log_a,
               d_skip_a=d_skip_a, norm_w_a=norm_w_a, sinks_b=sinks_b, w_in_cd=w_in_cd, w_out_cd=w_out_cd,
               lb_c=lb_c, gnorm_c=gnorm_c, fbias_d=fbias_d, ffn_w_in=ffn_w_in, ffn_conv_w=ffn_conv_w,
               ffn_conv_b=ffn_conv_b, ffn_w_out=ffn_w_out)
    st = dict(ssm_a=state_ssm_a, conv_a=state_conv_a, swa_k=cache_swa_k, swa_v=cache_swa_v,
              hgrn_c=state_hgrn_c, fox_k=cache_fox_k, fox_v=cache_fox_v, fox_logf=cache_fox_logf,
              ffn_conv=state_ffn_conv)
    y_p, sp = _trunk(x_prompt, c_prompt, None, prm, None)
    y_s, ss = _trunk(x_sample, c_sample, st, prm, page_table)
    return (y_p, y_s,
            sp['ssm_a'], ss['ssm_a'], sp['conv_a'], ss['conv_a'],
            sp['swa_k'], ss['swa_k'], sp['swa_v'], ss['swa_v'],
            sp['hgrn_c'], ss['hgrn_c'],
            sp['fox_k'], ss['fox_k'], sp['fox_v'], ss['fox_v'], sp['fox_logf'], ss['fox_logf'],
            sp['ffn_conv'], ss['ffn_conv'])
```

```python
import functools
import math

import numpy as np
import jax
import jax.numpy as jnp
from jax import lax
from jax.experimental import pallas as pl
from jax.experimental.pallas import tpu as pltpu

F32 = jnp.float32
BF16 = jnp.bfloat16
NEG_INF = float("-inf")

D_MODEL = 1024
DEPTH = 2
PAGE_SIZE = 128
H_A, P_A, N_A, G_A, CONV_A = 16, 64, 128, 2, 4
D_INNER_A = H_A * P_A
CONV_DIM_A = D_INNER_A + 2 * G_A * N_A
H_B, KV_B, HD_B, WINDOW = 8, 2, 64, 128
H_C, DK_C, DV_C = 4, 128, 128
H_D, KV_D, HD_D = 16, 4, 64
D_FF, FFN_CONV = 2816, 3
ALPHA = (2 * DEPTH) ** 0.25
SCALE_B = HD_B ** -0.5
SCALE_D = HD_D ** -0.5
LN_EPS = 1e-5
RMS_EPS = 1e-6
LOG2E = math.log2(math.e)

LANES = 128
SUBLANES = 8
VMEM_LIMIT_BYTES = 56 * 1024 * 1024

SSD_CHUNK = 128
GLA_CHUNK = 256
GLA_BLOCK = 16
FOX_TQ = 1024
FOX_TK = 512
FOX_HEAD_GROUPS = 2
FOX_PREP_T = 512
FOX_PAGES_PER_STEP = 16
ROW_TILE = 512
FFN_ROW_TILE = 256
SAMPLE_BATCH_TILE = 32
SWA_SAMPLE_SEQS = 8
FOX_SEQS_PER_STEP = 2


def _params(*sem):
    return pltpu.CompilerParams(dimension_semantics=sem, vmem_limit_bytes=VMEM_LIMIT_BYTES)


def _resident(shape):
    nd = len(shape)
    return pl.BlockSpec(shape, lambda *_: (0,) * nd, pipeline_mode=pl.Buffered(1))


def _silu(x):
    return x * jax.nn.sigmoid(x)


def _softplus(x):
    return jnp.maximum(x, 0.0) + jnp.log1p(jnp.exp(-jnp.abs(x)))


def _dot(a, b):
    return jnp.dot(a, b, preferred_element_type=F32)


def _dot_nt(a, b):
    return lax.dot_general(a, b, (((1,), (1,)), ((), ())), preferred_element_type=F32)


def _dot_tn(a, b):
    return lax.dot_general(a, b, (((0,), (0,)), ((), ())), preferred_element_type=F32)


def _split3(x):
    hi = x.astype(BF16)
    r = x - hi.astype(F32)
    mid = r.astype(BF16)
    lo = (r - mid.astype(F32)).astype(BF16)
    return hi, mid, lo


def _exact_dot_lhs(x, w01):
    hi, mid, lo = _split3(x)
    return _dot(hi, w01) + _dot(mid, w01) + _dot(lo, w01)


def _exact_dot_rhs(w01, x):
    hi, mid, lo = _split3(x)
    return _dot(w01, hi) + _dot(w01, mid) + _dot(w01, lo)


def _lower_tri_bf16(t):
    r = lax.broadcasted_iota(jnp.int32, (t, t), 0)
    c = lax.broadcasted_iota(jnp.int32, (t, t), 1)
    return jnp.where(r >= c, 1.0, 0.0).astype(BF16)


def _layer_norm_rows(r, g, b):
    mu = jnp.mean(r, axis=-1, keepdims=True)
    d = r - mu
    var = jnp.mean(d * d, axis=-1, keepdims=True)
    return d * lax.rsqrt(var + LN_EPS) * g + b


def _ada_kernel(c_ref, w_ref, b_ref, o_ref):
    s = _silu(c_ref[...]).astype(BF16)
    o_ref[0] = _dot(s, w_ref[0].astype(BF16)) + b_ref[0]


def _ada_all(c, ada_w, ada_b):
    bsz = c.shape[0]
    n_sub = ada_w.shape[0]
    n_out = ada_w.shape[2]
    tn = D_MODEL
    return pl.pallas_call(
        _ada_kernel,
        out_shape=jax.ShapeDtypeStruct((n_sub, bsz, n_out), F32),
        grid=(n_sub, n_out // tn),
        in_specs=[pl.BlockSpec((bsz, D_MODEL), lambda s, j: (0, 0)),
                  pl.BlockSpec((1, D_MODEL, tn), lambda s, j: (s, 0, j)),
                  pl.BlockSpec((1, 1, tn), lambda s, j: (s, 0, j))],
        out_specs=pl.BlockSpec((1, bsz, tn), lambda s, j: (s, 0, j)),
        compiler_params=_params("parallel", "parallel"),
        name="ada_mod",
    )(c, ada_w, ada_b)


def _mod_matmul_kernel(x_ref, m_ref, w_ref, *o_refs, bb, tl, sizes):
    d = x_ref.shape[-1]
    m = m_ref[...]
    h = x_ref[...] * (1.0 + m[:, :, d:2 * d]) + m[:, :, 0:d]
    h2 = h.reshape(bb * tl, d).astype(BF16)
    off = 0
    for o_ref, n in zip(o_refs, sizes):
        o_ref[...] = _dot(h2, w_ref[:, off:off + n]).reshape(bb, tl, n)
        off += n


def _mod_matmul(x, m, w_bf16, sizes, bb, tl):
    bsz, seq, d = x.shape
    n_tot = w_bf16.shape[1]
    kern = functools.partial(_mod_matmul_kernel, bb=bb, tl=tl, sizes=tuple(sizes))
    return pl.pallas_call(
        kern,
        out_shape=[jax.ShapeDtypeStruct((bsz, seq, n), F32) for n in sizes],
        grid=(bsz // bb, seq // tl),
        in_specs=[pl.BlockSpec((bb, tl, d), lambda i, j: (i, j, 0)),
                  pl.BlockSpec((bb, 1, 3 * d), lambda i, j: (i, 0, 0)),
                  _resident((d, n_tot))],
        out_specs=[pl.BlockSpec((bb, tl, n), lambda i, j: (i, j, 0)) for n in sizes],
        compiler_params=_params("parallel", "parallel"),
        name="mod_matmul",
    )(x, m, w_bf16)


def _mm_res_ln_kernel(*refs, bb, tl, n_in):
    y_refs = refs[0:n_in]
    w_refs = refs[n_in:2 * n_in]
    x_ref, m_ref, g_ref, b_ref, o_ref = refs[2 * n_in:]
    d = x_ref.shape[-1]
    mix = None
    for y_ref, w_ref in zip(y_refs, w_refs):
        y2 = y_ref[...].reshape(bb * tl, y_ref.shape[-1]).astype(BF16)
        part = _dot(y2, w_ref[...])
        mix = part if mix is None else mix + part
    gate = m_ref[:, :, 2 * d:3 * d]
    r = ALPHA * x_ref[...] + (1.0 + gate) * mix.reshape(bb, tl, d)
    o_ref[...] = _layer_norm_rows(r, g_ref[...], b_ref[...])


def _mm_res_ln(ys, ws_bf16, x, m, g, b, bb, tl):
    bsz, seq, d = x.shape
    kern = functools.partial(_mm_res_ln_kernel, bb=bb, tl=tl, n_in=len(ys))
    return pl.pallas_call(
        kern,
        out_shape=jax.ShapeDtypeStruct((bsz, seq, d), F32),
        grid=(bsz // bb, seq // tl),
        in_specs=([pl.BlockSpec((bb, tl, y.shape[-1]), lambda i, j: (i, j, 0)) for y in ys]
                  + [_resident(w.shape) for w in ws_bf16]
                  + [pl.BlockSpec((bb, tl, d), lambda i, j: (i, j, 0)),
                     pl.BlockSpec((bb, 1, 3 * d), lambda i, j: (i, 0, 0)),
                     pl.BlockSpec((1, d), lambda i, j: (0, 0)),
                     pl.BlockSpec((1, d), lambda i, j: (0, 0))]),
        out_specs=pl.BlockSpec((bb, tl, d), lambda i, j: (i, j, 0)),
        compiler_params=_params("parallel", "parallel"),
        name="mm_res_ln",
    )(*ys, *ws_bf16, x, m, g, b)


def _ffn_in_kernel(x_ref, m_ref, w_ref, buf0_ref, cw_ref, cb_ref, a_ref, buf1_ref, carry_scr, *, bb, tl, n_col):
    d = x_ref.shape[-1]
    dff = a_ref.shape[-1]
    j = pl.program_id(1)

    @pl.when(j == 0)
    def _():
        carry_scr[...] = buf0_ref[...]

    m = m_ref[...]
    h = x_ref[...] * (1.0 + m[:, :, d:2 * d]) + m[:, :, 0:d]
    h2 = h.reshape(bb * tl, d).astype(BF16)
    rows = bb * tl
    cw = dff // n_col
    t = lax.broadcasted_iota(jnp.int32, (rows, 1), 0) % tl
    for ci in range(n_col):
        lo = ci * cw
        u = _dot(h2, w_ref[:, lo:lo + cw])
        g = _dot(h2, w_ref[:, dff + lo:dff + lo + cw])
        carry = carry_scr[:, :, lo:lo + cw]
        prev1 = jnp.broadcast_to(carry[:, 1:2, :], (bb, tl, cw)).reshape(rows, cw)
        prev0 = jnp.broadcast_to(carry[:, 0:1, :], (bb, tl, cw)).reshape(rows, cw)
        tap1 = jnp.where(t == 0, prev1, pltpu.roll(g, 1, axis=0))
        tap2 = jnp.where(t == 0, prev0, jnp.where(t == 1, prev1, pltpu.roll(g, 2, axis=0)))
        y = cb_ref[:, lo:lo + cw] + g * cw_ref[2:3, lo:lo + cw]
        y = y + tap2 * cw_ref[0:1, lo:lo + cw]
        y = y + tap1 * cw_ref[1:2, lo:lo + cw]
        a_ref[:, :, lo:lo + cw] = (_silu(y) * u).reshape(bb, tl, cw)
        last2 = g.reshape(bb, tl, cw)[:, tl - (FFN_CONV - 1):, :]
        carry_scr[:, :, lo:lo + cw] = last2
        buf1_ref[:, :, lo:lo + cw] = last2


def _ffn_in(x, m, w_bf16, buf0, conv_w, conv_b, bb, tl):
    bsz, seq, d = x.shape
    kern = functools.partial(_ffn_in_kernel, bb=bb, tl=tl, n_col=2)
    return pl.pallas_call(
        kern,
        out_shape=[jax.ShapeDtypeStruct((bsz, seq, D_FF), F32),
                   jax.ShapeDtypeStruct((bsz, FFN_CONV - 1, D_FF), F32)],
        grid=(bsz // bb, seq // tl),
        in_specs=[pl.BlockSpec((bb, tl, d), lambda i, j: (i, j, 0)),
                  pl.BlockSpec((bb, 1, 3 * d), lambda i, j: (i, 0, 0)),
                  _resident((d, 2 * D_FF)),
                  pl.BlockSpec((bb, FFN_CONV - 1, D_FF), lambda i, j: (i, 0, 0)),
                  pl.BlockSpec((FFN_CONV, D_FF), lambda i, j: (0, 0)),
                  pl.BlockSpec((1, D_FF), lambda i, j: (0, 0))],
        out_specs=[pl.BlockSpec((bb, tl, D_FF), lambda i, j: (i, j, 0)),
                   pl.BlockSpec((bb, FFN_CONV - 1, D_FF), lambda i, j: (i, 0, 0))],
        scratch_shapes=[pltpu.VMEM((bb, FFN_CONV - 1, D_FF), F32)],
        compiler_params=_params("parallel", "arbitrary"),
        name="ffn_in_conv",
    )(x, m, w_bf16, buf0, conv_w, conv_b)


def _mamba_kernel(*refs, T, has_state, n_chunks):
    if has_state:
        (z_ref, xbc_ref, dt_ref, conv0_ref, ssm0_ref, cw_ref, cb_ref, dtb_ref, alog_ref, dsk_ref, nw_ref, e_ref,
         y_ref, conv1_ref, ssm1_ref, carry_scr, s_scr) = refs
    else:
        (z_ref, xbc_ref, dt_ref, cw_ref, cb_ref, dtb_ref, alog_ref, dsk_ref, nw_ref, e_ref,
         y_ref, conv1_ref, ssm1_ref, carry_scr, s_scr) = refs
    c = pl.program_id(1)
    n_pair = H_A // 2
    cdim = CONV_DIM_A

    @pl.when(c == 0)
    def _():
        if has_state:
            carry_scr[...] = conv0_ref[0]
            for j in range(n_pair):
                s_scr[j] = jnp.concatenate([ssm0_ref[0, 2 * j], ssm0_ref[0, 2 * j + 1]], axis=1)
        else:
            carry_scr[...] = jnp.zeros_like(carry_scr)
            s_scr[...] = jnp.zeros_like(s_scr)

    xb = xbc_ref[0]
    carry = carry_scr[...]
    row8 = lax.broadcasted_iota(jnp.int32, (SUBLANES, cdim), 0)
    acc = cb_ref[...] + xb * cw_ref[CONV_A - 1:CONV_A, :]
    for dshift in (3, 2, 1):
        rolled = pltpu.roll(xb, dshift, axis=0)
        top = jnp.where(row8 < dshift, pltpu.roll(carry, dshift, axis=0), rolled[0:SUBLANES])
        sh = top if T == SUBLANES else jnp.concatenate([top, rolled[SUBLANES:]], axis=0)
        acc = acc + sh * cw_ref[CONV_A - 1 - dshift:CONV_A - dshift, :]
    last8 = xb[T - SUBLANES:T, :]
    carry_scr[...] = last8
    conv1_ref[0] = last8
    xc = _silu(acc)
    xs = xc[:, 0:D_INNER_A]
    bm = xc[:, D_INNER_A:D_INNER_A + G_A * N_A]
    cm = xc[:, D_INNER_A + G_A * N_A:cdim]

    dt = _softplus(dt_ref[0] + dtb_ref[...])
    la = dt * (-jnp.exp(alog_ref[...]))
    tri = _lower_tri_bf16(T)
    cum = _exact_dot_rhs(tri, la)
    e = e_ref[...]
    dtx = _exact_dot_lhs(dt, e)
    cumx = _exact_dot_lhs(cum, e)
    x = xs * dtx
    decx = jnp.exp(cumx)
    lastx = cumx[T - 1:T, :]
    xd = (x * jnp.exp(lastx - cumx)).astype(BF16)
    elastx = jnp.exp(lastx)
    cum_t = cum.T

    rr = lax.broadcasted_iota(jnp.int32, (T, T), 0)
    cc = lax.broadcasted_iota(jnp.int32, (T, T), 1)
    causal = rr >= cc
    lane = lax.broadcasted_iota(jnp.int32, (1, 2 * P_A), 1)
    low = lane < P_A

    cb_g, c_g, b_g = [], [], []
    for g in range(G_A):
        cg = cm[:, g * N_A:(g + 1) * N_A].astype(BF16)
        bg = bm[:, g * N_A:(g + 1) * N_A].astype(BF16)
        c_g.append(cg)
        b_g.append(bg)
        cb_g.append(_dot_nt(cg, bg))

    heads_per_group = H_A // G_A
    outs = []
    for j in range(n_pair):
        g = (2 * j) // heads_per_group
        sl = slice(j * 2 * P_A, (j + 1) * 2 * P_A)
        xj = x[:, sl]
        yj = None
        for eidx in (0, 1):
            h = 2 * j + eidx
            col = cum[:, h:h + 1]
            row = cum_t[h:h + 1, :]
            decay = jnp.exp(jnp.where(causal, col - row, NEG_INF))
            scores = (cb_g[g] * decay).astype(BF16)
            xm = jnp.where(low if eidx == 0 else jnp.logical_not(low), xj, 0.0).astype(BF16)
            part = _dot(scores, xm)
            yj = part if yj is None else yj + part
        s_old = s_scr[j]
        yj = yj + _dot(c_g[g], s_old.astype(BF16)) * decx[:, sl]
        s_scr[j] = elastx[:, sl] * s_old + _dot_tn(b_g[g], xd[:, sl])
        outs.append(yj)
    y = jnp.concatenate(outs, axis=1) + dsk_ref[...] * xs
    y = y * _silu(z_ref[0])
    y_ref[0] = y * lax.rsqrt(jnp.mean(y * y, axis=-1, keepdims=True) + RMS_EPS) * nw_ref[...]

    @pl.when(c == n_chunks - 1)
    def _():
        for j in range(n_pair):
            sj = s_scr[j]
            ssm1_ref[0, 2 * j] = sj[:, 0:P_A]
            ssm1_ref[0, 2 * j + 1] = sj[:, P_A:2 * P_A]


def _mamba(z, xbc, dtp, conv0p, ssm0, conv_w, conv_b, dt_bias, a_log, d_skip, norm_w, T):
    bsz, seq, _ = z.shape
    n_chunks = seq // T
    has_state = conv0p is not None
    dtb = jnp.zeros((1, LANES), F32).at[0, :H_A].set(dt_bias)
    alog = jnp.zeros((1, LANES), F32).at[0, :H_A].set(a_log)
    dsk = jnp.repeat(d_skip, P_A)[None, :]
    head_of_col = np.arange(D_INNER_A) // P_A
    expand = jnp.asarray((np.arange(LANES)[:, None] == head_of_col[None, :]).astype(np.float32), dtype=BF16)
    kern = functools.partial(_mamba_kernel, T=T, has_state=has_state, n_chunks=n_chunks)
    tok = lambda w: pl.BlockSpec((1, T, w), lambda b, c: (b, c, 0))
    const = lambda shape: pl.BlockSpec(shape, lambda b, c: (0,) * len(shape))
    in_specs = [tok(D_INNER_A), tok(CONV_DIM_A), tok(LANES)]
    args = [z, xbc, dtp]
    if has_state:
        in_specs += [pl.BlockSpec((1, SUBLANES, CONV_DIM_A), lambda b, c: (b, 0, 0)),
                     pl.BlockSpec((1, H_A, N_A, P_A), lambda b, c: (b, 0, 0, 0))]
        args += [conv0p, ssm0]
    in_specs += [const((CONV_A, CONV_DIM_A)), const((1, CONV_DIM_A)), const((1, LANES)), const((1, LANES)),
                 const((1, D_INNER_A)), const((1, D_INNER_A)), const((LANES, D_INNER_A))]
    args += [conv_w, conv_b[None, :], dtb, alog, dsk, norm_w[None, :], expand]
    return pl.pallas_call(
        kern,
        out_shape=[jax.ShapeDtypeStruct((bsz, seq, D_INNER_A), F32),
                   jax.ShapeDtypeStruct((bsz, SUBLANES, CONV_DIM_A), F32),
                   jax.ShapeDtypeStruct((bsz, H_A, N_A, P_A), F32)],
        grid=(bsz, n_chunks),
        in_specs=in_specs,
        out_specs=[tok(D_INNER_A),
                   pl.BlockSpec((1, SUBLANES, CONV_DIM_A), lambda b, c: (b, 0, 0)),
                   pl.BlockSpec((1, H_A, N_A, P_A), lambda b, c: (b, 0, 0, 0))],
        scratch_shapes=[pltpu.VMEM((SUBLANES, CONV_DIM_A), F32),
                        pltpu.VMEM((H_A // 2, N_A, 2 * P_A), F32)],
        compiler_params=_params("parallel", "arbitrary"),
        name="mamba_ssd",
    )(*args)


def _swa_kernel(sinks_ref, q_ref, kp_ref, kc_ref, vp_ref, vc_ref, o_ref, *, SB, TQ, first_has_prev):
    n = pl.program_id(1)
    lane = lax.broadcasted_iota(jnp.int32, (1, 2 * HD_B), 1)
    low = lane < HD_B
    roll64 = lambda a: pltpu.roll(a, HD_B, axis=1)
    has_prev = jnp.logical_or(n > 0, first_has_prev)
    group = H_B // KV_B
    classes = [[h for h in range(H_B) if ((h // group) == (h % 2)) == flag] for flag in (True, False)]
    R = len(classes[0]) * TQ
    ip = lax.broadcasted_iota(jnp.int32, (R, WINDOW), 0) % TQ
    jp = lax.broadcasted_iota(jnp.int32, (R, WINDOW), 1)
    valid_prev = jnp.logical_and(jp >= ip, has_prev)
    ic = lax.broadcasted_iota(jnp.int32, (R, TQ), 0) % TQ
    jc = lax.broadcasted_iota(jnp.int32, (R, TQ), 1)
    valid_cur = jc <= ic
    for sb in range(SB):
        q = q_ref[sb] * SCALE_B
        kv_arrays = (kp_ref[sb], kc_ref[sb], vp_ref[sb], vc_ref[sb])
        res = {}
        for cls, heads in enumerate(classes):
            kpv, kcv, vpv, vcv = [(a if cls == 0 else roll64(a)).astype(BF16) for a in kv_arrays]
            qs = jnp.concatenate(
                [jnp.where(low if h % 2 == 0 else jnp.logical_not(low),
                           q[:, (h // 2) * 2 * HD_B:(h // 2 + 1) * 2 * HD_B], 0.0) for h in heads],
                axis=0).astype(BF16)
            sink = jnp.concatenate([jnp.full((TQ, 1), sinks_ref[h], F32) for h in heads], axis=0)
            sp = jnp.where(valid_prev, _dot_nt(qs, kpv), NEG_INF)
            sc = jnp.where(valid_cur, _dot_nt(qs, kcv), NEG_INF)
            mx = jnp.maximum(jnp.maximum(jnp.max(sp, axis=-1, keepdims=True),
                                         jnp.max(sc, axis=-1, keepdims=True)), sink)
            pp = jnp.exp(sp - mx)
            pc = jnp.exp(sc - mx)
            den = (jnp.sum(pp, axis=-1, keepdims=True) + jnp.sum(pc, axis=-1, keepdims=True)
                   + jnp.exp(sink - mx))
            o = (_dot(pp.astype(BF16), vpv) + _dot(pc.astype(BF16), vcv)) / den
            for idx, h in enumerate(heads):
                res[h] = o[idx * TQ:(idx + 1) * TQ]
        o_ref[sb] = jnp.concatenate([jnp.where(low, res[2 * j], res[2 * j + 1]) for j in range(H_B // 2)], axis=1)


def _swa(q, k_prev, k_cur, v_prev, v_cur, sinks, TQ, SB, prev_is_same_array):
    bsz, seq, _ = q.shape
    nb = seq // TQ
    kvw = KV_B * HD_B
    if prev_is_same_array:
        prev_map = lambda b, n: (b, jnp.maximum(n - 1, 0), 0)
    else:
        prev_map = lambda b, n: (b, 0, 0)
    kern = functools.partial(_swa_kernel, SB=SB, TQ=TQ, first_has_prev=not prev_is_same_array)
    return pl.pallas_call(
        kern,
        out_shape=jax.ShapeDtypeStruct((bsz, seq, H_B * HD_B), F32),
        grid=(bsz // SB, nb),
        in_specs=[pl.BlockSpec(memory_space=pltpu.SMEM),
                  pl.BlockSpec((SB, TQ, H_B * HD_B), lambda b, n: (b, n, 0)),
                  pl.BlockSpec((SB, WINDOW, kvw), prev_map),
                  pl.BlockSpec((SB, TQ, kvw), lambda b, n: (b, n, 0)),
                  pl.BlockSpec((SB, WINDOW, kvw), prev_map),
                  pl.BlockSpec((SB, TQ, kvw), lambda b, n: (b, n, 0))],
        out_specs=pl.BlockSpec((SB, TQ, H_B * HD_B), lambda b, n: (b, n, 0)),
        compiler_params=_params("parallel", "parallel"),
        name="swa_attn",
    )(sinks, q, k_prev, k_cur, v_prev, v_cur)


def _gla_kernel(*refs, T, blk, layer, has_state, n_chunks):
    if has_state:
        (qr_ref, fr_ref, ir_ref, gr_ref, lbc_ref, gw_ref, s0_ref, o_ref, s1_ref,
         st_scr, qd_scr, kd_scr, el_scr, v_scr, od_scr, oi_scr) = refs
    else:
        (qr_ref, fr_ref, ir_ref, gr_ref, lbc_ref, gw_ref, o_ref, s1_ref,
         st_scr, qd_scr, kd_scr, el_scr, v_scr, od_scr, oi_scr) = refs
    c = pl.program_id(1)
    n_blk = T // blk

    @pl.when(c == 0)
    def _():
        if has_state:
            for h in range(H_C):
                st_scr[h] = s0_ref[0, h].T
        else:
            st_scr[...] = jnp.zeros_like(st_scr)

    lbc = lbc_ref[...]
    ex = jnp.exp(lbc - jnp.max(lbc, axis=0, keepdims=True))
    sm = ex / jnp.sum(ex, axis=0, keepdims=True)
    lb = jnp.sum(sm[1:layer + 1], axis=0, keepdims=True) if layer >= 1 else jnp.zeros_like(sm[0:1])
    rr = lax.broadcasted_iota(jnp.int32, (T, T), 0)
    cc = lax.broadcasted_iota(jnp.int32, (T, T), 1)
    same_blk = (rr // blk) == (cc // blk)
    tri_blk = jnp.where(jnp.logical_and(same_blk, rr >= cc), 1.0, 0.0).astype(BF16)
    ones_blk = jnp.where(same_blk, 1.0, 0.0).astype(BF16)
    t_in_blk = lax.broadcasted_iota(jnp.int32, (T, 1), 0) % blk

    for h in range(H_C):
        sl = slice(h * DK_C, (h + 1) * DK_C)
        q = _silu(qr_ref[0, :, sl])
        fg = lb[:, sl] + (1.0 - lb[:, sl]) * jax.nn.sigmoid(fr_ref[0, :, sl])
        lf = jnp.log(fg)
        key = 1.0 - fg
        v = ir_ref[0, :, sl]
        cr = _exact_dot_rhs(tri_blk, lf)
        last = _exact_dot_rhs(ones_blk, lf)
        od = jnp.sum(q * key, axis=-1, keepdims=True) * v
        for off in range(1, blk):
            dec = jnp.exp(jnp.where(t_in_blk >= off, cr - pltpu.roll(cr, off, axis=0), NEG_INF))
            p = jnp.sum(q * dec * pltpu.roll(key, off, axis=0), axis=-1, keepdims=True)
            od = od + p * pltpu.roll(v, off, axis=0)
        od_scr[:, sl] = od
        qd_scr[:, sl] = q * jnp.exp(cr)
        kd_scr[:, sl] = key * jnp.exp(last - cr)
        el_scr[:, sl] = jnp.exp(last)
        v_scr[:, sl] = v

    def blk_step(i):
        r0 = i * blk if isinstance(i, int) else pl.multiple_of(i * blk, blk)
        rows = pl.ds(r0, blk)
        for h in range(H_C):
            sl = slice(h * DK_C, (h + 1) * DK_C)
            st = st_scr[h]
            oi_scr[rows, sl] = _dot_nt(qd_scr[rows, sl].astype(BF16), st.astype(BF16))
            st_scr[h] = (el_scr[pl.ds(r0, 1), sl] * st
                         + _dot_tn(v_scr[rows, sl].astype(BF16), kd_scr[rows, sl].astype(BF16)))

    if n_blk == 1:
        blk_step(0)
    else:
        def body(i, carry):
            blk_step(i)
            return carry
        lax.fori_loop(0, n_blk, body, 0, unroll=2)

    for h in range(H_C):
        sl = slice(h * DK_C, (h + 1) * DK_C)
        o = od_scr[:, sl] + oi_scr[:, sl]
        o = o * lax.rsqrt(jnp.mean(o * o, axis=-1, keepdims=True) + RMS_EPS) * gw_ref[:, sl]
        o_ref[0, :, sl] = o * _silu(gr_ref[0, :, sl])

    @pl.when(c == n_chunks - 1)
    def _():
        for h in range(H_C):
            s1_ref[0, h] = st_scr[h].T


def _gla(qr, fr, ir, gr, lb_c, gnorm_w, s0, layer, T):
    bsz, seq, w = qr.shape
    n_chunks = seq // T
    has_state = s0 is not None
    blk = min(GLA_BLOCK, T)
    kern = functools.partial(_gla_kernel, T=T, blk=blk, layer=layer, has_state=has_state, n_chunks=n_chunks)
    tok = pl.BlockSpec((1, T, w), lambda b, c: (b, c, 0))
    st_spec = pl.BlockSpec((1, H_C, DK_C, DV_C), lambda b, c: (b, 0, 0, 0))
    in_specs = [tok, tok, tok, tok,
                pl.BlockSpec((DEPTH, w), lambda b, c: (0, 0)),
                pl.BlockSpec((1, w), lambda b, c: (0, 0))]
    args = [qr, fr, ir, gr, lb_c, gnorm_w[None, :]]
    if has_state:
        in_specs.append(st_spec)
        args.append(s0)
    return pl.pallas_call(
        kern,
        out_shape=[jax.ShapeDtypeStruct((bsz, seq, w), F32),
                   jax.ShapeDtypeStruct((bsz, H_C, DK_C, DV_C), F32)],
        grid=(bsz, n_chunks),
        in_specs=in_specs,
        out_specs=[tok, st_spec],
        scratch_shapes=[pltpu.VMEM((H_C, DV_C, DK_C), F32)] + [pltpu.VMEM((T, w), F32) for _ in range(6)],
        compiler_params=_params("parallel", "arbitrary"),
        name="gla_hgrn2",
    )(*args)


BIAS_TERMS = 3
BIAS_SELF_LANE = BIAS_TERMS * H_D


def _fox_prep_kernel(q_ref, k_ref, v_ref, fd_ref, fb_ref, lf_ref, qm_ref, k2_ref, v2_ref, carry_scr, *, T):
    c = pl.program_id(1)

    @pl.when(c == 0)
    def _():
        carry_scr[...] = jnp.zeros_like(carry_scr)

    lf = -_softplus(-(fd_ref[0] + fb_ref[...]))
    lf_ref[0] = lf
    cum = _exact_dot_rhs(_lower_tri_bf16(T), lf) + carry_scr[...]
    carry_scr[...] = cum[T - 1:T, :]
    hi, mid, lo = [t.astype(F32) for t in _split3(cum * LOG2E)]

    lane = lax.broadcasted_iota(jnp.int32, (1, 2 * HD_D), 1)
    low = lane < HD_D
    roll_half = lambda a: pltpu.roll(a, HD_D, axis=1)
    xk_low = jnp.where(lane < H_D, -hi,
                       jnp.where(lane < 2 * H_D, pltpu.roll(-mid, H_D, axis=1),
                                 jnp.where(lane < BIAS_SELF_LANE, pltpu.roll(-lo, 2 * H_D, axis=1),
                                           jnp.where(lane < BIAS_SELF_LANE + BIAS_TERMS, 1.0, 0.0))))
    xk_high = roll_half(xk_low)
    for i in range(KV_D // 2):
        blk = slice(i * 2 * HD_D, (i + 1) * 2 * HD_D)
        kb = k_ref[0, :, blk]
        vb = v_ref[0, :, blk]
        kr = roll_half(kb)
        vr = roll_half(vb)
        for par in (0, 1):
            kv = 2 * i + par
            k_low, k_high = (kb, kr) if par == 0 else (kr, kb)
            k2_ref[0, 2 * kv] = jnp.where(low, k_low, xk_high).astype(BF16)
            k2_ref[0, 2 * kv + 1] = jnp.where(low, xk_low, k_high).astype(BF16)
            v2_ref[0, kv] = (jnp.where(low, vb, vr) if par == 0 else jnp.where(low, vr, vb)).astype(BF16)
    for h in range(H_D):
        qp = q_ref[0, :, (h // 2) * 2 * HD_D:(h // 2 + 1) * 2 * HD_D] * (SCALE_D * LOG2E)
        own =jnp.logical_or(jnp.logical_or(lane == h, lane == H_D + h), lane == 2 * H_D + h)
        xq = jnp.where(own, 1.0,
                       jnp.where(lane == BIAS_SELF_LANE, hi[:, h:h + 1],
                                 jnp.where(lane == BIAS_SELF_LANE + 1, mid[:, h:h + 1],
                                           jnp.where(lane == BIAS_SELF_LANE + 2, lo[:, h:h + 1], 0.0))))
        qm = jnp.where(low, qp, roll_half(xq)) if h % 2 == 0 else jnp.where(low, xq, qp)
        qm_ref[0, h] = qm.astype(BF16)


def _fox_prep(q, k, v, fd, fbias_pad, T):
    bsz, seq, _ = q.shape
    kern = functools.partial(_fox_prep_kernel, T=T)
    tok = lambda w: pl.BlockSpec((1, T, w), lambda b, c: (b, c, 0))
    heads = lambda n: pl.BlockSpec((1, n, T, 2 * HD_D), lambda b, c: (b, 0, c, 0))
    return pl.pallas_call(
        kern,
        out_shape=[jax.ShapeDtypeStruct((bsz, seq, LANES), F32),
                   jax.ShapeDtypeStruct((bsz, H_D, seq, 2 * HD_D), BF16),
                   jax.ShapeDtypeStruct((bsz, 2 * KV_D, seq, 2 * HD_D), BF16),
                   jax.ShapeDtypeStruct((bsz, KV_D, seq, 2 * HD_D), BF16)],
        grid=(bsz, seq // T),
        in_specs=[tok(H_D * HD_D), tok(KV_D * HD_D), tok(KV_D * HD_D), tok(LANES),
                  pl.BlockSpec((1, LANES), lambda b, c: (0, 0))],
        out_specs=[tok(LANES), heads(H_D), heads(2 * KV_D), heads(KV_D)],
        scratch_shapes=[pltpu.VMEM((1, LANES), F32)],
        compiler_params=_params("parallel", "arbitrary"),
        name="fox_prep",
    )(q, k, v, fd, fbias_pad)


def _fox_flash_kernel(qi_ref, ki_ref, qm_ref, k2_ref, v2_ref, o_ref, m_scr, l_scr, acc_scr, *, TQ, TK, HG):
    s_id = pl.program_id(2)
    qi = qi_ref[s_id]
    ki = ki_ref[s_id]
    group = H_D // KV_D
    n_heads = H_D // HG

    @pl.when(ki == 0)
    def _():
        m_scr[...] = jnp.full_like(m_scr, NEG_INF)
        l_scr[...] = jnp.zeros_like(l_scr)
        acc_scr[...] = jnp.zeros_like(acc_scr)

    def head_step(h, masked):
        kv = h // group
        s = _dot_nt(qm_ref[0, h], k2_ref[0, 2 * kv + h % 2])
        if masked:
            r = lax.broadcasted_iota(jnp.int32, (TQ, TK), 0)
            c = lax.broadcasted_iota(jnp.int32, (TQ, TK), 1)
            s = jnp.where(c - r <= qi * TQ - ki * TK, s, NEG_INF)
        m_prev = m_scr[h]
        m_new = jnp.maximum(m_prev, jnp.max(s, axis=-1, keepdims=True))
        alpha = jnp.exp2(m_prev - m_new)
        p = jnp.exp2(s - jnp.concatenate([m_new] * (TK // LANES), axis=1))
        l_scr[h] = alpha * l_scr[h] + jnp.sum(p, axis=-1, keepdims=True)
        acc_scr[h] = alpha * acc_scr[h] + _dot(p.astype(BF16), v2_ref[0, kv])
        m_scr[h] = m_new

    def all_heads(masked):
        for h in range(n_heads):
            head_step(h, masked)

    ratio = TQ // TK
    @pl.when(ki < qi * ratio)
    def _():
        all_heads(False)

    @pl.when(ki >= qi * ratio)
    def _():
        all_heads(True)

    @pl.when(ki == (qi + 1) * ratio - 1)
    def _():
        lane = lax.broadcasted_iota(jnp.int32, (1, 2 * HD_D), 1)
        low = lane < HD_D
        for j in range(n_heads // 2):
            o0 = acc_scr[2 * j] / l_scr[2 * j]
            o1 = acc_scr[2 * j + 1] / l_scr[2 * j + 1]
            o_ref[0, :, j * 2 * HD_D:(j + 1) * 2 * HD_D] = jnp.where(low, o0, o1)


def _fox_flash(qm, k2, v2, TQ, TK):
    bsz, _, seq, _ = qm.shape
    assert TQ % TK == 0
    nq = seq // TQ
    pairs = [(a, b) for a in range(nq) for b in range((a + 1) * (TQ // TK))]
    qi_tab = jnp.asarray([p[0] for p in pairs], jnp.int32)
    ki_tab = jnp.asarray([p[1] for p in pairs], jnp.int32)
    HG = FOX_HEAD_GROUPS
    nh, nkv = H_D // HG, KV_D // HG
    kern = functools.partial(_fox_flash_kernel, TQ=TQ, TK=TK, HG=HG)
    qw = H_D * HD_D
    grid_spec = pltpu.PrefetchScalarGridSpec(
        num_scalar_prefetch=2,
        grid=(bsz, HG, len(pairs)),
        in_specs=[pl.BlockSpec((1, nh, TQ, 2 * HD_D), lambda b, g, s, qt, kt: (b, g, qt[s], 0)),
                  pl.BlockSpec((1, 2 * nkv, TK, 2 * HD_D), lambda b, g, s, qt, kt: (b, g, kt[s], 0)),
                  pl.BlockSpec((1, nkv, TK, 2 * HD_D), lambda b, g, s, qt, kt: (b, g, kt[s], 0))],
        out_specs=pl.BlockSpec((1, TQ, qw // HG), lambda b, g, s, qt, kt: (b, qt[s], g)),
        scratch_shapes=[pltpu.VMEM((nh, TQ, LANES), F32),
                        pltpu.VMEM((nh, TQ, LANES), F32),
                        pltpu.VMEM((nh, TQ, 2 * HD_D), F32)])
    return pl.pallas_call(
        kern,
        out_shape=jax.ShapeDtypeStruct((bsz, seq, qw), F32),
        grid_spec=grid_spec,
        compiler_params=_params("parallel", "parallel", "arbitrary"),
        name="fox_flash",
    )(qi_tab, ki_tab, qm, k2, v2)


def _fox_paged_kernel(pt_ref, q_ref, kn_ref, vn_ref, fdn_ref, fb_ref, kt_hbm, vt_hbm, lft_hbm, o_ref, lfn_ref,
                      kbuf, vbuf, lbuf, sem, qall_scr, m_scr, l_scr, acc_scr, carry_scr,
                      *, SEQS, PP, NG, base, n_pages):
    b = pl.program_id(0)
    g = pl.program_id(1)
    n_steps = pl.num_programs(0) * NG
    t = b * NG + g
    slot = t % 2
    L = q_ref.shape[1]
    rows = H_D * L
    group = H_D // KV_D
    W = PP * PAGE_SIZE

    def page_copies(bb, gg, sl):
        cps = []
        for sq in range(SEQS):
            for i in range(PP):
                pg = base + pt_ref[bb * SEQS + sq, n_pages - 1 - (gg * PP + i)]
                dst = sq * PP + i
                cps.append(pltpu.make_async_copy(kt_hbm.at[pg], kbuf.at[sl, dst], sem.at[0, sl]))
                cps.append(pltpu.make_async_copy(vt_hbm.at[pg], vbuf.at[sl, dst], sem.at[1, sl]))
                cps.append(pltpu.make_async_copy(lft_hbm.at[pg], lbuf.at[sl, dst], sem.at[2, sl]))
        return cps

    @pl.when(t == 0)
    def _():
        for cp in page_copies(b, g, slot):
            cp.start()

    @pl.when(t + 1 < n_steps)
    def _():
        t1 = t + 1
        for cp in page_copies(t1 // NG, t1 % NG, 1 - slot):
            cp.start()

    @pl.when(g == 0)
    def _():
        for sq in range(SEQS):
            q = q_ref[sq] * SCALE_D
            for h in range(H_D):
                kv = h // group
                piece = q[:, h * HD_D:(h + 1) * HD_D]
                parts = []
                if kv > 0:
                    parts.append(jnp.zeros((L, kv * HD_D), F32))
                parts.append(piece)
                if kv < KV_D - 1:
                    parts.append(jnp.zeros((L, (KV_D - 1 - kv) * HD_D), F32))
                qall_scr[sq, h * L:(h + 1) * L, :] = jnp.concatenate(parts, axis=1).astype(BF16)
            lfn = -_softplus(-(fdn_ref[sq] + fb_ref[...]))
            lfn_ref[sq] = lfn
            cumn = _exact_dot_rhs(_lower_tri_bf16(L), lfn)
            cumn_t = cumn.T[0:H_D, :]
            bias = jnp.broadcast_to((-cumn_t)[:, None, :], (H_D, L, L)).reshape(rows, L)
            s = _dot_nt(qall_scr[sq], kn_ref[sq].astype(BF16)) + bias
            qidx = lax.broadcasted_iota(jnp.int32, (rows, L), 0) % L
            kidx = lax.broadcasted_iota(jnp.int32, (rows, L), 1)
            s = jnp.where(kidx <= qidx, s, NEG_INF)
            mx = jnp.max(s, axis=-1, keepdims=True)
            p = jnp.exp(s - mx)
            m_scr[sq] = mx
            l_scr[sq] = jnp.sum(p, axis=-1, keepdims=True)
            acc_scr[sq] = _dot(p.astype(BF16), vn_ref[sq].astype(BF16))
        carry_scr[...] = jnp.zeros_like(carry_scr)

    for cp in page_copies(b, g, slot):
        cp.wait()
    lane_in_page = lax.broadcasted_iota(jnp.int32, (1, W), 1) % PAGE_SIZE
    for sq in range(SEQS):
        pages = range(sq * PP, (sq + 1) * PP)
        kcat = jnp.concatenate([kbuf[slot, i] for i in pages], axis=1).astype(BF16)
        vcat = jnp.concatenate([vbuf[slot, i] for i in pages], axis=1).astype(BF16)
        s = _dot(qall_scr[sq], kcat)
        lft = jnp.concatenate([lbuf[slot, i] for i in pages], axis=1)
        y = lft
        step = 1
        while step < PAGE_SIZE:
            y = y + jnp.where(lane_in_page < PAGE_SIZE - step, pltpu.roll(y, W - step, axis=1), 0.0)
            step *= 2
        d_local = y - lft
        carry = carry_scr[sq]
        pieces = []
        for i in range(PP):
            pieces.append(d_local[:, i * PAGE_SIZE:(i + 1) * PAGE_SIZE] + carry)
            carry = carry + y[:, i * PAGE_SIZE:i * PAGE_SIZE + 1]
        carry_scr[sq] = carry
        bias_t = jnp.concatenate(pieces, axis=1)
        s = s + jnp.broadcast_to(bias_t[:, None, :], (H_D, L, W)).reshape(rows, W)
        m_prev = m_scr[sq]
        m_new = jnp.maximum(m_prev, jnp.max(s, axis=-1, keepdims=True))
        alpha = jnp.exp(m_prev - m_new)
        p = jnp.exp(s - m_new)
        l_scr[sq] = alpha * l_scr[sq] + jnp.sum(p, axis=-1, keepdims=True)
        acc_scr[sq] = alpha * acc_scr[sq] + _dot_nt(p.astype(BF16), vcat)
        m_scr[sq] = m_new

    @pl.when(g == NG - 1)
    def _():
        for sq in range(SEQS):
            o = acc_scr[sq] / l_scr[sq]
            parts = []
            for h in range(H_D):
                kv = h // group
                parts.append(o[h * L:(h + 1) * L, kv * HD_D:(kv + 1) * HD_D])
            o_ref[sq] = jnp.concatenate(parts, axis=1)


def _fox_paged(q, k_new, v_new, fd_new, fbias_pad, pool_k, pool_v, pool_lf, page_table, layer, PP):
    bsz, L, qw = q.shape
    n_pool = pool_k.shape[1]
    n_pages = page_table.shape[1]
    NG = n_pages // PP
    kvw = KV_D * HD_D
    pkt = jnp.transpose(pool_k, (0, 1, 3, 4, 2)).reshape(pool_k.shape[0] * n_pool, kvw, PAGE_SIZE)
    pvt = jnp.transpose(pool_v, (0, 1, 3, 4, 2)).reshape(pool_v.shape[0] * n_pool, kvw, PAGE_SIZE)
    plft = jnp.transpose(pool_lf, (0, 1, 3, 2)).reshape(pool_lf.shape[0] * n_pool, H_D, PAGE_SIZE)
    base = layer * n_pool

    SEQS = min(FOX_SEQS_PER_STEP, bsz)
    seq_map = lambda b, g, pt: (b, 0, 0)
    in_specs = [pl.BlockSpec((SEQS, L, qw), seq_map),
                pl.BlockSpec((SEQS, L, kvw), seq_map),
                pl.BlockSpec((SEQS, L, kvw), seq_map),
                pl.BlockSpec((SEQS, L, LANES), seq_map),
                pl.BlockSpec((1, LANES), lambda b, g, pt: (0, 0)),
                pl.BlockSpec(memory_space=pl.ANY),
                pl.BlockSpec(memory_space=pl.ANY),
                pl.BlockSpec(memory_space=pl.ANY)]
    kern = functools.partial(_fox_paged_kernel, SEQS=SEQS, PP=PP, NG=NG, base=base, n_pages=n_pages)
    grid_spec = pltpu.PrefetchScalarGridSpec(
        num_scalar_prefetch=1,
        grid=(bsz // SEQS, NG),
        in_specs=in_specs,
        out_specs=[pl.BlockSpec((SEQS, L, qw), seq_map), pl.BlockSpec((SEQS, L, LANES), seq_map)],
        scratch_shapes=[pltpu.VMEM((2, SEQS * PP, kvw, PAGE_SIZE), F32),
                        pltpu.VMEM((2, SEQS * PP, kvw, PAGE_SIZE), F32),
                        pltpu.VMEM((2, SEQS * PP, H_D, PAGE_SIZE), F32),
                        pltpu.SemaphoreType.DMA((3, 2)),
                        pltpu.VMEM((SEQS, H_D * L, kvw), BF16),
                        pltpu.VMEM((SEQS, H_D * L, 1), F32),
                        pltpu.VMEM((SEQS, H_D * L, 1), F32),
                        pltpu.VMEM((SEQS, H_D * L, kvw), F32),
                        pltpu.VMEM((SEQS, H_D, 1), F32)])
    return pl.pallas_call(
        kern,
        out_shape=[jax.ShapeDtypeStruct((bsz, L, qw), F32), jax.ShapeDtypeStruct((bsz, L, LANES), F32)],
        grid_spec=grid_spec,
        compiler_params=_params("arbitrary", "arbitrary"),
        name="fox_paged",
    )(page_table, q, k_new, v_new, fd_new, fbias_pad, pkt, pvt, plft)


def _pad_cols(w, n):
    return jnp.pad(w, ((0, 0), (0, n - w.shape[1])))


def _prep_w_ab(w):
    z, xbc, dt, q, k, v = jnp.split(w, np.cumsum((D_INNER_A, CONV_DIM_A, H_A, H_B * HD_B, KV_B * HD_B, KV_B * HD_B))[:-1].tolist(), axis=1)
    return jnp.concatenate([z, xbc, q, k, v, _pad_cols(dt, LANES)], axis=1).astype(BF16)


AB_SIZES = (D_INNER_A, CONV_DIM_A, H_B * HD_B, KV_B * HD_B, KV_B * HD_B, LANES)


def _prep_w_cd(w):
    sizes = (H_C * DK_C, H_C * DK_C, H_C * DV_C, H_C * DV_C, H_D * HD_D, KV_D * HD_D, KV_D * HD_D, H_D)
    parts = jnp.split(w, np.cumsum(sizes)[:-1].tolist(), axis=1)
    parts[-1] = _pad_cols(parts[-1], LANES)
    return jnp.concatenate(parts, axis=1).astype(BF16)


CD_SIZES = (H_C * DK_C, H_C * DK_C, H_C * DV_C, H_C * DV_C, H_D * HD_D, KV_D * HD_D, KV_D * HD_D, LANES)


def _trunk(x, c, st, prm, page_table):
    sample = page_table is not None
    bsz, seq, _ = x.shape
    if sample:
        bb, tl, ffn_tl = min(SAMPLE_BATCH_TILE, bsz), seq, seq
    else:
        bb, tl, ffn_tl = 1, min(ROW_TILE, seq), min(FFN_ROW_TILE, seq)
    n_sub = DEPTH * 2
    m_all = _ada_all(c, prm['ada_w'].reshape(n_sub, D_MODEL, 3 * D_MODEL), prm['ada_b'].reshape(n_sub, 1, 3 * D_MODEL))
    new = {}
    for l in range(DEPTH):
        j = l // 2
        m_mix = m_all[2 * l][:, None, :]
        m_ffn = m_all[2 * l + 1][:, None, :]
        if l % 2 == 0:
            z, xbc, qb, kb, vb, dtp = _mod_matmul(x, m_mix, _prep_w_ab(prm['w_in_ab'][j]), AB_SIZES, bb, tl)
            if sample:
                conv0p = jnp.pad(st['conv_a'][j], ((0, 0), (SUBLANES - (CONV_A - 1), 0), (0, 0)))
                ssm0 = st['ssm_a'][j]
            else:
                conv0p, ssm0 = None, None
            ya, conv1p, ssm1 = _mamba(z, xbc, dtp, conv0p, ssm0, prm['conv_w_a'][j], prm['conv_b_a'][j],
                                      prm['dt_bias_a'][j], prm['a_log_a'][j], prm['d_skip_a'][j],
                                      prm['norm_w_a'][j], T=min(SSD_CHUNK, seq))
            if sample:
                kbuf = st['swa_k'][j].reshape(bsz, WINDOW, KV_B * HD_B)
                vbuf = st['swa_v'][j].reshape(bsz, WINDOW, KV_B * HD_B)
                yb = _swa(qb, kbuf, kb, vbuf, vb, prm['sinks_b'][j], TQ=seq, SB=min(SWA_SAMPLE_SEQS, bsz),
                          prev_is_same_array=False)
                bk = jnp.concatenate([kbuf[:, seq:], kb], axis=1)
                bv = jnp.concatenate([vbuf[:, seq:], vb], axis=1)
            else:
                yb = _swa(qb, kb, kb, vb, vb, prm['sinks_b'][j], TQ=WINDOW, SB=1, prev_is_same_array=True)
                bk, bv = kb[:, -WINDOW:], vb[:, -WINDOW:]
            w_out = prm['w_out_ab'][j].astype(BF16)
            x = _mm_res_ln([ya, yb], [w_out[:D_INNER_A], w_out[D_INNER_A:]], x, m_mix,
                           prm['ln_g'][l, 0][None, :], prm['ln_b'][l, 0][None, :], bb, tl)
            new.setdefault('ssm_a', []).append(ssm1)
            new.setdefault('conv_a', []).append(conv1p[:, SUBLANES - (CONV_A - 1):, :])
            new.setdefault('swa_k', []).append(bk.reshape(bsz, WINDOW, KV_B, HD_B))
            new.setdefault('swa_v', []).append(bv.reshape(bsz, WINDOW, KV_B, HD_B))
        else:
            qc, fc, ic, gc, qd, kd, vd, fdp = _mod_matmul(x, m_mix, _prep_w_cd(prm['w_in_cd'][j]), CD_SIZES, bb, tl)
            s0 = st['hgrn_c'][j] if sample else None
            yc, s1 = _gla(qc, fc, ic, gc, prm['lb_c'], prm['gnorm_c'][j], s0, layer=l, T=min(GLA_CHUNK, seq))
            fbias_pad = jnp.zeros((1, LANES), F32).at[0, :H_D].set(prm['fbias_d'][j])
            if sample:
                yd, lfp = _fox_paged(qd, kd, vd, fdp, fbias_pad, st['fox_k'], st['fox_v'], st['fox_logf'],
                                     page_table, j, FOX_PAGES_PER_STEP)
            else:
                lfp, qm, k2, v2 = _fox_prep(qd, kd, vd, fdp, fbias_pad, min(FOX_PREP_T, seq))
                yd = _fox_flash(qm, k2, v2, min(FOX_TQ, seq), min(FOX_TK, seq))
            w_out = prm['w_out_cd'][j].astype(BF16)
            x = _mm_res_ln([yc, yd], [w_out[:H_C * DV_C], w_out[H_C * DV_C:]], x, m_mix,
                           prm['ln_g'][l, 0][None, :], prm['ln_b'][l, 0][None, :], bb, tl)
            new.setdefault('hgrn_c', []).append(s1)
            new.setdefault('fox_k', []).append(kd.reshape(bsz, seq, KV_D, HD_D))
            new.setdefault('fox_v', []).append(vd.reshape(bsz, seq, KV_D, HD_D))
            new.setdefault('fox_logf', []).append(lfp[:, :, :H_D])
        buf0 = st['ffn_conv'][l] if sample else jnp.zeros((bsz, FFN_CONV - 1, D_FF), F32)
        a, buf1 = _ffn_in(x, m_ffn, prm['ffn_w_in'][l].astype(BF16), buf0, prm['ffn_conv_w'][l],
                          prm['ffn_conv_b'][l][None, :], bb, ffn_tl)
        x = _mm_res_ln([a], [prm['ffn_w_out'][l].astype(BF16)], x, m_ffn,
                       prm['ln_g'][l, 1][None, :], prm['ln_b'][l, 1][None, :], bb, ffn_tl)
        new.setdefault('ffn_conv', []).append(buf1)
    return x, {name: jnp.stack(rows, axis=0) for name, rows in new.items()}


def kernel(x_prompt, x_sample, state_ssm_a, state_conv_a, cache_swa_k, cache_swa_v, state_hgrn_c, cache_fox_k, cache_fox_v, cache_fox_logf, state_ffn_conv, page_table, c_prompt, c_sample, ada_w, ada_b, ln_g, ln_b, w_in_ab, w_out_ab, conv_w_a, conv_b_a, dt_bias_a, a_log_a, d_skip_a, norm_w_a, sinks_b, w_in_cd, w_out_cd, lb_c, gnorm_c, fbias_d, ffn_w_in, ffn_conv_w, ffn_conv_b, ffn_w_out):
    prm = dict(ada_w=ada_w, ada_b=ada_b, ln_g=ln_g, ln_b=ln_b, w_in_ab=w_in_ab, w_out_ab=w_out_ab,
               conv_w_a=conv_w_a, conv_b_a=conv_b_a, dt_bias_a=dt_bias_a, a_log_a=a_log_a,
               d_skip_a=d_skip_a, norm_w_a=norm_w_a, sinks_b=sinks_b, w_in_cd=w_in_cd, w_out_cd=w_out_cd,
               lb_c=lb_c, gnorm_c=gnorm_c, fbias_d=fbias_d, ffn_w_in=ffn_w_in, ffn_conv_w=ffn_conv_w,
               ffn_conv_b=ffn_conv_b, ffn_w_out=ffn_w_out)
    st = dict(ssm_a=state_ssm_a, conv_a=state_conv_a, swa_k=cache_swa_k, swa_v=cache_swa_v,
              hgrn_c=state_hgrn_c, fox_k=cache_fox_k, fox_v=cache_fox_v, fox_logf=cache_fox_logf,
              ffn_conv=state_ffn_conv)
    y_p, sp = _trunk(x_prompt, c_prompt, None, prm, None)
    y_s, ss = _trunk(x_sample, c_sample, st, prm, page_table)
    return (y_p, y_s,
            sp['ssm_a'], ss['ssm_a'], sp['conv_a'], ss['conv_a'],
            sp['swa_k'], ss['swa_k'], sp['swa_v'], ss['swa_v'],
            sp['hgrn_c'], ss['hgrn_c'],
            sp['fox_k'], ss['fox_k'], sp['fox_v'], ss['fox_v'], sp['fox_logf'], ss['fox_logf'],
            sp['ffn_conv'], ss['ffn_conv'])
```

```python
import functools
import math

import numpy as np
import jax
import jax.numpy as jnp
from jax import lax
from jax.experimental import pallas as pl
from jax.experimental.pallas import tpu as pltpu

F32 = jnp.float32
BF16 = jnp.bfloat16
NEG_INF = float("-inf")

D_MODEL = 1024
DEPTH = 2
PAGE_SIZE = 128
H_A, P_A, N_A, G_A, CONV_A = 16, 64, 128, 2, 4
D_INNER_A = H_A * P_A
CONV_DIM_A = D_INNER_A + 2 * G_A * N_A
H_B, KV_B, HD_B, WINDOW = 8, 2, 64, 128
H_C, DK_C, DV_C = 4, 128, 128
H_D, KV_D, HD_D = 16, 4, 64
D_FF, FFN_CONV = 2816, 3
ALPHA = (2 * DEPTH) ** 0.25
SCALE_B = HD_B ** -0.5
SCALE_D = HD_D ** -0.5
LN_EPS = 1e-5
RMS_EPS = 1e-6
LOG2E = math.log2(math.e)

LANES = 128
SUBLANES = 8
VMEM_LIMIT_BYTES = 56 * 1024 * 1024

SSD_CHUNK = 128
GLA_CHUNK = 256
GLA_BLOCK = 16
FOX_TQ = 1024
FOX_TK = 512
FOX_HEAD_GROUPS = 2
FOX_PREP_T = 512
FOX_PAGES_PER_STEP = 16
ROW_TILE = 512
OUT_ROW_TILE = 1024
FFN_ROW_TILE = 512
SAMPLE_BATCH_TILE = 32
SWA_SAMPLE_SEQS = 8
FOX_SEQS_PER_STEP = 2


def _params(*sem):
    return pltpu.CompilerParams(dimension_semantics=sem, vmem_limit_bytes=VMEM_LIMIT_BYTES)


def _resident(shape):
    nd = len(shape)
    return pl.BlockSpec(shape, lambda *_: (0,) * nd, pipeline_mode=pl.Buffered(1))


def _silu(x):
    return x * jax.nn.sigmoid(x)


def _softplus(x):
    return jnp.maximum(x, 0.0) + jnp.log1p(jnp.exp(-jnp.abs(x)))


def _dot(a, b):
    return jnp.dot(a, b, preferred_element_type=F32)


def _dot_nt(a, b):
    return lax.dot_general(a, b, (((1,), (1,)), ((), ())), preferred_element_type=F32)


def _dot_tn(a, b):
    return lax.dot_general(a, b, (((0,), (0,)), ((), ())), preferred_element_type=F32)


def _split3(x):
    hi = x.astype(BF16)
    r = x - hi.astype(F32)
    mid = r.astype(BF16)
    lo = (r - mid.astype(F32)).astype(BF16)
    return hi, mid, lo


def _exact_dot_lhs(x, w01):
    hi, mid, lo = _split3(x)
    return _dot(hi, w01) + _dot(mid, w01) + _dot(lo, w01)


def _exact_dot_rhs(w01, x):
    hi, mid, lo = _split3(x)
    return _dot(w01, hi) + _dot(w01, mid) + _dot(w01, lo)


def _lower_tri_bf16(t):
    r = lax.broadcasted_iota(jnp.int32, (t, t), 0)
    c = lax.broadcasted_iota(jnp.int32, (t, t), 1)
    return jnp.where(r >= c, 1.0, 0.0).astype(BF16)


def _layer_norm_rows(r, g, b):
    mu = jnp.mean(r, axis=-1, keepdims=True)
    d = r - mu
    var = jnp.mean(d * d, axis=-1, keepdims=True)
    return d * lax.rsqrt(var + LN_EPS) * g + b


def _ada_kernel(c_ref, w_ref, b_ref, o_ref):
    s = _silu(c_ref[...]).astype(BF16)
    o_ref[0] = _dot(s, w_ref[0].astype(BF16)) + b_ref[0]


def _ada_all(c, ada_w, ada_b):
    bsz = c.shape[0]
    n_sub = ada_w.shape[0]
    n_out = ada_w.shape[2]
    tn = D_MODEL
    return pl.pallas_call(
        _ada_kernel,
        out_shape=jax.ShapeDtypeStruct((n_sub, bsz, n_out), F32),
        grid=(n_sub, n_out // tn),
        in_specs=[pl.BlockSpec((bsz, D_MODEL), lambda s, j: (0, 0)),
                  pl.BlockSpec((1, D_MODEL, tn), lambda s, j: (s, 0, j)),
                  pl.BlockSpec((1, 1, tn), lambda s, j: (s, 0, j))],
        out_specs=pl.BlockSpec((1, bsz, tn), lambda s, j: (s, 0, j)),
        compiler_params=_params("parallel", "parallel"),
        name="ada_mod",
    )(c, ada_w, ada_b)


def _mod_matmul_kernel(x_ref, m_ref, w_ref, *o_refs, bb, tl, sizes):
    d = x_ref.shape[-1]
    m = m_ref[...]
    h = x_ref[...] * (1.0 + m[:, :, d:2 * d]) + m[:, :, 0:d]
    h2 = h.reshape(bb * tl, d).astype(BF16)
    off = 0
    for o_ref, n in zip(o_refs, sizes):
        o_ref[...] = _dot(h2, w_ref[:, off:off + n]).reshape(bb, tl, n)
        off += n


def _mod_matmul(x, m, w_bf16, sizes, bb, tl):
    bsz, seq, d = x.shape
    n_tot = w_bf16.shape[1]
    kern = functools.partial(_mod_matmul_kernel, bb=bb, tl=tl, sizes=tuple(sizes))
    return pl.pallas_call(
        kern,
        out_shape=[jax.ShapeDtypeStruct((bsz, seq, n), F32) for n in sizes],
        grid=(bsz // bb, seq // tl),
        in_specs=[pl.BlockSpec((bb, tl, d), lambda i, j: (i, j, 0)),
                  pl.BlockSpec((bb, 1, 3 * d), lambda i, j: (i, 0, 0)),
                  _resident((d, n_tot))],
        out_specs=[pl.BlockSpec((bb, tl, n), lambda i, j: (i, j, 0)) for n in sizes],
        compiler_params=_params("parallel", "parallel"),
        name="mod_matmul",
    )(x, m, w_bf16)


def _mm_res_ln_kernel(*refs, bb, tl, n_in):
    y_refs = refs[0:n_in]
    w_refs = refs[n_in:2 * n_in]
    x_ref, m_ref, g_ref, b_ref, o_ref = refs[2 * n_in:]
    d = x_ref.shape[-1]
    mix = None
    for y_ref, w_ref in zip(y_refs, w_refs):
        y2 = y_ref[...].reshape(bb * tl, y_ref.shape[-1]).astype(BF16)
        part = _dot(y2, w_ref[...])
        mix = part if mix is None else mix + part
    gate = m_ref[:, :, 2 * d:3 * d]
    r = ALPHA * x_ref[...] + (1.0 + gate) * mix.reshape(bb, tl, d)
    o_ref[...] = _layer_norm_rows(r, g_ref[...], b_ref[...])


def _mm_res_ln(ys, ws_bf16, x, m, g, b, bb, tl):
    bsz, seq, d = x.shape
    kern = functools.partial(_mm_res_ln_kernel, bb=bb, tl=tl, n_in=len(ys))
    return pl.pallas_call(
        kern,
        out_shape=jax.ShapeDtypeStruct((bsz, seq, d), F32),
        grid=(bsz // bb, seq // tl),
        in_specs=([pl.BlockSpec((bb, tl, y.shape[-1]), lambda i, j: (i, j, 0)) for y in ys]
                  + [_resident(w.shape) for w in ws_bf16]
                  + [pl.BlockSpec((bb, tl, d), lambda i, j: (i, j, 0)),
                     pl.BlockSpec((bb, 1, 3 * d), lambda i, j: (i, 0, 0)),
                     pl.BlockSpec((1, d), lambda i, j: (0, 0)),
                     pl.BlockSpec((1, d), lambda i, j: (0, 0))]),
        out_specs=pl.BlockSpec((bb, tl, d), lambda i, j: (i, j, 0)),
        compiler_params=_params("parallel", "parallel"),
        name="mm_res_ln",
    )(*ys, *ws_bf16, x, m, g, b)


def _ffn_in_kernel(x_ref, m_ref, w_ref, buf0_ref, cw_ref, cb_ref, a_ref, buf1_ref, carry_scr, *, bb, tl, n_col):
    d = x_ref.shape[-1]
    dff = a_ref.shape[-1]
    j = pl.program_id(1)

    @pl.when(j == 0)
    def _():
        carry_scr[...] = buf0_ref[...]

    m = m_ref[...]
    h = x_ref[...] * (1.0 + m[:, :, d:2 * d]) + m[:, :, 0:d]
    h2 = h.reshape(bb * tl, d).astype(BF16)
    rows = bb * tl
    cw = dff // n_col
    t = lax.broadcasted_iota(jnp.int32, (rows, 1), 0) % tl
    for ci in range(n_col):
        lo = ci * cw
        u = _dot(h2, w_ref[:, lo:lo + cw])
        g = _dot(h2, w_ref[:, dff + lo:dff + lo + cw])
        carry = carry_scr[:, :, lo:lo + cw]
        prev1 = jnp.broadcast_to(carry[:, 1:2, :], (bb, tl, cw)).reshape(rows, cw)
        prev0 = jnp.broadcast_to(carry[:, 0:1, :], (bb, tl, cw)).reshape(rows, cw)
        tap1 = jnp.where(t == 0, prev1, pltpu.roll(g, 1, axis=0))
        tap2 = jnp.where(t == 0, prev0, jnp.where(t == 1, prev1, pltpu.roll(g, 2, axis=0)))
        y = cb_ref[:, lo:lo + cw] + g * cw_ref[2:3, lo:lo + cw]
        y = y + tap2 * cw_ref[0:1, lo:lo + cw]
        y = y + tap1 * cw_ref[1:2, lo:lo + cw]
        a_ref[:, :, lo:lo + cw] = (_silu(y) * u).reshape(bb, tl, cw)
        last2 = g.reshape(bb, tl, cw)[:, tl - (FFN_CONV - 1):, :]
        carry_scr[:, :, lo:lo + cw] = last2
        buf1_ref[:, :, lo:lo + cw] = last2


def _ffn_in(x, m, w_bf16, buf0, conv_w, conv_b, bb, tl):
    bsz, seq, d = x.shape
    kern = functools.partial(_ffn_in_kernel, bb=bb, tl=tl, n_col=2)
    return pl.pallas_call(
        kern,
        out_shape=[jax.ShapeDtypeStruct((bsz, seq, D_FF), F32),
                   jax.ShapeDtypeStruct((bsz, FFN_CONV - 1, D_FF), F32)],
        grid=(bsz // bb, seq // tl),
        in_specs=[pl.BlockSpec((bb, tl, d), lambda i, j: (i, j, 0)),
                  pl.BlockSpec((bb, 1, 3 * d), lambda i, j: (i, 0, 0)),
                  _resident((d, 2 * D_FF)),
                  pl.BlockSpec((bb, FFN_CONV - 1, D_FF), lambda i, j: (i, 0, 0)),
                  pl.BlockSpec((FFN_CONV, D_FF), lambda i, j: (0, 0)),
                  pl.BlockSpec((1, D_FF), lambda i, j: (0, 0))],
        out_specs=[pl.BlockSpec((bb, tl, D_FF), lambda i, j: (i, j, 0)),
                   pl.BlockSpec((bb, FFN_CONV - 1, D_FF), lambda i, j: (i, 0, 0))],
        scratch_shapes=[pltpu.VMEM((bb, FFN_CONV - 1, D_FF), F32)],
        compiler_params=_params("parallel", "arbitrary"),
        name="ffn_in_conv",
    )(x, m, w_bf16, buf0, conv_w, conv_b)


def _mamba_kernel(*refs, T, has_state, n_chunks):
    if has_state:
        (z_ref, xbc_ref, dt_ref, conv0_ref, ssm0_ref, cw_ref, cb_ref, dtb_ref, alog_ref, dsk_ref, nw_ref, e_ref,
         y_ref, conv1_ref, ssm1_ref, carry_scr, s_scr) = refs
    else:
        (z_ref, xbc_ref, dt_ref, cw_ref, cb_ref, dtb_ref, alog_ref, dsk_ref, nw_ref, e_ref,
         y_ref, conv1_ref, ssm1_ref, carry_scr, s_scr) = refs
    c = pl.program_id(1)
    n_pair = H_A // 2
    cdim = CONV_DIM_A

    @pl.when(c == 0)
    def _():
        if has_state:
            carry_scr[...] = conv0_ref[0]
            for j in range(n_pair):
                s_scr[j] = jnp.concatenate([ssm0_ref[0, 2 * j], ssm0_ref[0, 2 * j + 1]], axis=1)
        else:
            carry_scr[...] = jnp.zeros_like(carry_scr)
            s_scr[...] = jnp.zeros_like(s_scr)

    xb = xbc_ref[0]
    carry = carry_scr[...]
    row8 = lax.broadcasted_iota(jnp.int32, (SUBLANES, cdim), 0)
    acc = cb_ref[...] + xb * cw_ref[CONV_A - 1:CONV_A, :]
    for dshift in (3, 2, 1):
        rolled = pltpu.roll(xb, dshift, axis=0)
        top = jnp.where(row8 < dshift, pltpu.roll(carry, dshift, axis=0), rolled[0:SUBLANES])
        sh = top if T == SUBLANES else jnp.concatenate([top, rolled[SUBLANES:]], axis=0)
        acc = acc + sh * cw_ref[CONV_A - 1 - dshift:CONV_A - dshift, :]
    last8 = xb[T - SUBLANES:T, :]
    carry_scr[...] = last8
    conv1_ref[0] = last8
    xc = _silu(acc)
    xs = xc[:, 0:D_INNER_A]
    bm = xc[:, D_INNER_A:D_INNER_A + G_A * N_A]
    cm = xc[:, D_INNER_A + G_A * N_A:cdim]

    dt = _softplus(dt_ref[0] + dtb_ref[...])
    la = dt * (-jnp.exp(alog_ref[...]))
    tri = _lower_tri_bf16(T)
    cum = _exact_dot_rhs(tri, la)
    e = e_ref[...]
    dtx = _exact_dot_lhs(dt, e)
    cumx = _exact_dot_lhs(cum, e)
    x = xs * dtx
    decx = jnp.exp(cumx)
    lastx = cumx[T - 1:T, :]
    xd = (x * jnp.exp(lastx - cumx)).astype(BF16)
    elastx = jnp.exp(lastx)
    cum_t = cum.T

    rr = lax.broadcasted_iota(jnp.int32, (T, T), 0)
    cc = lax.broadcasted_iota(jnp.int32, (T, T), 1)
    causal = rr >= cc
    lane = lax.broadcasted_iota(jnp.int32, (1, 2 * P_A), 1)
    low = lane < P_A

    cb_g, c_g, b_g = [], [], []
    for g in range(G_A):
        cg = cm[:, g * N_A:(g + 1) * N_A].astype(BF16)
        bg = bm[:, g * N_A:(g + 1) * N_A].astype(BF16)
        c_g.append(cg)
        b_g.append(bg)
        cb_g.append(_dot_nt(cg, bg))

    heads_per_group = H_A // G_A
    outs = []
    for j in range(n_pair):
        g = (2 * j) // heads_per_group
        sl = slice(j * 2 * P_A, (j + 1) * 2 * P_A)
        xj = x[:, sl]
        yj = None
        for eidx in (0, 1):
            h = 2 * j + eidx
            col = cum[:, h:h + 1]
            row = cum_t[h:h + 1, :]
            decay = jnp.exp(jnp.where(causal, col - row, NEG_INF))
            scores = (cb_g[g] * decay).astype(BF16)
            xm = jnp.where(low if eidx == 0 else jnp.logical_not(low), xj, 0.0).astype(BF16)
            part = _dot(scores, xm)
            yj = part if yj is None else yj + part
        s_old = s_scr[j]
        yj = yj + _dot(c_g[g], s_old.astype(BF16)) * decx[:, sl]
        s_scr[j] = elastx[:, sl] * s_old + _dot_tn(b_g[g], xd[:, sl])
        outs.append(yj)
    y = jnp.concatenate(outs, axis=1) + dsk_ref[...] * xs
    y = y * _silu(z_ref[0])
    y_ref[0] = y * lax.rsqrt(jnp.mean(y * y, axis=-1, keepdims=True) + RMS_EPS) * nw_ref[...]

    @pl.when(c == n_chunks - 1)
    def _():
        for j in range(n_pair):
            sj = s_scr[j]
            ssm1_ref[0, 2 * j] = sj[:, 0:P_A]
            ssm1_ref[0, 2 * j + 1] = sj[:, P_A:2 * P_A]


def _mamba(z, xbc, dtp, conv0p, ssm0, conv_w, conv_b, dt_bias, a_log, d_skip, norm_w, T):
    bsz, seq, _ = z.shape
    n_chunks = seq // T
    has_state = conv0p is not None
    dtb = jnp.zeros((1, LANES), F32).at[0, :H_A].set(dt_bias)
    alog = jnp.zeros((1, LANES), F32).at[0, :H_A].set(a_log)
    dsk = jnp.repeat(d_skip, P_A)[None, :]
    head_of_col = np.arange(D_INNER_A) // P_A
    expand = jnp.asarray((np.arange(LANES)[:, None] == head_of_col[None, :]).astype(np.float32), dtype=BF16)
    kern = functools.partial(_mamba_kernel, T=T, has_state=has_state, n_chunks=n_chunks)
    tok = lambda w: pl.BlockSpec((1, T, w), lambda b, c: (b, c, 0))
    const = lambda shape: pl.BlockSpec(shape, lambda b, c: (0,) * len(shape))
    in_specs = [tok(D_INNER_A), tok(CONV_DIM_A), tok(LANES)]
    args = [z, xbc, dtp]
    if has_state:
        in_specs += [pl.BlockSpec((1, SUBLANES, CONV_DIM_A), lambda b, c: (b, 0, 0)),
                     pl.BlockSpec((1, H_A, N_A, P_A), lambda b, c: (b, 0, 0, 0))]
        args += [conv0p, ssm0]
    in_specs += [const((CONV_A, CONV_DIM_A)), const((1, CONV_DIM_A)), const((1, LANES)), const((1, LANES)),
                 const((1, D_INNER_A)), const((1, D_INNER_A)), const((LANES, D_INNER_A))]
    args += [conv_w, conv_b[None, :], dtb, alog, dsk, norm_w[None, :], expand]
    return pl.pallas_call(
        kern,
        out_shape=[jax.ShapeDtypeStruct((bsz, seq, D_INNER_A), F32),
                   jax.ShapeDtypeStruct((bsz, SUBLANES, CONV_DIM_A), F32),
                   jax.ShapeDtypeStruct((bsz, H_A, N_A, P_A), F32)],
        grid=(bsz, n_chunks),
        in_specs=in_specs,
        out_specs=[tok(D_INNER_A),
                   pl.BlockSpec((1, SUBLANES, CONV_DIM_A), lambda b, c: (b, 0, 0)),
                   pl.BlockSpec((1, H_A, N_A, P_A), lambda b, c: (b, 0, 0, 0))],
        scratch_shapes=[pltpu.VMEM((SUBLANES, CONV_DIM_A), F32),
                        pltpu.VMEM((H_A // 2, N_A, 2 * P_A), F32)],
        compiler_params=_params("parallel", "arbitrary"),
        name="mamba_ssd",
    )(*args)


def _swa_kernel(sinks_ref, q_ref, kp_ref, kc_ref, vp_ref, vc_ref, o_ref, *, SB, TQ, first_has_prev):
    n = pl.program_id(1)
    lane = lax.broadcasted_iota(jnp.int32, (1, 2 * HD_B), 1)
    low = lane < HD_B
    roll64 = lambda a: pltpu.roll(a, HD_B, axis=1)
    has_prev = jnp.logical_or(n > 0, first_has_prev)
    group = H_B // KV_B
    classes = [[h for h in range(H_B) if ((h // group) == (h % 2)) == flag] for flag in (True, False)]
    R = len(classes[0]) * TQ
    ip = lax.broadcasted_iota(jnp.int32, (R, WINDOW), 0) % TQ
    jp = lax.broadcasted_iota(jnp.int32, (R, WINDOW), 1)
    valid_prev = jnp.logical_and(jp >= ip, has_prev)
    ic = lax.broadcasted_iota(jnp.int32, (R, TQ), 0) % TQ
    jc = lax.broadcasted_iota(jnp.int32, (R, TQ), 1)
    valid_cur = jc <= ic
    for sb in range(SB):
        q = q_ref[sb] * SCALE_B
        kv_arrays = (kp_ref[sb], kc_ref[sb], vp_ref[sb], vc_ref[sb])
        res = {}
        for cls, heads in enumerate(classes):
            kpv, kcv, vpv, vcv = [(a if cls == 0 else roll64(a)).astype(BF16) for a in kv_arrays]
            qs = jnp.concatenate(
                [jnp.where(low if h % 2 == 0 else jnp.logical_not(low),
                           q[:, (h // 2) * 2 * HD_B:(h // 2 + 1) * 2 * HD_B], 0.0) for h in heads],
                axis=0).astype(BF16)
            sink = jnp.concatenate([jnp.full((TQ, 1), sinks_ref[h], F32) for h in heads], axis=0)
            sp = jnp.where(valid_prev, _dot_nt(qs, kpv), NEG_INF)
            sc = jnp.where(valid_cur, _dot_nt(qs, kcv), NEG_INF)
            mx = jnp.maximum(jnp.maximum(jnp.max(sp, axis=-1, keepdims=True),
                                         jnp.max(sc, axis=-1, keepdims=True)), sink)
            pp = jnp.exp(sp - mx)
            pc = jnp.exp(sc - mx)
            den = (jnp.sum(pp, axis=-1, keepdims=True) + jnp.sum(pc, axis=-1, keepdims=True)
                   + jnp.exp(sink - mx))
            o = (_dot(pp.astype(BF16), vpv) + _dot(pc.astype(BF16), vcv)) / den
            for idx, h in enumerate(heads):
                res[h] = o[idx * TQ:(idx + 1) * TQ]
        o_ref[sb] = jnp.concatenate([jnp.where(low, res[2 * j], res[2 * j + 1]) for j in range(H_B // 2)], axis=1)


def _swa(q, k_prev, k_cur, v_prev, v_cur, sinks, TQ, SB, prev_is_same_array):
    bsz, seq, _ = q.shape
    nb = seq // TQ
    kvw = KV_B * HD_B
    if prev_is_same_array:
        prev_map = lambda b, n: (b, jnp.maximum(n - 1, 0), 0)
    else:
        prev_map = lambda b, n: (b, 0, 0)
    kern = functools.partial(_swa_kernel, SB=SB, TQ=TQ, first_has_prev=not prev_is_same_array)
    return pl.pallas_call(
        kern,
        out_shape=jax.ShapeDtypeStruct((bsz, seq, H_B * HD_B), F32),
        grid=(bsz // SB, nb),
        in_specs=[pl.BlockSpec(memory_space=pltpu.SMEM),
                  pl.BlockSpec((SB, TQ, H_B * HD_B), lambda b, n: (b, n, 0)),
                  pl.BlockSpec((SB, WINDOW, kvw), prev_map),
                  pl.BlockSpec((SB, TQ, kvw), lambda b, n: (b, n, 0)),
                  pl.BlockSpec((SB, WINDOW, kvw), prev_map),
                  pl.BlockSpec((SB, TQ, kvw), lambda b, n: (b, n, 0))],
        out_specs=pl.BlockSpec((SB, TQ, H_B * HD_B), lambda b, n: (b, n, 0)),
        compiler_params=_params("parallel", "parallel"),
        name="swa_attn",
    )(sinks, q, k_prev, k_cur, v_prev, v_cur)


def _gla_kernel(*refs, T, blk, layer, has_state, n_chunks):
    if has_state:
        (qr_ref, fr_ref, ir_ref, gr_ref, lbc_ref, gw_ref, s0_ref, o_ref, s1_ref,
         st_scr, qd_scr, kd_scr, el_scr, v_scr, od_scr, oi_scr) = refs
    else:
        (qr_ref, fr_ref, ir_ref, gr_ref, lbc_ref, gw_ref, o_ref, s1_ref,
         st_scr, qd_scr, kd_scr, el_scr, v_scr, od_scr, oi_scr) = refs
    c = pl.program_id(1)
    n_blk = T // blk

    @pl.when(c == 0)
    def _():
        if has_state:
            for h in range(H_C):
                st_scr[h] = s0_ref[0, h].T
        else:
            st_scr[...] = jnp.zeros_like(st_scr)

    lbc = lbc_ref[...]
    ex = jnp.exp(lbc - jnp.max(lbc, axis=0, keepdims=True))
    sm = ex / jnp.sum(ex, axis=0, keepdims=True)
    lb = jnp.sum(sm[1:layer + 1], axis=0, keepdims=True) if layer >= 1 else jnp.zeros_like(sm[0:1])
    rr = lax.broadcasted_iota(jnp.int32, (T, T), 0)
    cc = lax.broadcasted_iota(jnp.int32, (T, T), 1)
    same_blk = (rr // blk) == (cc // blk)
    tri_blk = jnp.where(jnp.logical_and(same_blk, rr >= cc), 1.0, 0.0).astype(BF16)
    ones_blk = jnp.where(same_blk, 1.0, 0.0).astype(BF16)
    t_in_blk = lax.broadcasted_iota(jnp.int32, (T, 1), 0) % blk

    for h in range(H_C):
        sl = slice(h * DK_C, (h + 1) * DK_C)
        q = _silu(qr_ref[0, :, sl])
        fg = lb[:, sl] + (1.0 - lb[:, sl]) * jax.nn.sigmoid(fr_ref[0, :, sl])
        lf = jnp.log(fg)
        key = 1.0 - fg
        v = ir_ref[0, :, sl]
        cr = _exact_dot_rhs(tri_blk, lf)
        last = _exact_dot_rhs(ones_blk, lf)
        od = jnp.sum(q * key, axis=-1, keepdims=True) * v
        for off in range(1, blk):
            dec = jnp.exp(jnp.where(t_in_blk >= off, cr - pltpu.roll(cr, off, axis=0), NEG_INF))
            p = jnp.sum(q * dec * pltpu.roll(key, off, axis=0), axis=-1, keepdims=True)
            od = od + p * pltpu.roll(v, off, axis=0)
        od_scr[:, sl] = od
        qd_scr[:, sl] = q * jnp.exp(cr)
        kd_scr[:, sl] = key * jnp.exp(last - cr)
        el_scr[:, sl] = jnp.exp(last)
        v_scr[:, sl] = v

    def blk_step(i):
        r0 = i * blk if isinstance(i, int) else pl.multiple_of(i * blk, blk)
        rows = pl.ds(r0, blk)
        for h in range(H_C):
            sl = slice(h * DK_C, (h + 1) * DK_C)
            st = st_scr[h]
            oi_scr[rows, sl] = _dot_nt(qd_scr[rows, sl].astype(BF16), st.astype(BF16))
            st_scr[h] = (el_scr[pl.ds(r0, 1), sl] * st
                         + _dot_tn(v_scr[rows, sl].astype(BF16), kd_scr[rows, sl].astype(BF16)))

    if n_blk == 1:
        blk_step(0)
    else:
        def body(i, carry):
            blk_step(i)
            return carry
        lax.fori_loop(0, n_blk, body, 0, unroll=2)

    for h in range(H_C):
        sl = slice(h * DK_C, (h + 1) * DK_C)
        o = od_scr[:, sl] + oi_scr[:, sl]
        o = o * lax.rsqrt(jnp.mean(o * o, axis=-1, keepdims=True) + RMS_EPS) * gw_ref[:, sl]
        o_ref[0, :, sl] = o * _silu(gr_ref[0, :, sl])

    @pl.when(c == n_chunks - 1)
    def _():
        for h in range(H_C):
            s1_ref[0, h] = st_scr[h].T


def _gla(qr, fr, ir, gr, lb_c, gnorm_w, s0, layer, T):
    bsz, seq, w = qr.shape
    n_chunks = seq // T
    has_state = s0 is not None
    blk = min(GLA_BLOCK, T)
    kern = functools.partial(_gla_kernel, T=T, blk=blk, layer=layer, has_state=has_state, n_chunks=n_chunks)
    tok = pl.BlockSpec((1, T, w), lambda b, c: (b, c, 0))
    st_spec = pl.BlockSpec((1, H_C, DK_C, DV_C), lambda b, c: (b, 0, 0, 0))
    in_specs = [tok, tok, tok, tok,
                pl.BlockSpec((DEPTH, w), lambda b, c: (0, 0)),
                pl.BlockSpec((1, w), lambda b, c: (0, 0))]
    args = [qr, fr, ir, gr, lb_c, gnorm_w[None, :]]
    if has_state:
        in_specs.append(st_spec)
        args.append(s0)
    return pl.pallas_call(
        kern,
        out_shape=[jax.ShapeDtypeStruct((bsz, seq, w), F32),
                   jax.ShapeDtypeStruct((bsz, H_C, DK_C, DV_C), F32)],
        grid=(bsz, n_chunks),
        in_specs=in_specs,
        out_specs=[tok, st_spec],
        scratch_shapes=[pltpu.VMEM((H_C, DV_C, DK_C), F32)] + [pltpu.VMEM((T, w), F32) for _ in range(6)],
        compiler_params=_params("parallel", "arbitrary"),
        name="gla_hgrn2",
    )(*args)


BIAS_TERMS = 3
BIAS_SELF_LANE = BIAS_TERMS * H_D


def _fox_prep_kernel(q_ref, k_ref, v_ref, fd_ref, fb_ref, lf_ref, qm_ref, k2_ref, v2_ref, carry_scr, *, T):
    c = pl.program_id(1)

    @pl.when(c == 0)
    def _():
        carry_scr[...] = jnp.zeros_like(carry_scr)

    lf = -_softplus(-(fd_ref[0] + fb_ref[...]))
    lf_ref[0] = lf
    cum = _exact_dot_rhs(_lower_tri_bf16(T), lf) + carry_scr[...]
    carry_scr[...] = cum[T - 1:T, :]
    hi, mid, lo = [t.astype(F32) for t in _split3(cum * LOG2E)]

    lane = lax.broadcasted_iota(jnp.int32, (1, 2 * HD_D), 1)
    low = lane < HD_D
    roll_half = lambda a: pltpu.roll(a, HD_D, axis=1)
    xk_low = jnp.where(lane < H_D, -hi,
                       jnp.where(lane < 2 * H_D, pltpu.roll(-mid, H_D, axis=1),
                                 jnp.where(lane < BIAS_SELF_LANE, pltpu.roll(-lo, 2 * H_D, axis=1),
                                           jnp.where(lane < BIAS_SELF_LANE + BIAS_TERMS, 1.0, 0.0))))
    xk_high = roll_half(xk_low)
    for i in range(KV_D // 2):
        blk = slice(i * 2 * HD_D, (i + 1) * 2 * HD_D)
        kb = k_ref[0, :, blk]
        vb = v_ref[0, :, blk]
        kr = roll_half(kb)
        vr = roll_half(vb)
        for par in (0, 1):
            kv = 2 * i + par
            k_low, k_high = (kb, kr) if par == 0 else (kr, kb)
            k2_ref[0, 2 * kv] = jnp.where(low, k_low, xk_high).astype(BF16)
            k2_ref[0, 2 * kv + 1] = jnp.where(low, xk_low, k_high).astype(BF16)
            v_low, v_high = (vb, vr) if par == 0 else (vr, vb)
            v2_ref[0, 2 * kv] = jnp.where(low, v_low, jnp.where(lane == HD_D, 1.0, 0.0)).astype(BF16)
            v2_ref[0, 2 * kv + 1] = jnp.where(low, jnp.where(lane == 0, 1.0, 0.0), v_high).astype(BF16)
    for h in range(H_D):
        qp = q_ref[0, :, (h // 2) * 2 * HD_D:(h // 2 + 1) * 2 * HD_D] * (SCALE_D * LOG2E)
        own =jnp.logical_or(jnp.logical_or(lane == h, lane == H_D + h), lane == 2 * H_D + h)
        xq = jnp.where(own, 1.0,
                       jnp.where(lane == BIAS_SELF_LANE, hi[:, h:h + 1],
                                 jnp.where(lane == BIAS_SELF_LANE + 1, mid[:, h:h + 1],
                                           jnp.where(lane == BIAS_SELF_LANE + 2, lo[:, h:h + 1], 0.0))))
        qm = jnp.where(low, qp, roll_half(xq)) if h % 2 == 0 else jnp.where(low, xq, qp)
        qm_ref[0, h] = qm.astype(BF16)


def _fox_prep(q, k, v, fd, fbias_pad, T):
    bsz, seq, _ = q.shape
    kern = functools.partial(_fox_prep_kernel, T=T)
    tok = lambda w: pl.BlockSpec((1, T, w), lambda b, c: (b, c, 0))
    heads = lambda n: pl.BlockSpec((1, n, T, 2 * HD_D), lambda b, c: (b, 0, c, 0))
    return pl.pallas_call(
        kern,
        out_shape=[jax.ShapeDtypeStruct((bsz, seq, LANES), F32),
                   jax.ShapeDtypeStruct((bsz, H_D, seq, 2 * HD_D), BF16),
                   jax.ShapeDtypeStruct((bsz, 2 * KV_D, seq, 2 * HD_D), BF16),
                   jax.ShapeDtypeStruct((bsz, 2 * KV_D, seq, 2 * HD_D), BF16)],
        grid=(bsz, seq // T),
        in_specs=[tok(H_D * HD_D), tok(KV_D * HD_D), tok(KV_D * HD_D), tok(LANES),
                  pl.BlockSpec((1, LANES), lambda b, c: (0, 0))],
        out_specs=[tok(LANES), heads(H_D), heads(2 * KV_D), heads(2 * KV_D)],
        scratch_shapes=[pltpu.VMEM((1, LANES), F32)],
        compiler_params=_params("parallel", "arbitrary"),
        name="fox_prep",
    )(q, k, v, fd, fbias_pad)


def _fox_flash_kernel(qi_ref, ki_ref, qm_ref, k2_ref, v2_ref, o_ref, m_scr, acc_scr, *, TQ, TK, HG):
    s_id = pl.program_id(2)
    qi = qi_ref[s_id]
    ki = ki_ref[s_id]
    group = H_D // KV_D
    n_heads = H_D // HG

    @pl.when(ki == 0)
    def _():
        m_scr[...] = jnp.full_like(m_scr, NEG_INF)
        acc_scr[...] = jnp.zeros_like(acc_scr)

    def head_step(h, masked):
        kv = h // group
        s = _dot_nt(qm_ref[0, h], k2_ref[0, 2 * kv + h % 2])
        if masked:
            r = lax.broadcasted_iota(jnp.int32, (TQ, TK), 0)
            c = lax.broadcasted_iota(jnp.int32, (TQ, TK), 1)
            s = jnp.where(c - r <= qi * TQ - ki * TK, s, NEG_INF)
        m_prev = m_scr[h]
        m_new = jnp.maximum(m_prev, jnp.max(s, axis=-1, keepdims=True))
        alpha = jnp.exp2(m_prev - m_new)
        p = jnp.exp2(s - jnp.concatenate([m_new] * (TK // LANES), axis=1))
        acc_scr[h] = alpha * acc_scr[h] + _dot(p.astype(BF16), v2_ref[0, 2 * kv + h % 2])
        m_scr[h] = m_new

    def all_heads(masked):
        for h in range(n_heads):
            head_step(h, masked)

    ratio = TQ // TK
    @pl.when(ki < qi * ratio)
    def _():
        all_heads(False)

    @pl.when(ki >= qi * ratio)
    def _():
        all_heads(True)

    @pl.when(ki == (qi + 1) * ratio - 1)
    def _():
        lane = lax.broadcasted_iota(jnp.int32, (1, 2 * HD_D), 1)
        low = lane < HD_D
        for j in range(n_heads // 2):
            a0 = acc_scr[2 * j]
            a1 = acc_scr[2 * j + 1]
            o0 = a0 / a0[:, HD_D:HD_D + 1]
            o1 = a1 / a1[:, 0:1]
            o_ref[0, :, j * 2 * HD_D:(j + 1) * 2 * HD_D] = jnp.where(low, o0, o1)


def _fox_flash(qm, k2, v2, TQ, TK):
    bsz, _, seq, _ = qm.shape
    assert TQ % TK == 0
    nq = seq // TQ
    pairs = [(a, b) for a in range(nq) for b in range((a + 1) * (TQ // TK))]
    qi_tab = jnp.asarray([p[0] for p in pairs], jnp.int32)
    ki_tab = jnp.asarray([p[1] for p in pairs], jnp.int32)
    HG = FOX_HEAD_GROUPS
    nh, nkv = H_D // HG, KV_D // HG
    kern = functools.partial(_fox_flash_kernel, TQ=TQ, TK=TK, HG=HG)
    qw = H_D * HD_D
    grid_spec = pltpu.PrefetchScalarGridSpec(
        num_scalar_prefetch=2,
        grid=(bsz, HG, len(pairs)),
        in_specs=[pl.BlockSpec((1, nh, TQ, 2 * HD_D), lambda b, g, s, qt, kt: (b, g, qt[s], 0)),
                  pl.BlockSpec((1, 2 * nkv, TK, 2 * HD_D), lambda b, g, s, qt, kt: (b, g, kt[s], 0)),
                  pl.BlockSpec((1, 2 * nkv, TK, 2 * HD_D), lambda b, g, s, qt, kt: (b, g, kt[s], 0))],
        out_specs=pl.BlockSpec((1, TQ, qw // HG), lambda b, g, s, qt, kt: (b, qt[s], g)),
        scratch_shapes=[pltpu.VMEM((nh, TQ, LANES), F32),
                        pltpu.VMEM((nh, TQ, 2 * HD_D), F32)])
    return pl.pallas_call(
        kern,
        out_shape=jax.ShapeDtypeStruct((bsz, seq, qw), F32),
        grid_spec=grid_spec,
        compiler_params=_params("parallel", "parallel", "arbitrary"),
        name="fox_flash",
    )(qi_tab, ki_tab, qm, k2, v2)


def _fox_paged_kernel(pt_ref, q_ref, kn_ref, vn_ref, fdn_ref, fb_ref, kt_hbm, vt_hbm, lft_hbm, o_ref, lfn_ref,
                      kbuf, vbuf, lbuf, sem, qall_scr, m_scr, l_scr, acc_scr, carry_scr,
                      *, SEQS, PP, NG, base, n_pages):
    b = pl.program_id(0)
    g = pl.program_id(1)
    n_steps = pl.num_programs(0) * NG
    t = b * NG + g
    slot = t % 2
    L = q_ref.shape[1]
    rows = H_D * L
    group = H_D // KV_D
    W = PP * PAGE_SIZE

    def page_copies(bb, gg, sl):
        cps = []
        for sq in range(SEQS):
            for i in range(PP):
                pg = base + pt_ref[bb * SEQS + sq, n_pages - 1 - (gg * PP + i)]
                dst = sq * PP + i
                cps.append(pltpu.make_async_copy(kt_hbm.at[pg], kbuf.at[sl, dst], sem.at[0, sl]))
                cps.append(pltpu.make_async_copy(vt_hbm.at[pg], vbuf.at[sl, dst], sem.at[1, sl]))
                cps.append(pltpu.make_async_copy(lft_hbm.at[pg], lbuf.at[sl, dst], sem.at[2, sl]))
        return cps

    @pl.when(t == 0)
    def _():
        for cp in page_copies(b, g, slot):
            cp.start()

    @pl.when(t + 1 < n_steps)
    def _():
        t1 = t + 1
        for cp in page_copies(t1 // NG, t1 % NG, 1 - slot):
            cp.start()

    @pl.when(g == 0)
    def _():
        for sq in range(SEQS):
            q = q_ref[sq] * SCALE_D
            for h in range(H_D):
                kv = h // group
                piece = q[:, h * HD_D:(h + 1) * HD_D]
                parts = []
                if kv > 0:
                    parts.append(jnp.zeros((L, kv * HD_D), F32))
                parts.append(piece)
                if kv < KV_D - 1:
                    parts.append(jnp.zeros((L, (KV_D - 1 - kv) * HD_D), F32))
                qall_scr[sq, h * L:(h + 1) * L, :] = jnp.concatenate(parts, axis=1).astype(BF16)
            lfn = -_softplus(-(fdn_ref[sq] + fb_ref[...]))
            lfn_ref[sq] = lfn
            cumn = _exact_dot_rhs(_lower_tri_bf16(L), lfn)
            cumn_t = cumn.T[0:H_D, :]
            bias = jnp.broadcast_to((-cumn_t)[:, None, :], (H_D, L, L)).reshape(rows, L)
            s = _dot_nt(qall_scr[sq], kn_ref[sq].astype(BF16)) + bias
            qidx = lax.broadcasted_iota(jnp.int32, (rows, L), 0) % L
            kidx = lax.broadcasted_iota(jnp.int32, (rows, L), 1)
            s = jnp.where(kidx <= qidx, s, NEG_INF)
            mx = jnp.max(s, axis=-1, keepdims=True)
            p = jnp.exp(s - mx)
            m_scr[sq] = mx
            l_scr[sq] = jnp.sum(p, axis=-1, keepdims=True)
            acc_scr[sq] = _dot(p.astype(BF16), vn_ref[sq].astype(BF16))
        carry_scr[...] = jnp.zeros_like(carry_scr)

    for cp in page_copies(b, g, slot):
        cp.wait()
    lane_in_page = lax.broadcasted_iota(jnp.int32, (1, W), 1) % PAGE_SIZE
    for sq in range(SEQS):
        pages = range(sq * PP, (sq + 1) * PP)
        kcat = jnp.concatenate([kbuf[slot, i] for i in pages], axis=1).astype(BF16)
        vcat = jnp.concatenate([vbuf[slot, i] for i in pages], axis=1).astype(BF16)
        s = _dot(qall_scr[sq], kcat)
        lft = jnp.concatenate([lbuf[slot, i] for i in pages], axis=1)
        y = lft
        step = 1
        while step < PAGE_SIZE:
            y = y + jnp.where(lane_in_page < PAGE_SIZE - step, pltpu.roll(y, W - step, axis=1), 0.0)
            step *= 2
        d_local = y - lft
        carry = carry_scr[sq]
        pieces = []
        for i in range(PP):
            pieces.append(d_local[:, i * PAGE_SIZE:(i + 1) * PAGE_SIZE] + carry)
            carry = carry + y[:, i * PAGE_SIZE:i * PAGE_SIZE + 1]
        carry_scr[sq] = carry
        bias_t = jnp.concatenate(pieces, axis=1)
        s = s + jnp.broadcast_to(bias_t[:, None, :], (H_D, L, W)).reshape(rows, W)
        m_prev = m_scr[sq]
        m_new = jnp.maximum(m_prev, jnp.max(s, axis=-1, keepdims=True))
        alpha = jnp.exp(m_prev - m_new)
        p = jnp.exp(s - m_new)
        l_scr[sq] = alpha * l_scr[sq] + jnp.sum(p, axis=-1, keepdims=True)
        acc_scr[sq] = alpha * acc_scr[sq] + _dot_nt(p.astype(BF16), vcat)
        m_scr[sq] = m_new

    @pl.when(g == NG - 1)
    def _():
        for sq in range(SEQS):
            o = acc_scr[sq] / l_scr[sq]
            parts = []
            for h in range(H_D):
                kv = h // group
                parts.append(o[h * L:(h + 1) * L, kv * HD_D:(kv + 1) * HD_D])
            o_ref[sq] = jnp.concatenate(parts, axis=1)


def _fox_paged(q, k_new, v_new, fd_new, fbias_pad, pool_k, pool_v, pool_lf, page_table, layer, PP):
    bsz, L, qw = q.shape
    n_pool = pool_k.shape[1]
    n_pages = page_table.shape[1]
    NG = n_pages // PP
    kvw = KV_D * HD_D
    pkt = jnp.transpose(pool_k, (0, 1, 3, 4, 2)).reshape(pool_k.shape[0] * n_pool, kvw, PAGE_SIZE)
    pvt = jnp.transpose(pool_v, (0, 1, 3, 4, 2)).reshape(pool_v.shape[0] * n_pool, kvw, PAGE_SIZE)
    plft = jnp.transpose(pool_lf, (0, 1, 3, 2)).reshape(pool_lf.shape[0] * n_pool, H_D, PAGE_SIZE)
    base = layer * n_pool

    SEQS = min(FOX_SEQS_PER_STEP, bsz)
    seq_map = lambda b, g, pt: (b, 0, 0)
    in_specs = [pl.BlockSpec((SEQS, L, qw), seq_map),
                pl.BlockSpec((SEQS, L, kvw), seq_map),
                pl.BlockSpec((SEQS, L, kvw), seq_map),
                pl.BlockSpec((SEQS, L, LANES), seq_map),
                pl.BlockSpec((1, LANES), lambda b, g, pt: (0, 0)),
                pl.BlockSpec(memory_space=pl.ANY),
                pl.BlockSpec(memory_space=pl.ANY),
                pl.BlockSpec(memory_space=pl.ANY)]
    kern = functools.partial(_fox_paged_kernel, SEQS=SEQS, PP=PP, NG=NG, base=base, n_pages=n_pages)
    grid_spec = pltpu.PrefetchScalarGridSpec(
        num_scalar_prefetch=1,
        grid=(bsz // SEQS, NG),
        in_specs=in_specs,
        out_specs=[pl.BlockSpec((SEQS, L, qw), seq_map), pl.BlockSpec((SEQS, L, LANES), seq_map)],
        scratch_shapes=[pltpu.VMEM((2, SEQS * PP, kvw, PAGE_SIZE), F32),
                        pltpu.VMEM((2, SEQS * PP, kvw, PAGE_SIZE), F32),
                        pltpu.VMEM((2, SEQS * PP, H_D, PAGE_SIZE), F32),
                        pltpu.SemaphoreType.DMA((3, 2)),
                        pltpu.VMEM((SEQS, H_D * L, kvw), BF16),
                        pltpu.VMEM((SEQS, H_D * L, 1), F32),
                        pltpu.VMEM((SEQS, H_D * L, 1), F32),
                        pltpu.VMEM((SEQS, H_D * L, kvw), F32),
                        pltpu.VMEM((SEQS, H_D, 1), F32)])
    return pl.pallas_call(
        kern,
        out_shape=[jax.ShapeDtypeStruct((bsz, L, qw), F32), jax.ShapeDtypeStruct((bsz, L, LANES), F32)],
        grid_spec=grid_spec,
        compiler_params=_params("arbitrary", "arbitrary"),
        name="fox_paged",
    )(page_table, q, k_new, v_new, fd_new, fbias_pad, pkt, pvt, plft)


def _pad_cols(w, n):
    return jnp.pad(w, ((0, 0), (0, n - w.shape[1])))


def _prep_w_ab(w):
    z, xbc, dt, q, k, v = jnp.split(w, np.cumsum((D_INNER_A, CONV_DIM_A, H_A, H_B * HD_B, KV_B * HD_B, KV_B * HD_B))[:-1].tolist(), axis=1)
    return jnp.concatenate([z, xbc, q, k, v, _pad_cols(dt, LANES)], axis=1).astype(BF16)


AB_SIZES = (D_INNER_A, CONV_DIM_A, H_B * HD_B, KV_B * HD_B, KV_B * HD_B, LANES)


def _prep_w_cd(w):
    sizes = (H_C * DK_C, H_C * DK_C, H_C * DV_C, H_C * DV_C, H_D * HD_D, KV_D * HD_D, KV_D * HD_D, H_D)
    parts = jnp.split(w, np.cumsum(sizes)[:-1].tolist(), axis=1)
    parts[-1] = _pad_cols(parts[-1], LANES)
    return jnp.concatenate(parts, axis=1).astype(BF16)


CD_SIZES = (H_C * DK_C, H_C * DK_C, H_C * DV_C, H_C * DV_C, H_D * HD_D, KV_D * HD_D, KV_D * HD_D, LANES)


def _trunk(x, c, st, prm, page_table):
    sample = page_table is not None
    bsz, seq, _ = x.shape
    if sample:
        bb, tl, out_tl, ffn_tl = min(SAMPLE_BATCH_TILE, bsz), seq, seq, seq
    else:
        bb, tl, out_tl, ffn_tl = 1, min(ROW_TILE, seq), min(OUT_ROW_TILE, seq), min(FFN_ROW_TILE, seq)
    n_sub = DEPTH * 2
    m_all = _ada_all(c, prm['ada_w'].reshape(n_sub, D_MODEL, 3 * D_MODEL), prm['ada_b'].reshape(n_sub, 1, 3 * D_MODEL))
    new = {}
    for l in range(DEPTH):
        j = l // 2
        m_mix = m_all[2 * l][:, None, :]
        m_ffn = m_all[2 * l + 1][:, None, :]
        if l % 2 == 0:
            z, xbc, qb, kb, vb, dtp = _mod_matmul(x, m_mix, _prep_w_ab(prm['w_in_ab'][j]), AB_SIZES, bb, tl)
            if sample:
                conv0p = jnp.pad(st['conv_a'][j], ((0, 0), (SUBLANES - (CONV_A - 1), 0), (0, 0)))
                ssm0 = st['ssm_a'][j]
            else:
                conv0p, ssm0 = None, None
            ya, conv1p, ssm1 = _mamba(z, xbc, dtp, conv0p, ssm0, prm['conv_w_a'][j], prm['conv_b_a'][j],
                                      prm['dt_bias_a'][j], prm['a_log_a'][j], prm['d_skip_a'][j],
                                      prm['norm_w_a'][j], T=min(SSD_CHUNK, seq))
            if sample:
                kbuf = st['swa_k'][j].reshape(bsz, WINDOW, KV_B * HD_B)
                vbuf = st['swa_v'][j].reshape(bsz, WINDOW, KV_B * HD_B)
                yb = _swa(qb, kbuf, kb, vbuf, vb, prm['sinks_b'][j], TQ=seq, SB=min(SWA_SAMPLE_SEQS, bsz),
                          prev_is_same_array=False)
                bk = jnp.concatenate([kbuf[:, seq:], kb], axis=1)
                bv = jnp.concatenate([vbuf[:, seq:], vb], axis=1)
            else:
                yb = _swa(qb, kb, kb, vb, vb, prm['sinks_b'][j], TQ=WINDOW, SB=1, prev_is_same_array=True)
                bk, bv = kb[:, -WINDOW:], vb[:, -WINDOW:]
            w_out = prm['w_out_ab'][j].astype(BF16)
            x = _mm_res_ln([ya, yb], [w_out[:D_INNER_A], w_out[D_INNER_A:]], x, m_mix,
                           prm['ln_g'][l, 0][None, :], prm['ln_b'][l, 0][None, :], bb, out_tl)
            new.setdefault('ssm_a', []).append(ssm1)
            new.setdefault('conv_a', []).append(conv1p[:, SUBLANES - (CONV_A - 1):, :])
            new.setdefault('swa_k', []).append(bk.reshape(bsz, WINDOW, KV_B, HD_B))
            new.setdefault('swa_v', []).append(bv.reshape(bsz, WINDOW, KV_B, HD_B))
        else:
            qc, fc, ic, gc, qd, kd, vd, fdp = _mod_matmul(x, m_mix, _prep_w_cd(prm['w_in_cd'][j]), CD_SIZES, bb, tl)
            s0 = st['hgrn_c'][j] if sample else None
            yc, s1 = _gla(qc, fc, ic, gc, prm['lb_c'], prm['gnorm_c'][j], s0, layer=l, T=min(GLA_CHUNK, seq))
            fbias_pad = jnp.zeros((1, LANES), F32).at[0, :H_D].set(prm['fbias_d'][j])
            if sample:
                yd, lfp = _fox_paged(qd, kd, vd, fdp, fbias_pad, st['fox_k'], st['fox_v'], st['fox_logf'],
                                     page_table, j, FOX_PAGES_PER_STEP)
            else:
                lfp, qm, k2, v2 = _fox_prep(qd, kd, vd, fdp, fbias_pad, min(FOX_PREP_T, seq))
                yd = _fox_flash(qm, k2, v2, min(FOX_TQ, seq), min(FOX_TK, seq))
            w_out = prm['w_out_cd'][j].astype(BF16)
            x = _mm_res_ln([yc, yd], [w_out[:H_C * DV_C], w_out[H_C * DV_C:]], x, m_mix,
                           prm['ln_g'][l, 0][None, :], prm['ln_b'][l, 0][None, :], bb, out_tl)
            new.setdefault('hgrn_c', []).append(s1)
            new.setdefault('fox_k', []).append(kd.reshape(bsz, seq, KV_D, HD_D))
            new.setdefault('fox_v', []).append(vd.reshape(bsz, seq, KV_D, HD_D))
            new.setdefault('fox_logf', []).append(lfp[:, :, :H_D])
        buf0 = st['ffn_conv'][l] if sample else jnp.zeros((bsz, FFN_CONV - 1, D_FF), F32)
        a, buf1 = _ffn_in(x, m_ffn, prm['ffn_w_in'][l].astype(BF16), buf0, prm['ffn_conv_w'][l],
                          prm['ffn_conv_b'][l][None, :], bb, ffn_tl)
        x = _mm_res_ln([a], [prm['ffn_w_out'][l].astype(BF16)], x, m_ffn,
                       prm['ln_g'][l, 1][None, :], prm['ln_b'][l, 1][None, :], bb, ffn_tl)
        new.setdefault('ffn_conv', []).append(buf1)
    return x, {name: jnp.stack(rows, axis=0) for name, rows in new.items()}


def kernel(x_prompt, x_sample, state_ssm_a, state_conv_a, cache_swa_k, cache_swa_v, state_hgrn_c, cache_fox_k, cache_fox_v, cache_fox_logf, state_ffn_conv, page_table, c_prompt, c_sample, ada_w, ada_b, ln_g, ln_b, w_in_ab, w_out_ab, conv_w_a, conv_b_a, dt_bias_a, a_log_a, d_skip_a, norm_w_a, sinks_b, w_in_cd, w_out_cd, lb_c, gnorm_c, fbias_d, ffn_w_in, ffn_conv_w, ffn_conv_b, ffn_w_out):
    prm = dict(ada_w=ada_w, ada_b=ada_b, ln_g=ln_g, ln_b=ln_b, w_in_ab=w_in_ab, w_out_ab=w_out_ab,
               conv_w_a=conv_w_a, conv_b_a=conv_b_a, dt_bias_a=dt_bias_a, a_log_a=a_log_a,
               d_skip_a=d_skip_a, norm_w_a=norm_w_a, sinks_b=sinks_b, w_in_cd=w_in_cd, w_out_cd=w_out_cd,
               lb_c=lb_c, gnorm_c=gnorm_c, fbias_d=fbias_d, ffn_w_in=ffn_w_in, ffn_conv_w=ffn_conv_w,
               ffn_conv_b=ffn_conv_b, ffn_w_out=ffn_w_out)
    st = dict(ssm_a=state_ssm_a, conv_a=state_conv_a, swa_k=cache_swa_k, swa_v=cache_swa_v,
              hgrn_c=state_hgrn_c, fox_k=cache_fox_k, fox_v=cache_fox_v, fox_logf=cache_fox_logf,
              ffn_conv=state_ffn_conv)
    y_p, sp = _trunk(x_prompt, c_prompt, None, prm, None)
    y_s, ss = _trunk(x_sample, c_sample, st, prm, page_table)
    return (y_p, y_s,
            sp['ssm_a'], ss['ssm_a'], sp['conv_a'], ss['conv_a'],
            sp['swa_k'], ss['swa_k'], sp['swa_v'], ss['swa_v'],
            sp['hgrn_c'], ss['hgrn_c'],
            sp['fox_k'], ss['fox_k'], sp['fox_v'], ss['fox_v'], sp['fox_logf'], ss['fox_logf'],
            sp['ffn_conv'], ss['ffn_conv'])
```

```python
import functools
import math

import numpy as np
import jax
import jax.numpy as jnp
from jax import lax
from jax.experimental import pallas as pl
from jax.experimental.pallas import tpu as pltpu

F32 = jnp.float32
BF16 = jnp.bfloat16
NEG_INF = float("-inf")

D_MODEL = 1024
DEPTH = 2
PAGE_SIZE = 128
H_A, P_A, N_A, G_A, CONV_A = 16, 64, 128, 2, 4
D_INNER_A = H_A * P_A
CONV_DIM_A = D_INNER_A + 2 * G_A * N_A
H_B, KV_B, HD_B, WINDOW = 8, 2, 64, 128
H_C, DK_C, DV_C = 4, 128, 128
H_D, KV_D, HD_D = 16, 4, 64
D_FF, FFN_CONV = 2816, 3
ALPHA = (2 * DEPTH) ** 0.25
SCALE_B = HD_B ** -0.5
SCALE_D = HD_D ** -0.5
LN_EPS = 1e-5
RMS_EPS = 1e-6
LOG2E = math.log2(math.e)

LANES = 128
SUBLANES = 8
VMEM_LIMIT_BYTES = 56 * 1024 * 1024

SSD_CHUNK = 128
GLA_CHUNK = 256
GLA_BLOCK = 16
FOX_TQ = 1024
FOX_TK = 512
FOX_HEAD_GROUPS = 2
FOX_PREP_T = 512
FOX_PAGES_PER_STEP = 16
ROW_TILE = 512
OUT_ROW_TILE = 1024
FFN_ROW_TILE = 512
SAMPLE_BATCH_TILE = 32
SWA_SAMPLE_SEQS = 8
MAMBA_SAMPLE_SEQS = 4
GLA_SAMPLE_SEQS = 4
FOX_SEQS_PER_STEP = 2


def _params(*sem):
    return pltpu.CompilerParams(dimension_semantics=sem, vmem_limit_bytes=VMEM_LIMIT_BYTES)


def _resident(shape):
    nd = len(shape)
    return pl.BlockSpec(shape, lambda *_: (0,) * nd, pipeline_mode=pl.Buffered(1))


def _silu(x):
    return x * jax.nn.sigmoid(x)


def _softplus(x):
    return jnp.maximum(x, 0.0) + jnp.log1p(jnp.exp(-jnp.abs(x)))


def _dot(a, b):
    return jnp.dot(a, b, preferred_element_type=F32)


def _dot_nt(a, b):
    return lax.dot_general(a, b, (((1,), (1,)), ((), ())), preferred_element_type=F32)


def _dot_tn(a, b):
    return lax.dot_general(a, b, (((0,), (0,)), ((), ())), preferred_element_type=F32)


def _split3(x):
    hi = x.astype(BF16)
    r = x - hi.astype(F32)
    mid = r.astype(BF16)
    lo = (r - mid.astype(F32)).astype(BF16)
    return hi, mid, lo


def _exact_dot_lhs(x, w01):
    hi, mid, lo = _split3(x)
    return _dot(hi, w01) + _dot(mid, w01) + _dot(lo, w01)


def _exact_dot_rhs(w01, x):
    hi, mid, lo = _split3(x)
    return _dot(w01, hi) + _dot(w01, mid) + _dot(w01, lo)


def _lower_tri_bf16(t):
    r = lax.broadcasted_iota(jnp.int32, (t, t), 0)
    c = lax.broadcasted_iota(jnp.int32, (t, t), 1)
    return jnp.where(r >= c, 1.0, 0.0).astype(BF16)


def _layer_norm_rows(r, g, b):
    mu = jnp.mean(r, axis=-1, keepdims=True)
    d = r - mu
    var = jnp.mean(d * d, axis=-1, keepdims=True)
    return d * lax.rsqrt(var + LN_EPS) * g + b


def _ada_kernel(c_ref, w_ref, b_ref, o_ref):
    s = _silu(c_ref[...]).astype(BF16)
    o_ref[0] = _dot(s, w_ref[0].astype(BF16)) + b_ref[0]


def _ada_all(c, ada_w, ada_b):
    bsz = c.shape[0]
    n_sub = ada_w.shape[0]
    n_out = ada_w.shape[2]
    tn = D_MODEL
    return pl.pallas_call(
        _ada_kernel,
        out_shape=jax.ShapeDtypeStruct((n_sub, bsz, n_out), F32),
        grid=(n_sub, n_out // tn),
        in_specs=[pl.BlockSpec((bsz, D_MODEL), lambda s, j: (0, 0)),
                  pl.BlockSpec((1, D_MODEL, tn), lambda s, j: (s, 0, j)),
                  pl.BlockSpec((1, 1, tn), lambda s, j: (s, 0, j))],
        out_specs=pl.BlockSpec((1, bsz, tn), lambda s, j: (s, 0, j)),
        compiler_params=_params("parallel", "parallel"),
        name="ada_mod",
    )(c, ada_w, ada_b)


def _mod_matmul_kernel(x_ref, m_ref, w_ref, *o_refs, bb, tl, sizes):
    d = x_ref.shape[-1]
    m = m_ref[...]
    h = x_ref[...] * (1.0 + m[:, :, d:2 * d]) + m[:, :, 0:d]
    h2 = h.reshape(bb * tl, d).astype(BF16)
    off = 0
    for o_ref, n in zip(o_refs, sizes):
        o_ref[...] = _dot(h2, w_ref[:, off:off + n]).reshape(bb, tl, n)
        off += n


def _mod_matmul(x, m, w_bf16, sizes, bb, tl):
    bsz, seq, d = x.shape
    n_tot = w_bf16.shape[1]
    kern = functools.partial(_mod_matmul_kernel, bb=bb, tl=tl, sizes=tuple(sizes))
    return pl.pallas_call(
        kern,
        out_shape=[jax.ShapeDtypeStruct((bsz, seq, n), F32) for n in sizes],
        grid=(bsz // bb, seq // tl),
        in_specs=[pl.BlockSpec((bb, tl, d), lambda i, j: (i, j, 0)),
                  pl.BlockSpec((bb, 1, 3 * d), lambda i, j: (i, 0, 0)),
                  _resident((d, n_tot))],
        out_specs=[pl.BlockSpec((bb, tl, n), lambda i, j: (i, j, 0)) for n in sizes],
        compiler_params=_params("parallel", "parallel"),
        name="mod_matmul",
    )(x, m, w_bf16)


def _mm_res_ln_kernel(*refs, bb, tl, n_in):
    y_refs = refs[0:n_in]
    w_refs = refs[n_in:2 * n_in]
    x_ref, m_ref, g_ref, b_ref, o_ref = refs[2 * n_in:]
    d = x_ref.shape[-1]
    mix = None
    for y_ref, w_ref in zip(y_refs, w_refs):
        y2 = y_ref[...].reshape(bb * tl, y_ref.shape[-1]).astype(BF16)
        part = _dot(y2, w_ref[...])
        mix = part if mix is None else mix + part
    gate = m_ref[:, :, 2 * d:3 * d]
    r = ALPHA * x_ref[...] + (1.0 + gate) * mix.reshape(bb, tl, d)
    o_ref[...] = _layer_norm_rows(r, g_ref[...], b_ref[...])


def _mm_res_ln(ys, ws_bf16, x, m, g, b, bb, tl):
    bsz, seq, d = x.shape
    kern = functools.partial(_mm_res_ln_kernel, bb=bb, tl=tl, n_in=len(ys))
    return pl.pallas_call(
        kern,
        out_shape=jax.ShapeDtypeStruct((bsz, seq, d), F32),
        grid=(bsz // bb, seq // tl),
        in_specs=([pl.BlockSpec((bb, tl, y.shape[-1]), lambda i, j: (i, j, 0)) for y in ys]
                  + [_resident(w.shape) for w in ws_bf16]
                  + [pl.BlockSpec((bb, tl, d), lambda i, j: (i, j, 0)),
                     pl.BlockSpec((bb, 1, 3 * d), lambda i, j: (i, 0, 0)),
                     pl.BlockSpec((1, d), lambda i, j: (0, 0)),
                     pl.BlockSpec((1, d), lambda i, j: (0, 0))]),
        out_specs=pl.BlockSpec((bb, tl, d), lambda i, j: (i, j, 0)),
        compiler_params=_params("parallel", "parallel"),
        name="mm_res_ln",
    )(*ys, *ws_bf16, x, m, g, b)


def _ffn_in_kernel(x_ref, m_ref, w_ref, buf0_ref, cw_ref, cb_ref, a_ref, buf1_ref, carry_scr, *, bb, tl, n_col):
    d = x_ref.shape[-1]
    dff = a_ref.shape[-1]
    j = pl.program_id(1)

    @pl.when(j == 0)
    def _():
        carry_scr[...] = buf0_ref[...]

    m = m_ref[...]
    h = x_ref[...] * (1.0 + m[:, :, d:2 * d]) + m[:, :, 0:d]
    h2 = h.reshape(bb * tl, d).astype(BF16)
    rows = bb * tl
    cw = dff // n_col
    t = lax.broadcasted_iota(jnp.int32, (rows, 1), 0) % tl
    for ci in range(n_col):
        lo = ci * cw
        u = _dot(h2, w_ref[:, lo:lo + cw])
        g = _dot(h2, w_ref[:, dff + lo:dff + lo + cw])
        carry = carry_scr[:, :, lo:lo + cw]
        prev1 = jnp.broadcast_to(carry[:, 1:2, :], (bb, tl, cw)).reshape(rows, cw)
        prev0 = jnp.broadcast_to(carry[:, 0:1, :], (bb, tl, cw)).reshape(rows, cw)
        tap1 = jnp.where(t == 0, prev1, pltpu.roll(g, 1, axis=0))
        tap2 = jnp.where(t == 0, prev0, jnp.where(t == 1, prev1, pltpu.roll(g, 2, axis=0)))
        y = cb_ref[:, lo:lo + cw] + g * cw_ref[2:3, lo:lo + cw]
        y = y + tap2 * cw_ref[0:1, lo:lo + cw]
        y = y + tap1 * cw_ref[1:2, lo:lo + cw]
        a_ref[:, :, lo:lo + cw] = (_silu(y) * u).reshape(bb, tl, cw)
        last2 = g.reshape(bb, tl, cw)[:, tl - (FFN_CONV - 1):, :]
        carry_scr[:, :, lo:lo + cw] = last2
        buf1_ref[:, :, lo:lo + cw] = last2


def _ffn_in(x, m, w_bf16, buf0, conv_w, conv_b, bb, tl):
    bsz, seq, d = x.shape
    kern = functools.partial(_ffn_in_kernel, bb=bb, tl=tl, n_col=2)
    return pl.pallas_call(
        kern,
        out_shape=[jax.ShapeDtypeStruct((bsz, seq, D_FF), F32),
                   jax.ShapeDtypeStruct((bsz, FFN_CONV - 1, D_FF), F32)],
        grid=(bsz // bb, seq // tl),
        in_specs=[pl.BlockSpec((bb, tl, d), lambda i, j: (i, j, 0)),
                  pl.BlockSpec((bb, 1, 3 * d), lambda i, j: (i, 0, 0)),
                  _resident((d, 2 * D_FF)),
                  pl.BlockSpec((bb, FFN_CONV - 1, D_FF), lambda i, j: (i, 0, 0)),
                  pl.BlockSpec((FFN_CONV, D_FF), lambda i, j: (0, 0)),
                  pl.BlockSpec((1, D_FF), lambda i, j: (0, 0))],
        out_specs=[pl.BlockSpec((bb, tl, D_FF), lambda i, j: (i, j, 0)),
                   pl.BlockSpec((bb, FFN_CONV - 1, D_FF), lambda i, j: (i, 0, 0))],
        scratch_shapes=[pltpu.VMEM((bb, FFN_CONV - 1, D_FF), F32)],
        compiler_params=_params("parallel", "arbitrary"),
        name="ffn_in_conv",
    )(x, m, w_bf16, buf0, conv_w, conv_b)


def _mamba_kernel(*refs, SB, T, has_state, n_chunks):
    if has_state:
        (z_ref, xbc_ref, dt_ref, conv0_ref, ssm0_ref, cw_ref, cb_ref, dtb_ref, alog_ref, dsk_ref, nw_ref, e_ref,
         y_ref, conv1_ref, ssm1_ref, carry_scr, s_scr) = refs
    else:
        (z_ref, xbc_ref, dt_ref, cw_ref, cb_ref, dtb_ref, alog_ref, dsk_ref, nw_ref, e_ref,
         y_ref, conv1_ref, ssm1_ref, carry_scr, s_scr) = refs
    c = pl.program_id(1)
    n_pair = H_A // 2
    cdim = CONV_DIM_A

    @pl.when(c == 0)
    def _():
        if has_state:
            carry_scr[...] = conv0_ref[...]
            for sb in range(SB):
                for j in range(n_pair):
                    s_scr[sb * n_pair + j] = jnp.concatenate([ssm0_ref[sb, 2 * j], ssm0_ref[sb, 2 * j + 1]], axis=0)
        else:
            carry_scr[...] = jnp.zeros_like(carry_scr)
            s_scr[...] = jnp.zeros_like(s_scr)

    row8 = lax.broadcasted_iota(jnp.int32, (SUBLANES, cdim), 0)
    tri = _lower_tri_bf16(T)
    e = e_ref[...]
    rr = lax.broadcasted_iota(jnp.int32, (T, T), 0)
    cc = lax.broadcasted_iota(jnp.int32, (T, T), 1)
    causal = rr >= cc
    lane = lax.broadcasted_iota(jnp.int32, (1, 2 * P_A), 1)
    low = lane < P_A
    first_of_pair = lax.broadcasted_iota(jnp.int32, (2 * P_A, 1), 0) < P_A
    heads_per_group = H_A // G_A

    for sb in range(SB):
        xb = xbc_ref[sb]
        carry = carry_scr[sb]
        acc = cb_ref[...] + xb * cw_ref[CONV_A - 1:CONV_A, :]
        for dshift in (3, 2, 1):
            rolled = pltpu.roll(xb, dshift, axis=0)
            top = jnp.where(row8 < dshift, pltpu.roll(carry, dshift, axis=0), rolled[0:SUBLANES])
            sh = top if T == SUBLANES else jnp.concatenate([top, rolled[SUBLANES:]], axis=0)
            acc = acc + sh * cw_ref[CONV_A - 1 - dshift:CONV_A - dshift, :]
        last8 = xb[T - SUBLANES:T, :]
        carry_scr[sb] = last8
        conv1_ref[sb] = last8
        xc = _silu(acc)
        xs = xc[:, 0:D_INNER_A]
        bm = xc[:, D_INNER_A:D_INNER_A + G_A * N_A]
        cm = xc[:, D_INNER_A + G_A * N_A:cdim]

        dt = _softplus(dt_ref[sb] + dtb_ref[...])
        la = dt * (-jnp.exp(alog_ref[...]))
        cum = _exact_dot_rhs(tri, la)
        dtx = _exact_dot_lhs(dt, e)
        cumx = _exact_dot_lhs(cum, e)
        x = xs * dtx
        decx = jnp.exp(cumx)
        xd = (x * jnp.exp(cumx[T - 1:T, :] - cumx)).astype(BF16)
        cum_t = cum.T

        cb_g, c_g, b_g = [], [], []
        for g in range(G_A):
            cg = cm[:, g * N_A:(g + 1) * N_A].astype(BF16)
            bg = bm[:, g * N_A:(g + 1) * N_A].astype(BF16)
            c_g.append(cg)
            b_g.append(bg)
            cb_g.append(_dot_nt(cg, bg))

        outs = []
        for j in range(n_pair):
            g = (2 * j) // heads_per_group
            sl = slice(j * 2 * P_A, (j + 1) * 2 * P_A)
            xj = x[:, sl]
            yj = None
            for eidx in (0, 1):
                h = 2 * j + eidx
                col = cum[:, h:h + 1]
                row = cum_t[h:h + 1, :]
                decay = jnp.exp(jnp.where(causal, col - row, NEG_INF))
                scores = (cb_g[g] * decay).astype(BF16)
                xm = jnp.where(low if eidx == 0 else jnp.logical_not(low), xj, 0.0).astype(BF16)
                part = _dot(scores, xm)
                yj = part if yj is None else yj + part
            s_old = s_scr[sb * n_pair + j]
            yj = yj + _dot_nt(c_g[g], s_old.astype(BF16)) * decx[:, sl]
            e_last = jnp.where(first_of_pair, jnp.exp(cum_t[2 * j:2 * j + 1, T - 1:T]),
                               jnp.exp(cum_t[2 * j + 1:2 * j + 2, T - 1:T]))
            s_scr[sb * n_pair + j] = e_last * s_old + _dot_tn(xd[:, sl], b_g[g])
            outs.append(yj)
        y = jnp.concatenate(outs, axis=1) + dsk_ref[...] * xs
        y = y * _silu(z_ref[sb])
        y_ref[sb] = y * lax.rsqrt(jnp.mean(y * y, axis=-1, keepdims=True) + RMS_EPS) * nw_ref[...]

    @pl.when(c == n_chunks - 1)
    def _():
        for sb in range(SB):
            for j in range(n_pair):
                sj = s_scr[sb * n_pair + j]
                ssm1_ref[sb, 2 * j] = sj[0:P_A, :]
                ssm1_ref[sb, 2 * j + 1] = sj[P_A:2 * P_A, :]


def _mamba(z, xbc, dtp, conv0p, ssm0, conv_w, conv_b, dt_bias, a_log, d_skip, norm_w, T):
    bsz, seq, _ = z.shape
    n_chunks = seq // T
    has_state = conv0p is not None
    dtb = jnp.zeros((1, LANES), F32).at[0, :H_A].set(dt_bias)
    alog = jnp.zeros((1, LANES), F32).at[0, :H_A].set(a_log)
    dsk = jnp.repeat(d_skip, P_A)[None, :]
    head_of_col = np.arange(D_INNER_A) // P_A
    expand = jnp.asarray((np.arange(LANES)[:, None] == head_of_col[None, :]).astype(np.float32), dtype=BF16)
    SB = min(MAMBA_SAMPLE_SEQS, bsz) if has_state else 1
    kern = functools.partial(_mamba_kernel, SB=SB, T=T, has_state=has_state, n_chunks=n_chunks)
    tok = lambda w: pl.BlockSpec((SB, T, w), lambda b, c: (b, c, 0))
    const = lambda shape: pl.BlockSpec(shape, lambda b, c: (0,) * len(shape))
    conv_spec = pl.BlockSpec((SB, SUBLANES, CONV_DIM_A), lambda b, c: (b, 0, 0))
    ssm_spec = pl.BlockSpec((SB, H_A, P_A, N_A), lambda b, c: (b, 0, 0, 0))
    in_specs = [tok(D_INNER_A), tok(CONV_DIM_A), tok(LANES)]
    args = [z, xbc, dtp]
    if has_state:
        in_specs += [conv_spec, ssm_spec]
        args += [conv0p, jnp.swapaxes(ssm0, 2, 3)]
    in_specs += [const((CONV_A, CONV_DIM_A)), const((1, CONV_DIM_A)), const((1, LANES)), const((1, LANES)),
                 const((1, D_INNER_A)), const((1, D_INNER_A)), const((LANES, D_INNER_A))]
    args += [conv_w, conv_b[None, :], dtb, alog, dsk, norm_w[None, :], expand]
    y, conv1p, ssm1_t = pl.pallas_call(
        kern,
        out_shape=[jax.ShapeDtypeStruct((bsz, seq, D_INNER_A), F32),
                   jax.ShapeDtypeStruct((bsz, SUBLANES, CONV_DIM_A), F32),
                   jax.ShapeDtypeStruct((bsz, H_A, P_A, N_A), F32)],
        grid=(bsz // SB, n_chunks),
        in_specs=in_specs,
        out_specs=[tok(D_INNER_A), conv_spec, ssm_spec],
        scratch_shapes=[pltpu.VMEM((SB, SUBLANES, CONV_DIM_A), F32),
                        pltpu.VMEM((SB * (H_A // 2), 2 * P_A, N_A), F32)],
        compiler_params=_params("parallel", "arbitrary"),
        name="mamba_ssd",
    )(*args)
    return y, conv1p, jnp.swapaxes(ssm1_t, 2, 3)


def _swa_kernel(sinks_ref, q_ref, kp_ref, kc_ref, vp_ref, vc_ref, o_ref, *, SB, TQ, first_has_prev):
    n = pl.program_id(1)
    lane = lax.broadcasted_iota(jnp.int32, (1, 2 * HD_B), 1)
    low = lane < HD_B
    roll64 = lambda a: pltpu.roll(a, HD_B, axis=1)
    has_prev = jnp.logical_or(n > 0, first_has_prev)
    group = H_B // KV_B
    classes = [[h for h in range(H_B) if ((h // group) == (h % 2)) == flag] for flag in (True, False)]
    R = len(classes[0]) * TQ
    ip = lax.broadcasted_iota(jnp.int32, (R, WINDOW), 0) % TQ
    jp = lax.broadcasted_iota(jnp.int32, (R, WINDOW), 1)
    valid_prev = jnp.logical_and(jp >= ip, has_prev)
    ic = lax.broadcasted_iota(jnp.int32, (R, TQ), 0) % TQ
    jc = lax.broadcasted_iota(jnp.int32, (R, TQ), 1)
    valid_cur = jc <= ic
    for sb in range(SB):
        q = q_ref[sb] * SCALE_B
        kv_arrays = (kp_ref[sb], kc_ref[sb], vp_ref[sb], vc_ref[sb])
        res = {}
        for cls, heads in enumerate(classes):
            kpv, kcv, vpv, vcv = [(a if cls == 0 else roll64(a)).astype(BF16) for a in kv_arrays]
            qs = jnp.concatenate(
                [jnp.where(low if h % 2 == 0 else jnp.logical_not(low),
                           q[:, (h // 2) * 2 * HD_B:(h // 2 + 1) * 2 * HD_B], 0.0) for h in heads],
                axis=0).astype(BF16)
            sink = jnp.concatenate([jnp.full((TQ, 1), sinks_ref[h], F32) for h in heads], axis=0)
            sp = jnp.where(valid_prev, _dot_nt(qs, kpv), NEG_INF)
            sc = jnp.where(valid_cur, _dot_nt(qs, kcv), NEG_INF)
            mx = jnp.maximum(jnp.maximum(jnp.max(sp, axis=-1, keepdims=True),
                                         jnp.max(sc, axis=-1, keepdims=True)), sink)
            pp = jnp.exp(sp - mx)
            pc = jnp.exp(sc - mx)
            den = (jnp.sum(pp, axis=-1, keepdims=True) + jnp.sum(pc, axis=-1, keepdims=True)
                   + jnp.exp(sink - mx))
            o = (_dot(pp.astype(BF16), vpv) + _dot(pc.astype(BF16), vcv)) / den
            for idx, h in enumerate(heads):
                res[h] = o[idx * TQ:(idx + 1) * TQ]
        o_ref[sb] = jnp.concatenate([jnp.where(low, res[2 * j], res[2 * j + 1]) for j in range(H_B // 2)], axis=1)


def _swa(q, k_prev, k_cur, v_prev, v_cur, sinks, TQ, SB, prev_is_same_array):
    bsz, seq, _ = q.shape
    nb = seq // TQ
    kvw = KV_B * HD_B
    if prev_is_same_array:
        prev_map = lambda b, n: (b, jnp.maximum(n - 1, 0), 0)
    else:
        prev_map = lambda b, n: (b, 0, 0)
    kern = functools.partial(_swa_kernel, SB=SB, TQ=TQ, first_has_prev=not prev_is_same_array)
    return pl.pallas_call(
        kern,
        out_shape=jax.ShapeDtypeStruct((bsz, seq, H_B * HD_B), F32),
        grid=(bsz // SB, nb),
        in_specs=[pl.BlockSpec(memory_space=pltpu.SMEM),
                  pl.BlockSpec((SB, TQ, H_B * HD_B), lambda b, n: (b, n, 0)),
                  pl.BlockSpec((SB, WINDOW, kvw), prev_map),
                  pl.BlockSpec((SB, TQ, kvw), lambda b, n: (b, n, 0)),
                  pl.BlockSpec((SB, WINDOW, kvw), prev_map),
                  pl.BlockSpec((SB, TQ, kvw), lambda b, n: (b, n, 0))],
        out_specs=pl.BlockSpec((SB, TQ, H_B * HD_B), lambda b, n: (b, n, 0)),
        compiler_params=_params("parallel", "parallel"),
        name="swa_attn",
    )(sinks, q, k_prev, k_cur, v_prev, v_cur)


def _gla_kernel(*refs, SB, T, blk, layer, has_state, n_chunks):
    if has_state:
        (qr_ref, fr_ref, ir_ref, gr_ref, lbc_ref, gw_ref, s0_ref, o_ref, s1_ref,
         st_scr, qd_scr, kd_scr, el_scr, v_scr, od_scr, oi_scr) = refs
    else:
        (qr_ref, fr_ref, ir_ref, gr_ref, lbc_ref, gw_ref, o_ref, s1_ref,
         st_scr, qd_scr, kd_scr, el_scr, v_scr, od_scr, oi_scr) = refs
    c = pl.program_id(1)
    n_blk = T // blk

    @pl.when(c == 0)
    def _():
        if has_state:
            for sb in range(SB):
                for h in range(H_C):
                    st_scr[sb * H_C + h] = s0_ref[sb, h].T
        else:
            st_scr[...] = jnp.zeros_like(st_scr)

    lbc = lbc_ref[...]
    ex = jnp.exp(lbc - jnp.max(lbc, axis=0, keepdims=True))
    sm = ex / jnp.sum(ex, axis=0, keepdims=True)
    lb = jnp.sum(sm[1:layer + 1], axis=0, keepdims=True) if layer >= 1 else jnp.zeros_like(sm[0:1])
    rr = lax.broadcasted_iota(jnp.int32, (T, T), 0)
    cc = lax.broadcasted_iota(jnp.int32, (T, T), 1)
    same_blk = (rr // blk) == (cc // blk)
    tri_blk = jnp.where(jnp.logical_and(same_blk, rr >= cc), 1.0, 0.0).astype(BF16)
    ones_blk = jnp.where(same_blk, 1.0, 0.0).astype(BF16)
    t_in_blk = lax.broadcasted_iota(jnp.int32, (T, 1), 0) % blk

    seq_heads = [(sb, h) for sb in range(SB) for h in range(H_C)]
    for sb, h in seq_heads:
        sl = slice(h * DK_C, (h + 1) * DK_C)
        q = _silu(qr_ref[sb, :, sl])
        fg = lb[:, sl] + (1.0 - lb[:, sl]) * jax.nn.sigmoid(fr_ref[sb, :, sl])
        lf = jnp.log(fg)
        key = 1.0 - fg
        v = ir_ref[sb, :, sl]
        cr = _exact_dot_rhs(tri_blk, lf)
        last = _exact_dot_rhs(ones_blk, lf)
        od = jnp.sum(q * key, axis=-1, keepdims=True) * v
        for off in range(1, blk):
            dec = jnp.exp(jnp.where(t_in_blk >= off, cr - pltpu.roll(cr, off, axis=0), NEG_INF))
            p = jnp.sum(q * dec * pltpu.roll(key, off, axis=0), axis=-1, keepdims=True)
            od = od + p * pltpu.roll(v, off, axis=0)
        od_scr[sb, :, sl] = od
        qd_scr[sb, :, sl] = q * jnp.exp(cr)
        kd_scr[sb, :, sl] = key * jnp.exp(last - cr)
        el_scr[sb, :, sl] = jnp.exp(last)
        v_scr[sb, :, sl] = v

    def blk_step(i):
        r0 = i * blk if isinstance(i, int) else pl.multiple_of(i * blk, blk)
        rows = pl.ds(r0, blk)
        for sb, h in seq_heads:
            sl = slice(h * DK_C, (h + 1) * DK_C)
            st = st_scr[sb * H_C + h]
            oi_scr[sb, rows, sl] = _dot_nt(qd_scr[sb, rows, sl].astype(BF16), st.astype(BF16))
            st_scr[sb * H_C + h] = (el_scr[sb, pl.ds(r0, 1), sl] * st
                                    + _dot_tn(v_scr[sb, rows, sl].astype(BF16), kd_scr[sb, rows, sl].astype(BF16)))

    if n_blk == 1:
        blk_step(0)
    else:
        def body(i, carry):
            blk_step(i)
            return carry
        lax.fori_loop(0, n_blk, body, 0, unroll=2)

    for sb, h in seq_heads:
        sl = slice(h * DK_C, (h + 1) * DK_C)
        o = od_scr[sb, :, sl] + oi_scr[sb, :, sl]
        o = o * lax.rsqrt(jnp.mean(o * o, axis=-1, keepdims=True) + RMS_EPS) * gw_ref[:, sl]
        o_ref[sb, :, sl] = o * _silu(gr_ref[sb, :, sl])

    @pl.when(c == n_chunks - 1)
    def _():
        for sb, h in seq_heads:
            s1_ref[sb, h] = st_scr[sb * H_C + h].T


def _gla(qr, fr, ir, gr, lb_c, gnorm_w, s0, layer, T):
    bsz, seq, w = qr.shape
    n_chunks = seq // T
    has_state = s0 is not None
    blk = min(GLA_BLOCK, T)
    SB = min(GLA_SAMPLE_SEQS, bsz) if has_state else 1
    kern = functools.partial(_gla_kernel, SB=SB, T=T, blk=blk, layer=layer, has_state=has_state,
                             n_chunks=n_chunks)
    tok = pl.BlockSpec((SB, T, w), lambda b, c: (b, c, 0))
    st_spec = pl.BlockSpec((SB, H_C, DK_C, DV_C), lambda b, c: (b, 0, 0, 0))
    in_specs = [tok, tok, tok, tok,
                pl.BlockSpec((DEPTH, w), lambda b, c: (0, 0)),
                pl.BlockSpec((1, w), lambda b, c: (0, 0))]
    args = [qr, fr, ir, gr, lb_c, gnorm_w[None, :]]
    if has_state:
        in_specs.append(st_spec)
        args.append(s0)
    return pl.pallas_call(
        kern,
        out_shape=[jax.ShapeDtypeStruct((bsz, seq, w), F32),
                   jax.ShapeDtypeStruct((bsz, H_C, DK_C, DV_C), F32)],
        grid=(bsz // SB, n_chunks),
        in_specs=in_specs,
        out_specs=[tok, st_spec],
        scratch_shapes=([pltpu.VMEM((SB * H_C, DV_C, DK_C), F32)]
                        + [pltpu.VMEM((SB, T, w), F32) for _ in range(6)]),
        compiler_params=_params("parallel", "arbitrary"),
        name="gla_hgrn2",
    )(*args)


BIAS_TERMS = 3
BIAS_SELF_LANE = BIAS_TERMS * H_D


def _fox_prep_kernel(q_ref, k_ref, v_ref, fd_ref, fb_ref, lf_ref, qm_ref, k2_ref, v2_ref, carry_scr, *, T):
    c = pl.program_id(1)

    @pl.when(c == 0)
    def _():
        carry_scr[...] = jnp.zeros_like(carry_scr)

    lf = -_softplus(-(fd_ref[0] + fb_ref[...]))
    lf_ref[0] = lf
    cum = _exact_dot_rhs(_lower_tri_bf16(T), lf) + carry_scr[...]
    carry_scr[...] = cum[T - 1:T, :]
    hi, mid, lo = [t.astype(F32) for t in _split3(cum * LOG2E)]

    lane = lax.broadcasted_iota(jnp.int32, (1, 2 * HD_D), 1)
    low = lane < HD_D
    roll_half = lambda a: pltpu.roll(a, HD_D, axis=1)
    xk_low = jnp.where(lane < H_D, -hi,
                       jnp.where(lane < 2 * H_D, pltpu.roll(-mid, H_D, axis=1),
                                 jnp.where(lane < BIAS_SELF_LANE, pltpu.roll(-lo, 2 * H_D, axis=1),
                                           jnp.where(lane < BIAS_SELF_LANE + BIAS_TERMS, 1.0, 0.0))))
    xk_high = roll_half(xk_low)
    for i in range(KV_D // 2):
        blk = slice(i * 2 * HD_D, (i + 1) * 2 * HD_D)
        kb = k_ref[0, :, blk]
        vb = v_ref[0, :, blk]
        kr = roll_half(kb)
        vr = roll_half(vb)
        for par in (0, 1):
            kv = 2 * i + par
            k_low, k_high = (kb, kr) if par == 0 else (kr, kb)
            k2_ref[0, 2 * kv] = jnp.where(low, k_low, xk_high).astype(BF16)
            k2_ref[0, 2 * kv + 1] = jnp.where(low, xk_low, k_high).astype(BF16)
            v_low, v_high = (vb, vr) if par == 0 else (vr, vb)
            v2_ref[0, 2 * kv] = jnp.where(low, v_low, jnp.where(lane == HD_D, 1.0, 0.0)).astype(BF16)
            v2_ref[0, 2 * kv + 1] = jnp.where(low, jnp.where(lane == 0, 1.0, 0.0), v_high).astype(BF16)
    for h in range(H_D):
        qp = q_ref[0, :, (h // 2) * 2 * HD_D:(h // 2 + 1) * 2 * HD_D] * (SCALE_D * LOG2E)
        own =jnp.logical_or(jnp.logical_or(lane == h, lane == H_D + h), lane == 2 * H_D + h)
        xq = jnp.where(own, 1.0,
                       jnp.where(lane == BIAS_SELF_LANE, hi[:, h:h + 1],
                                 jnp.where(lane == BIAS_SELF_LANE + 1, mid[:, h:h + 1],
                                           jnp.where(lane == BIAS_SELF_LANE + 2, lo[:, h:h + 1], 0.0))))
        qm = jnp.where(low, qp, roll_half(xq)) if h % 2 == 0 else jnp.where(low, xq, qp)
        qm_ref[0, h] = qm.astype(BF16)


def _fox_prep(q, k, v, fd, fbias_pad, T):
    bsz, seq, _ = q.shape
    kern = functools.partial(_fox_prep_kernel, T=T)
    tok = lambda w: pl.BlockSpec((1, T, w), lambda b, c: (b, c, 0))
    heads = lambda n: pl.BlockSpec((1, n, T, 2 * HD_D), lambda b, c: (b, 0, c, 0))
    return pl.pallas_call(
        kern,
        out_shape=[jax.ShapeDtypeStruct((bsz, seq, LANES), F32),
                   jax.ShapeDtypeStruct((bsz, H_D, seq, 2 * HD_D), BF16),
                   jax.ShapeDtypeStruct((bsz, 2 * KV_D, seq, 2 * HD_D), BF16),
                   jax.ShapeDtypeStruct((bsz, 2 * KV_D, seq, 2 * HD_D), BF16)],
        grid=(bsz, seq // T),
        in_specs=[tok(H_D * HD_D), tok(KV_D * HD_D), tok(KV_D * HD_D), tok(LANES),
                  pl.BlockSpec((1, LANES), lambda b, c: (0, 0))],
        out_specs=[tok(LANES), heads(H_D), heads(2 * KV_D), heads(2 * KV_D)],
        scratch_shapes=[pltpu.VMEM((1, LANES), F32)],
        compiler_params=_params("parallel", "arbitrary"),
        name="fox_prep",
    )(q, k, v, fd, fbias_pad)


def _fox_flash_kernel(qi_ref, ki_ref, qm_ref, k2_ref, v2_ref, o_ref, m_scr, acc_scr, *, TQ, TK, HG):
    s_id = pl.program_id(2)
    qi = qi_ref[s_id]
    ki = ki_ref[s_id]
    group = H_D // KV_D
    n_heads = H_D // HG

    @pl.when(ki == 0)
    def _():
        m_scr[...] = jnp.full_like(m_scr, NEG_INF)
        acc_scr[...] = jnp.zeros_like(acc_scr)

    def head_step(h, masked):
        kv = h // group
        s = _dot_nt(qm_ref[0, h], k2_ref[0, 2 * kv + h % 2])
        if masked:
            r = lax.broadcasted_iota(jnp.int32, (TQ, TK), 0)
            c = lax.broadcasted_iota(jnp.int32, (TQ, TK), 1)
            s = jnp.where(c - r <= qi * TQ - ki * TK, s, NEG_INF)
        m_prev = m_scr[h]
        m_new = jnp.maximum(m_prev, jnp.max(s, axis=-1, keepdims=True))
        alpha = jnp.exp2(m_prev - m_new)
        p = jnp.exp2(s - jnp.concatenate([m_new] * (TK // LANES), axis=1))
        acc_scr[h] = alpha * acc_scr[h] + _dot(p.astype(BF16), v2_ref[0, 2 * kv + h % 2])
        m_scr[h] = m_new

    def all_heads(masked):
        for h in range(n_heads):
            head_step(h, masked)

    ratio = TQ // TK
    @pl.when(ki < qi * ratio)
    def _():
        all_heads(False)

    @pl.when(ki >= qi * ratio)
    def _():
        all_heads(True)

    @pl.when(ki == (qi + 1) * ratio - 1)
    def _():
        lane = lax.broadcasted_iota(jnp.int32, (1, 2 * HD_D), 1)
        low = lane < HD_D
        for j in range(n_heads // 2):
            a0 = acc_scr[2 * j]
            a1 = acc_scr[2 * j + 1]
            o0 = a0 / a0[:, HD_D:HD_D + 1]
            o1 = a1 / a1[:, 0:1]
            o_ref[0, :, j * 2 * HD_D:(j + 1) * 2 * HD_D] = jnp.where(low, o0, o1)


def _fox_flash(qm, k2, v2, TQ, TK):
    bsz, _, seq, _ = qm.shape
    assert TQ % TK == 0
    nq = seq // TQ
    pairs = [(a, b) for a in range(nq) for b in range((a + 1) * (TQ // TK))]
    qi_tab = jnp.asarray([p[0] for p in pairs], jnp.int32)
    ki_tab = jnp.asarray([p[1] for p in pairs], jnp.int32)
    HG = FOX_HEAD_GROUPS
    nh, nkv = H_D // HG, KV_D // HG
    kern = functools.partial(_fox_flash_kernel, TQ=TQ, TK=TK, HG=HG)
    qw = H_D * HD_D
    grid_spec = pltpu.PrefetchScalarGridSpec(
        num_scalar_prefetch=2,
        grid=(bsz, HG, len(pairs)),
        in_specs=[pl.BlockSpec((1, nh, TQ, 2 * HD_D), lambda b, g, s, qt, kt: (b, g, qt[s], 0)),
                  pl.BlockSpec((1, 2 * nkv, TK, 2 * HD_D), lambda b, g, s, qt, kt: (b, g, kt[s], 0)),
                  pl.BlockSpec((1, 2 * nkv, TK, 2 * HD_D), lambda b, g, s, qt, kt: (b, g, kt[s], 0))],
        out_specs=pl.BlockSpec((1, TQ, qw // HG), lambda b, g, s, qt, kt: (b, qt[s], g)),
        scratch_shapes=[pltpu.VMEM((nh, TQ, LANES), F32),
                        pltpu.VMEM((nh, TQ, 2 * HD_D), F32)])
    return pl.pallas_call(
        kern,
        out_shape=jax.ShapeDtypeStruct((bsz, seq, qw), F32),
        grid_spec=grid_spec,
        compiler_params=_params("parallel", "parallel", "arbitrary"),
        name="fox_flash",
    )(qi_tab, ki_tab, qm, k2, v2)


def _fox_paged_kernel(pt_ref, q_ref, kn_ref, vn_ref, fdn_ref, fb_ref, kt_hbm, vt_hbm, lft_hbm, o_ref, lfn_ref,
                      kbuf, vbuf, lbuf, sem, qall_scr, m_scr, l_scr, acc_scr, carry_scr,
                      *, SEQS, PP, NG, base, n_pages):
    b = pl.program_id(0)
    g = pl.program_id(1)
    n_steps = pl.num_programs(0) * NG
    t = b * NG + g
    slot = t % 2
    L = q_ref.shape[1]
    rows = H_D * L
    group = H_D // KV_D
    W = PP * PAGE_SIZE

    def page_copies(bb, gg, sl):
        cps = []
        for sq in range(SEQS):
            for i in range(PP):
                pg = base + pt_ref[bb * SEQS + sq, n_pages - 1 - (gg * PP + i)]
                dst = sq * PP + i
                cps.append(pltpu.make_async_copy(kt_hbm.at[pg], kbuf.at[sl, dst], sem.at[0, sl]))
                cps.append(pltpu.make_async_copy(vt_hbm.at[pg], vbuf.at[sl, dst], sem.at[1, sl]))
                cps.append(pltpu.make_async_copy(lft_hbm.at[pg], lbuf.at[sl, dst], sem.at[2, sl]))
        return cps

    @pl.when(t == 0)
    def _():
        for cp in page_copies(b, g, slot):
            cp.start()

    @pl.when(t + 1 < n_steps)
    def _():
        t1 = t + 1
        for cp in page_copies(t1 // NG, t1 % NG, 1 - slot):
            cp.start()

    @pl.when(g == 0)
    def _():
        for sq in range(SEQS):
            q = q_ref[sq] * SCALE_D
            for h in range(H_D):
                kv = h // group
                piece = q[:, h * HD_D:(h + 1) * HD_D]
                parts = []
                if kv > 0:
                    parts.append(jnp.zeros((L, kv * HD_D), F32))
                parts.append(piece)
                if kv < KV_D - 1:
                    parts.append(jnp.zeros((L, (KV_D - 1 - kv) * HD_D), F32))
                qall_scr[sq, h * L:(h + 1) * L, :] = jnp.concatenate(parts, axis=1).astype(BF16)
            lfn = -_softplus(-(fdn_ref[sq] + fb_ref[...]))
            lfn_ref[sq] = lfn
            cumn = _exact_dot_rhs(_lower_tri_bf16(L), lfn)
            cumn_t = cumn.T[0:H_D, :]
            bias = jnp.broadcast_to((-cumn_t)[:, None, :], (H_D, L, L)).reshape(rows, L)
            s = _dot_nt(qall_scr[sq], kn_ref[sq].astype(BF16)) + bias
            qidx = lax.broadcasted_iota(jnp.int32, (rows, L), 0) % L
            kidx = lax.broadcasted_iota(jnp.int32, (rows, L), 1)
            s = jnp.where(kidx <= qidx, s, NEG_INF)
            mx = jnp.max(s, axis=-1, keepdims=True)
            p = jnp.exp(s - mx)
            m_scr[sq] = mx
            l_scr[sq] = jnp.sum(p, axis=-1, keepdims=True)
            acc_scr[sq] = _dot(p.astype(BF16), vn_ref[sq].astype(BF16))
        carry_scr[...] = jnp.zeros_like(carry_scr)

    for cp in page_copies(b, g, slot):
        cp.wait()
    lane_in_page = lax.broadcasted_iota(jnp.int32, (1, W), 1) % PAGE_SIZE
    for sq in range(SEQS):
        pages = range(sq * PP, (sq + 1) * PP)
        kcat = jnp.concatenate([kbuf[slot, i] for i in pages], axis=1).astype(BF16)
        vcat = jnp.concatenate([vbuf[slot, i] for i in pages], axis=1).astype(BF16)
        s = _dot(qall_scr[sq], kcat)
        lft = jnp.concatenate([lbuf[slot, i] for i in pages], axis=1)
        y = lft
        step = 1
        while step < PAGE_SIZE:
            y = y + jnp.where(lane_in_page < PAGE_SIZE - step, pltpu.roll(y, W - step, axis=1), 0.0)
            step *= 2
        d_local = y - lft
        carry = carry_scr[sq]
        pieces = []
        for i in range(PP):
            pieces.append(d_local[:, i * PAGE_SIZE:(i + 1) * PAGE_SIZE] + carry)
            carry = carry + y[:, i * PAGE_SIZE:i * PAGE_SIZE + 1]
        carry_scr[sq] = carry
        bias_t = jnp.concatenate(pieces, axis=1)
        s = s + jnp.broadcast_to(bias_t[:, None, :], (H_D, L, W)).reshape(rows, W)
        m_prev = m_scr[sq]
        m_new = jnp.maximum(m_prev, jnp.max(s, axis=-1, keepdims=True))
        alpha = jnp.exp(m_prev - m_new)
        p = jnp.exp(s - m_new)
        l_scr[sq] = alpha * l_scr[sq] + jnp.sum(p, axis=-1, keepdims=True)
        acc_scr[sq] = alpha * acc_scr[sq] + _dot_nt(p.astype(BF16), vcat)
        m_scr[sq] = m_new

    @pl.when(g == NG - 1)
    def _():
        for sq in range(SEQS):
            o = acc_scr[sq] / l_scr[sq]
            parts = []
            for h in range(H_D):
                kv = h // group
                parts.append(o[h * L:(h + 1) * L, kv * HD_D:(kv + 1) * HD_D])
            o_ref[sq] = jnp.concatenate(parts, axis=1)


def _fox_paged(q, k_new, v_new, fd_new, fbias_pad, pool_k, pool_v, pool_lf, page_table, layer, PP):
    bsz, L, qw = q.shape
    n_pool = pool_k.shape[1]
    n_pages = page_table.shape[1]
    NG = n_pages // PP
    kvw = KV_D * HD_D
    pkt = jnp.transpose(pool_k, (0, 1, 3, 4, 2)).reshape(pool_k.shape[0] * n_pool, kvw, PAGE_SIZE)
    pvt = jnp.transpose(pool_v, (0, 1, 3, 4, 2)).reshape(pool_v.shape[0] * n_pool, kvw, PAGE_SIZE)
    plft = jnp.transpose(pool_lf, (0, 1, 3, 2)).reshape(pool_lf.shape[0] * n_pool, H_D, PAGE_SIZE)
    base = layer * n_pool

    SEQS = min(FOX_SEQS_PER_STEP, bsz)
    seq_map = lambda b, g, pt: (b, 0, 0)
    in_specs = [pl.BlockSpec((SEQS, L, qw), seq_map),
                pl.BlockSpec((SEQS, L, kvw), seq_map),
                pl.BlockSpec((SEQS, L, kvw), seq_map),
                pl.BlockSpec((SEQS, L, LANES), seq_map),
                pl.BlockSpec((1, LANES), lambda b, g, pt: (0, 0)),
                pl.BlockSpec(memory_space=pl.ANY),
                pl.BlockSpec(memory_space=pl.ANY),
                pl.BlockSpec(memory_space=pl.ANY)]
    kern = functools.partial(_fox_paged_kernel, SEQS=SEQS, PP=PP, NG=NG, base=base, n_pages=n_pages)
    grid_spec = pltpu.PrefetchScalarGridSpec(
        num_scalar_prefetch=1,
        grid=(bsz // SEQS, NG),
        in_specs=in_specs,
        out_specs=[pl.BlockSpec((SEQS, L, qw), seq_map), pl.BlockSpec((SEQS, L, LANES), seq_map)],
        scratch_shapes=[pltpu.VMEM((2, SEQS * PP, kvw, PAGE_SIZE), F32),
                        pltpu.VMEM((2, SEQS * PP, kvw, PAGE_SIZE), F32),
                        pltpu.VMEM((2, SEQS * PP, H_D, PAGE_SIZE), F32),
                        pltpu.SemaphoreType.DMA((3, 2)),
                        pltpu.VMEM((SEQS, H_D * L, kvw), BF16),
                        pltpu.VMEM((SEQS, H_D * L, 1), F32),
                        pltpu.VMEM((SEQS, H_D * L, 1), F32),
                        pltpu.VMEM((SEQS, H_D * L, kvw), F32),
                        pltpu.VMEM((SEQS, H_D, 1), F32)])
    return pl.pallas_call(
        kern,
        out_shape=[jax.ShapeDtypeStruct((bsz, L, qw), F32), jax.ShapeDtypeStruct((bsz, L, LANES), F32)],
        grid_spec=grid_spec,
        compiler_params=_params("arbitrary", "arbitrary"),
        name="fox_paged",
    )(page_table, q, k_new, v_new, fd_new, fbias_pad, pkt, pvt, plft)


def _pad_cols(w, n):
    return jnp.pad(w, ((0, 0), (0, n - w.shape[1])))


def _prep_w_ab(w):
    z, xbc, dt, q, k, v = jnp.split(w, np.cumsum((D_INNER_A, CONV_DIM_A, H_A, H_B * HD_B, KV_B * HD_B, KV_B * HD_B))[:-1].tolist(), axis=1)
    return jnp.concatenate([z, xbc, q, k, v, _pad_cols(dt, LANES)], axis=1).astype(BF16)


AB_SIZES = (D_INNER_A, CONV_DIM_A, H_B * HD_B, KV_B * HD_B, KV_B * HD_B, LANES)


def _prep_w_cd(w):
    sizes = (H_C * DK_C, H_C * DK_C, H_C * DV_C, H_C * DV_C, H_D * HD_D, KV_D * HD_D, KV_D * HD_D, H_D)
    parts = jnp.split(w, np.cumsum(sizes)[:-1].tolist(), axis=1)
    parts[-1] = _pad_cols(parts[-1], LANES)
    return jnp.concatenate(parts, axis=1).astype(BF16)


CD_SIZES = (H_C * DK_C, H_C * DK_C, H_C * DV_C, H_C * DV_C, H_D * HD_D, KV_D * HD_D, KV_D * HD_D, LANES)


def _trunk(x, c, st, prm, page_table):
    sample = page_table is not None
    bsz, seq, _ = x.shape
    if sample:
        bb, tl, out_tl, ffn_tl = min(SAMPLE_BATCH_TILE, bsz), seq, seq, seq
    else:
        bb, tl, out_tl, ffn_tl = 1, min(ROW_TILE, seq), min(OUT_ROW_TILE, seq), min(FFN_ROW_TILE, seq)
    n_sub = DEPTH * 2
    m_all = _ada_all(c, prm['ada_w'].reshape(n_sub, D_MODEL, 3 * D_MODEL), prm['ada_b'].reshape(n_sub, 1, 3 * D_MODEL))
    new = {}
    for l in range(DEPTH):
        j = l // 2
        m_mix = m_all[2 * l][:, None, :]
        m_ffn = m_all[2 * l + 1][:, None, :]
        if l % 2 == 0:
            z, xbc, qb, kb, vb, dtp = _mod_matmul(x, m_mix, _prep_w_ab(prm['w_in_ab'][j]), AB_SIZES, bb, tl)
            if sample:
                conv0p = jnp.pad(st['conv_a'][j], ((0, 0), (SUBLANES - (CONV_A - 1), 0), (0, 0)))
                ssm0 = st['ssm_a'][j]
            else:
                conv0p, ssm0 = None, None
            ya, conv1p, ssm1 = _mamba(z, xbc, dtp, conv0p, ssm0, prm['conv_w_a'][j], prm['conv_b_a'][j],
                                      prm['dt_bias_a'][j], prm['a_log_a'][j], prm['d_skip_a'][j],
                                      prm['norm_w_a'][j], T=min(SSD_CHUNK, seq))
            if sample:
                kbuf = st['swa_k'][j].reshape(bsz, WINDOW, KV_B * HD_B)
                vbuf = st['swa_v'][j].reshape(bsz, WINDOW, KV_B * HD_B)
                yb = _swa(qb, kbuf, kb, vbuf, vb, prm['sinks_b'][j], TQ=seq, SB=min(SWA_SAMPLE_SEQS, bsz),
                          prev_is_same_array=False)
                bk = jnp.concatenate([kbuf[:, seq:], kb], axis=1)
                bv = jnp.concatenate([vbuf[:, seq:], vb], axis=1)
            else:
                yb = _swa(qb, kb, kb, vb, vb, prm['sinks_b'][j], TQ=WINDOW, SB=1, prev_is_same_array=True)
                bk, bv = kb[:, -WINDOW:], vb[:, -WINDOW:]
            w_out = prm['w_out_ab'][j].astype(BF16)
            x = _mm_res_ln([ya, yb], [w_out[:D_INNER_A], w_out[D_INNER_A:]], x, m_mix,
                           prm['ln_g'][l, 0][None, :], prm['ln_b'][l, 0][None, :], bb, out_tl)
            new.setdefault('ssm_a', []).append(ssm1)
            new.setdefault('conv_a', []).append(conv1p[:, SUBLANES - (CONV_A - 1):, :])
            new.setdefault('swa_k', []).append(bk.reshape(bsz, WINDOW, KV_B, HD_B))
            new.setdefault('swa_v', []).append(bv.reshape(bsz, WINDOW, KV_B, HD_B))
        else:
            qc, fc, ic, gc, qd, kd, vd, fdp = _mod_matmul(x, m_mix, _prep_w_cd(prm['w_in_cd'][j]), CD_SIZES, bb, tl)
            s0 = st['hgrn_c'][j] if sample else None
            yc, s1 = _gla(qc, fc, ic, gc, prm['lb_c'], prm['gnorm_c'][j], s0, layer=l, T=min(GLA_CHUNK, seq))
            fbias_pad = jnp.zeros((1, LANES), F32).at[0, :H_D].set(prm['fbias_d'][j])
            if sample:
                yd, lfp = _fox_paged(qd, kd, vd, fdp, fbias_pad, st['fox_k'], st['fox_v'], st['fox_logf'],
                                     page_table, j, FOX_PAGES_PER_STEP)
            else:
                lfp, qm, k2, v2 = _fox_prep(qd, kd, vd, fdp, fbias_pad, min(FOX_PREP_T, seq))
                yd = _fox_flash(qm, k2, v2, min(FOX_TQ, seq), min(FOX_TK, seq))
            w_out = prm['w_out_cd'][j].astype(BF16)
            x = _mm_res_ln([yc, yd], [w_out[:H_C * DV_C], w_out[H_C * DV_C:]], x, m_mix,
                           prm['ln_g'][l, 0][None, :], prm['ln_b'][l, 0][None, :], bb, out_tl)
            new.setdefault('hgrn_c', []).append(s1)
            new.setdefault('fox_k', []).append(kd.reshape(bsz, seq, KV_D, HD_D))
            new.setdefault('fox_v', []).append(vd.reshape(bsz, seq, KV_D, HD_D))
            new.setdefault('fox_logf', []).append(lfp[:, :, :H_D])
        buf0 = st['ffn_conv'][l] if sample else jnp.zeros((bsz, FFN_CONV - 1, D_FF), F32)
        a, buf1 = _ffn_in(x, m_ffn, prm['ffn_w_in'][l].astype(BF16), buf0, prm['ffn_conv_w'][l],
                          prm['ffn_conv_b'][l][None, :], bb, ffn_tl)
        x = _mm_res_ln([a], [prm['ffn_w_out'][l].astype(BF16)], x, m_ffn,
                       prm['ln_g'][l, 1][None, :], prm['ln_b'][l, 1][None, :], bb, ffn_tl)
        new.setdefault('ffn_conv', []).append(buf1)
    return x, {name: jnp.stack(rows, axis=0) for name, rows in new.items()}


def kernel(x_prompt, x_sample, state_ssm_a, state_conv_a, cache_swa_k, cache_swa_v, state_hgrn_c, cache_fox_k, cache_fox_v, cache_fox_logf, state_ffn_conv, page_table, c_prompt, c_sample, ada_w, ada_b, ln_g, ln_b, w_in_ab, w_out_ab, conv_w_a, conv_b_a, dt_bias_a, a_log_a, d_skip_a, norm_w_a, sinks_b, w_in_cd, w_out_cd, lb_c, gnorm_c, fbias_d, ffn_w_in, ffn_conv_w, ffn_conv_b, ffn_w_out):
    prm = dict(ada_w=ada_w, ada_b=ada_b, ln_g=ln_g, ln_b=ln_b, w_in_ab=w_in_ab, w_out_ab=w_out_ab,
               conv_w_a=conv_w_a, conv_b_a=conv_b_a, dt_bias_a=dt_bias_a, a_log_a=a_log_a,
               d_skip_a=d_skip_a, norm_w_a=norm_w_a, sinks_b=sinks_b, w_in_cd=w_in_cd, w_out_cd=w_out_cd,
               lb_c=lb_c, gnorm_c=gnorm_c, fbias_d=fbias_d, ffn_w_in=ffn_w_in, ffn_conv_w=ffn_conv_w,
               ffn_conv_b=ffn_conv_b, ffn_w_out=ffn_w_out)
    st = dict(ssm_a=state_ssm_a, conv_a=state_conv_a, swa_k=cache_swa_k, swa_v=cache_swa_v,
              hgrn_c=state_hgrn_c, fox_k=cache_fox_k, fox_v=cache_fox_v, fox_logf=cache_fox_logf,
              ffn_conv=state_ffn_conv)
    y_p, sp = _trunk(x_prompt, c_prompt, None, prm, None)
    y_s, ss = _trunk(x_sample, c_sample, st, prm, page_table)
    return (y_p, y_s,
            sp['ssm_a'], ss['ssm_a'], sp['conv_a'], ss['conv_a'],
            sp['swa_k'], ss['swa_k'], sp['swa_v'], ss['swa_v'],
            sp['hgrn_c'], ss['hgrn_c'],
            sp['fox_k'], ss['fox_k'], sp['fox_v'], ss['fox_v'], sp['fox_logf'], ss['fox_logf'],
            sp['ffn_conv'], ss['ffn_conv'])
```

```python
import functools
import math

import numpy as np
import jax
import jax.numpy as jnp
from jax import lax
from jax.experimental import pallas as pl
from jax.experimental.pallas import tpu as pltpu

F32 = jnp.float32
BF16 = jnp.bfloat16
NEG_INF = float("-inf")

D_MODEL = 1024
DEPTH = 2
PAGE_SIZE = 128
H_A, P_A, N_A, G_A, CONV_A = 16, 64, 128, 2, 4
D_INNER_A = H_A * P_A
CONV_DIM_A = D_INNER_A + 2 * G_A * N_A
H_B, KV_B, HD_B, WINDOW = 8, 2, 64, 128
H_C, DK_C, DV_C = 4, 128, 128
H_D, KV_D, HD_D = 16, 4, 64
D_FF, FFN_CONV = 2816, 3
ALPHA = (2 * DEPTH) ** 0.25
SCALE_B = HD_B ** -0.5
SCALE_D = HD_D ** -0.5
LN_EPS = 1e-5
RMS_EPS = 1e-6
LOG2E = math.log2(math.e)

LANES = 128
SUBLANES = 8
VMEM_LIMIT_BYTES = 56 * 1024 * 1024

SSD_CHUNK = 128
GLA_CHUNK = 256
GLA_BLOCK = 16
FOX_TQ = 1024
FOX_TK = 1024
FOX_HEAD_GROUPS = 2
FOX_PREP_T = 512
FOX_PAGES_PER_STEP = 16
ROW_TILE = 512
OUT_ROW_TILE = 1024
FFN_ROW_TILE = 512
SAMPLE_BATCH_TILE = 32
SWA_SAMPLE_SEQS = 8
MAMBA_SAMPLE_SEQS = 4
GLA_SAMPLE_SEQS = 4
FOX_SEQS_PER_STEP = 2


def _params(*sem):
    return pltpu.CompilerParams(dimension_semantics=sem, vmem_limit_bytes=VMEM_LIMIT_BYTES)


def _resident(shape):
    nd = len(shape)
    return pl.BlockSpec(shape, lambda *_: (0,) * nd, pipeline_mode=pl.Buffered(1))


def _silu(x):
    return x * jax.nn.sigmoid(x)


def _softplus(x):
    return jnp.maximum(x, 0.0) + jnp.log1p(jnp.exp(-jnp.abs(x)))


def _dot(a, b):
    return jnp.dot(a, b, preferred_element_type=F32)


def _dot_nt(a, b):
    return lax.dot_general(a, b, (((1,), (1,)), ((), ())), preferred_element_type=F32)


def _dot_tn(a, b):
    return lax.dot_general(a, b, (((0,), (0,)), ((), ())), preferred_element_type=F32)


def _split3(x):
    hi = x.astype(BF16)
    r = x - hi.astype(F32)
    mid = r.astype(BF16)
    lo = (r - mid.astype(F32)).astype(BF16)
    return hi, mid, lo


def _exact_dot_lhs(x, w01):
    hi, mid, lo = _split3(x)
    return _dot(hi, w01) + _dot(mid, w01) + _dot(lo, w01)


def _exact_dot_rhs(w01, x):
    hi, mid, lo = _split3(x)
    return _dot(w01, hi) + _dot(w01, mid) + _dot(w01, lo)


def _lower_tri_bf16(t):
    r = lax.broadcasted_iota(jnp.int32, (t, t), 0)
    c = lax.broadcasted_iota(jnp.int32, (t, t), 1)
    return jnp.where(r >= c, 1.0, 0.0).astype(BF16)


def _layer_norm_rows(r, g, b):
    mu = jnp.mean(r, axis=-1, keepdims=True)
    d = r - mu
    var = jnp.mean(d * d, axis=-1, keepdims=True)
    return d * lax.rsqrt(var + LN_EPS) * g + b


def _ada_kernel(c_ref, w_ref, b_ref, o_ref):
    s = _silu(c_ref[...]).astype(BF16)
    o_ref[0] = _dot(s, w_ref[0].astype(BF16)) + b_ref[0]


def _ada_all(c, ada_w, ada_b):
    bsz = c.shape[0]
    n_sub = ada_w.shape[0]
    n_out = ada_w.shape[2]
    tn = D_MODEL
    return pl.pallas_call(
        _ada_kernel,
        out_shape=jax.ShapeDtypeStruct((n_sub, bsz, n_out), F32),
        grid=(n_sub, n_out // tn),
        in_specs=[pl.BlockSpec((bsz, D_MODEL), lambda s, j: (0, 0)),
                  pl.BlockSpec((1, D_MODEL, tn), lambda s, j: (s, 0, j)),
                  pl.BlockSpec((1, 1, tn), lambda s, j: (s, 0, j))],
        out_specs=pl.BlockSpec((1, bsz, tn), lambda s, j: (s, 0, j)),
        compiler_params=_params("parallel", "parallel"),
        name="ada_mod",
    )(c, ada_w, ada_b)


def _mod_matmul_kernel(x_ref, m_ref, w_ref, *o_refs, bb, tl, sizes):
    d = x_ref.shape[-1]
    m = m_ref[...]
    h = x_ref[...] * (1.0 + m[:, :, d:2 * d]) + m[:, :, 0:d]
    h2 = h.reshape(bb * tl, d).astype(BF16)
    off = 0
    for o_ref, n in zip(o_refs, sizes):
        o_ref[...] = _dot(h2, w_ref[:, off:off + n]).reshape(bb, tl, n)
        off += n


def _mod_matmul(x, m, w_bf16, sizes, bb, tl):
    bsz, seq, d = x.shape
    n_tot = w_bf16.shape[1]
    kern = functools.partial(_mod_matmul_kernel, bb=bb, tl=tl, sizes=tuple(sizes))
    return pl.pallas_call(
        kern,
        out_shape=[jax.ShapeDtypeStruct((bsz, seq, n), F32) for n in sizes],
        grid=(bsz // bb, seq // tl),
        in_specs=[pl.BlockSpec((bb, tl, d), lambda i, j: (i, j, 0)),
                  pl.BlockSpec((bb, 1, 3 * d), lambda i, j: (i, 0, 0)),
                  _resident((d, n_tot))],
        out_specs=[pl.BlockSpec((bb, tl, n), lambda i, j: (i, j, 0)) for n in sizes],
        compiler_params=_params("parallel", "parallel"),
        name="mod_matmul",
    )(x, m, w_bf16)


def _mm_res_ln_kernel(*refs, bb, tl, n_in):
    y_refs = refs[0:n_in]
    w_refs = refs[n_in:2 * n_in]
    x_ref, m_ref, g_ref, b_ref, o_ref = refs[2 * n_in:]
    d = x_ref.shape[-1]
    mix = None
    for y_ref, w_ref in zip(y_refs, w_refs):
        y2 = y_ref[...].reshape(bb * tl, y_ref.shape[-1]).astype(BF16)
        part = _dot(y2, w_ref[...])
        mix = part if mix is None else mix + part
    gate = m_ref[:, :, 2 * d:3 * d]
    r = ALPHA * x_ref[...] + (1.0 + gate) * mix.reshape(bb, tl, d)
    o_ref[...] = _layer_norm_rows(r, g_ref[...], b_ref[...])


def _mm_res_ln(ys, ws_bf16, x, m, g, b, bb, tl):
    bsz, seq, d = x.shape
    kern = functools.partial(_mm_res_ln_kernel, bb=bb, tl=tl, n_in=len(ys))
    return pl.pallas_call(
        kern,
        out_shape=jax.ShapeDtypeStruct((bsz, seq, d), F32),
        grid=(bsz // bb, seq // tl),
        in_specs=([pl.BlockSpec((bb, tl, y.shape[-1]), lambda i, j: (i, j, 0)) for y in ys]
                  + [_resident(w.shape) for w in ws_bf16]
                  + [pl.BlockSpec((bb, tl, d), lambda i, j: (i, j, 0)),
                     pl.BlockSpec((bb, 1, 3 * d), lambda i, j: (i, 0, 0)),
                     pl.BlockSpec((1, d), lambda i, j: (0, 0)),
                     pl.BlockSpec((1, d), lambda i, j: (0, 0))]),
        out_specs=pl.BlockSpec((bb, tl, d), lambda i, j: (i, j, 0)),
        compiler_params=_params("parallel", "parallel"),
        name="mm_res_ln",
    )(*ys, *ws_bf16, x, m, g, b)


def _ffn_in_kernel(x_ref, m_ref, w_ref, buf0_ref, cw_ref, cb_ref, a_ref, buf1_ref, carry_scr, *, bb, tl, n_col):
    d = x_ref.shape[-1]
    dff = a_ref.shape[-1]
    j = pl.program_id(1)

    @pl.when(j == 0)
    def _():
        carry_scr[...] = buf0_ref[...]

    m = m_ref[...]
    h = x_ref[...] * (1.0 + m[:, :, d:2 * d]) + m[:, :, 0:d]
    h2 = h.reshape(bb * tl, d).astype(BF16)
    rows = bb * tl
    cw = dff // n_col
    t = lax.broadcasted_iota(jnp.int32, (rows, 1), 0) % tl
    for ci in range(n_col):
        lo = ci * cw
        u = _dot(h2, w_ref[:, lo:lo + cw])
        g = _dot(h2, w_ref[:, dff + lo:dff + lo + cw])
        carry = carry_scr[:, :, lo:lo + cw]
        prev1 = jnp.broadcast_to(carry[:, 1:2, :], (bb, tl, cw)).reshape(rows, cw)
        prev0 = jnp.broadcast_to(carry[:, 0:1, :], (bb, tl, cw)).reshape(rows, cw)
        tap1 = jnp.where(t == 0, prev1, pltpu.roll(g, 1, axis=0))
        tap2 = jnp.where(t == 0, prev0, jnp.where(t == 1, prev1, pltpu.roll(g, 2, axis=0)))
        y = cb_ref[:, lo:lo + cw] + g * cw_ref[2:3, lo:lo + cw]
        y = y + tap2 * cw_ref[0:1, lo:lo + cw]
        y = y + tap1 * cw_ref[1:2, lo:lo + cw]
        a_ref[:, :, lo:lo + cw] = (_silu(y) * u).reshape(bb, tl, cw)
        last2 = g.reshape(bb, tl, cw)[:, tl - (FFN_CONV - 1):, :]
        carry_scr[:, :, lo:lo + cw] = last2
        buf1_ref[:, :, lo:lo + cw] = last2


def _ffn_in(x, m, w_bf16, buf0, conv_w, conv_b, bb, tl):
    bsz, seq, d = x.shape
    kern = functools.partial(_ffn_in_kernel, bb=bb, tl=tl, n_col=2)
    return pl.pallas_call(
        kern,
        out_shape=[jax.ShapeDtypeStruct((bsz, seq, D_FF), F32),
                   jax.ShapeDtypeStruct((bsz, FFN_CONV - 1, D_FF), F32)],
        grid=(bsz // bb, seq // tl),
        in_specs=[pl.BlockSpec((bb, tl, d), lambda i, j: (i, j, 0)),
                  pl.BlockSpec((bb, 1, 3 * d), lambda i, j: (i, 0, 0)),
                  _resident((d, 2 * D_FF)),
                  pl.BlockSpec((bb, FFN_CONV - 1, D_FF), lambda i, j: (i, 0, 0)),
                  pl.BlockSpec((FFN_CONV, D_FF), lambda i, j: (0, 0)),
                  pl.BlockSpec((1, D_FF), lambda i, j: (0, 0))],
        out_specs=[pl.BlockSpec((bb, tl, D_FF), lambda i, j: (i, j, 0)),
                   pl.BlockSpec((bb, FFN_CONV - 1, D_FF), lambda i, j: (i, 0, 0))],
        scratch_shapes=[pltpu.VMEM((bb, FFN_CONV - 1, D_FF), F32)],
        compiler_params=_params("parallel", "arbitrary"),
        name="ffn_in_conv",
    )(x, m, w_bf16, buf0, conv_w, conv_b)


def _mamba_kernel(*refs, SB, T, has_state, n_chunks):
    if has_state:
        (z_ref, xbc_ref, dt_ref, conv0_ref, ssm0_ref, cw_ref, cb_ref, dtb_ref, alog_ref, dsk_ref, nw_ref, e_ref,
         y_ref, conv1_ref, ssm1_ref, carry_scr, s_scr) = refs
    else:
        (z_ref, xbc_ref, dt_ref, cw_ref, cb_ref, dtb_ref, alog_ref, dsk_ref, nw_ref, e_ref,
         y_ref, conv1_ref, ssm1_ref, carry_scr, s_scr) = refs
    c = pl.program_id(1)
    n_pair = H_A // 2
    cdim = CONV_DIM_A

    @pl.when(c == 0)
    def _():
        if has_state:
            carry_scr[...] = conv0_ref[...]
            for sb in range(SB):
                for j in range(n_pair):
                    s_scr[sb * n_pair + j] = jnp.concatenate([ssm0_ref[sb, 2 * j], ssm0_ref[sb, 2 * j + 1]], axis=0)
        else:
            carry_scr[...] = jnp.zeros_like(carry_scr)
            s_scr[...] = jnp.zeros_like(s_scr)

    row8 = lax.broadcasted_iota(jnp.int32, (SUBLANES, cdim), 0)
    tri = _lower_tri_bf16(T)
    e = e_ref[...]
    rr = lax.broadcasted_iota(jnp.int32, (T, T), 0)
    cc = lax.broadcasted_iota(jnp.int32, (T, T), 1)
    causal = rr >= cc
    lane = lax.broadcasted_iota(jnp.int32, (1, 2 * P_A), 1)
    low = lane < P_A
    first_of_pair = lax.broadcasted_iota(jnp.int32, (2 * P_A, 1), 0) < P_A
    heads_per_group = H_A // G_A

    for sb in range(SB):
        xb = xbc_ref[sb]
        carry = carry_scr[sb]
        acc = cb_ref[...] + xb * cw_ref[CONV_A - 1:CONV_A, :]
        for dshift in (3, 2, 1):
            rolled = pltpu.roll(xb, dshift, axis=0)
            top = jnp.where(row8 < dshift, pltpu.roll(carry, dshift, axis=0), rolled[0:SUBLANES])
            sh = top if T == SUBLANES else jnp.concatenate([top, rolled[SUBLANES:]], axis=0)
            acc = acc + sh * cw_ref[CONV_A - 1 - dshift:CONV_A - dshift, :]
        last8 = xb[T - SUBLANES:T, :]
        carry_scr[sb] = last8
        conv1_ref[sb] = last8
        xc = _silu(acc)
        xs = xc[:, 0:D_INNER_A]
        bm = xc[:, D_INNER_A:D_INNER_A + G_A * N_A]
        cm = xc[:, D_INNER_A + G_A * N_A:cdim]

        dt = _softplus(dt_ref[sb] + dtb_ref[...])
        la = dt * (-jnp.exp(alog_ref[...]))
        cum = _exact_dot_rhs(tri, la)
        dtx = _exact_dot_lhs(dt, e)
        cumx = _exact_dot_lhs(cum, e)
        x = xs * dtx
        decx = jnp.exp(cumx)
        xd = (x * jnp.exp(cumx[T - 1:T, :] - cumx)).astype(BF16)
        cum_t = cum.T

        cb_g, c_g, b_g = [], [], []
        for g in range(G_A):
            cg = cm[:, g * N_A:(g + 1) * N_A].astype(BF16)
            bg = bm[:, g * N_A:(g + 1) * N_A].astype(BF16)
            c_g.append(cg)
            b_g.append(bg)
            cb_g.append(_dot_nt(cg, bg))

        outs = []
        for j in range(n_pair):
            g = (2 * j) // heads_per_group
            sl = slice(j * 2 * P_A, (j + 1) * 2 * P_A)
            xj = x[:, sl]
            yj = None
            for eidx in (0, 1):
                h = 2 * j + eidx
                col = cum[:, h:h + 1]
                row = cum_t[h:h + 1, :]
                decay = jnp.exp(jnp.where(causal, col - row, NEG_INF))
                scores = (cb_g[g] * decay).astype(BF16)
                xm = jnp.where(low if eidx == 0 else jnp.logical_not(low), xj, 0.0).astype(BF16)
                part = _dot(scores, xm)
                yj = part if yj is None else yj + part
            s_old = s_scr[sb * n_pair + j]
            yj = yj + _dot_nt(c_g[g], s_old.astype(BF16)) * decx[:, sl]
            e_last = jnp.where(first_of_pair, jnp.exp(cum_t[2 * j:2 * j + 1, T - 1:T]),
                               jnp.exp(cum_t[2 * j + 1:2 * j + 2, T - 1:T]))
            s_scr[sb * n_pair + j] = e_last * s_old + _dot_tn(xd[:, sl], b_g[g])
            outs.append(yj)
        y = jnp.concatenate(outs, axis=1) + dsk_ref[...] * xs
        y = y * _silu(z_ref[sb])
        y_ref[sb] = y * lax.rsqrt(jnp.mean(y * y, axis=-1, keepdims=True) + RMS_EPS) * nw_ref[...]

    @pl.when(c == n_chunks - 1)
    def _():
        for sb in range(SB):
            for j in range(n_pair):
                sj = s_scr[sb * n_pair + j]
                ssm1_ref[sb, 2 * j] = sj[0:P_A, :]
                ssm1_ref[sb, 2 * j + 1] = sj[P_A:2 * P_A, :]


def _mamba(z, xbc, dtp, conv0p, ssm0, conv_w, conv_b, dt_bias, a_log, d_skip, norm_w, T):
    bsz, seq, _ = z.shape
    n_chunks = seq // T
    has_state = conv0p is not None
    dtb = jnp.zeros((1, LANES), F32).at[0, :H_A].set(dt_bias)
    alog = jnp.zeros((1, LANES), F32).at[0, :H_A].set(a_log)
    dsk = jnp.repeat(d_skip, P_A)[None, :]
    head_of_col = np.arange(D_INNER_A) // P_A
    expand = jnp.asarray((np.arange(LANES)[:, None] == head_of_col[None, :]).astype(np.float32), dtype=BF16)
    SB = min(MAMBA_SAMPLE_SEQS, bsz) if has_state else 1
    kern = functools.partial(_mamba_kernel, SB=SB, T=T, has_state=has_state, n_chunks=n_chunks)
    tok = lambda w: pl.BlockSpec((SB, T, w), lambda b, c: (b, c, 0))
    const = lambda shape: pl.BlockSpec(shape, lambda b, c: (0,) * len(shape))
    conv_spec = pl.BlockSpec((SB, SUBLANES, CONV_DIM_A), lambda b, c: (b, 0, 0))
    ssm_spec = pl.BlockSpec((SB, H_A, P_A, N_A), lambda b, c: (b, 0, 0, 0))
    in_specs = [tok(D_INNER_A), tok(CONV_DIM_A), tok(LANES)]
    args = [z, xbc, dtp]
    if has_state:
        in_specs += [conv_spec, ssm_spec]
        args += [conv0p, jnp.swapaxes(ssm0, 2, 3)]
    in_specs += [const((CONV_A, CONV_DIM_A)), const((1, CONV_DIM_A)), const((1, LANES)), const((1, LANES)),
                 const((1, D_INNER_A)), const((1, D_INNER_A)), const((LANES, D_INNER_A))]
    args += [conv_w, conv_b[None, :], dtb, alog, dsk, norm_w[None, :], expand]
    y, conv1p, ssm1_t = pl.pallas_call(
        kern,
        out_shape=[jax.ShapeDtypeStruct((bsz, seq, D_INNER_A), F32),
                   jax.ShapeDtypeStruct((bsz, SUBLANES, CONV_DIM_A), F32),
                   jax.ShapeDtypeStruct((bsz, H_A, P_A, N_A), F32)],
        grid=(bsz // SB, n_chunks),
        in_specs=in_specs,
        out_specs=[tok(D_INNER_A), conv_spec, ssm_spec],
        scratch_shapes=[pltpu.VMEM((SB, SUBLANES, CONV_DIM_A), F32),
                        pltpu.VMEM((SB * (H_A // 2), 2 * P_A, N_A), F32)],
        compiler_params=_params("parallel", "arbitrary"),
        name="mamba_ssd",
    )(*args)
    return y, conv1p, jnp.swapaxes(ssm1_t, 2, 3)


def _swa_kernel(sinks_ref, q_ref, kp_ref, kc_ref, vp_ref, vc_ref, o_ref, *, SB, TQ, first_has_prev):
    n = pl.program_id(1)
    lane = lax.broadcasted_iota(jnp.int32, (1, 2 * HD_B), 1)
    low = lane < HD_B
    roll64 = lambda a: pltpu.roll(a, HD_B, axis=1)
    has_prev = jnp.logical_or(n > 0, first_has_prev)
    group = H_B // KV_B
    classes = [[h for h in range(H_B) if ((h // group) == (h % 2)) == flag] for flag in (True, False)]
    R = len(classes[0]) * TQ
    ip = lax.broadcasted_iota(jnp.int32, (R, WINDOW), 0) % TQ
    jp = lax.broadcasted_iota(jnp.int32, (R, WINDOW), 1)
    valid_prev = jnp.logical_and(jp >= ip, has_prev)
    ic = lax.broadcasted_iota(jnp.int32, (R, TQ), 0) % TQ
    jc = lax.broadcasted_iota(jnp.int32, (R, TQ), 1)
    valid_cur = jc <= ic
    for sb in range(SB):
        q = q_ref[sb] * SCALE_B
        kv_arrays = (kp_ref[sb], kc_ref[sb], vp_ref[sb], vc_ref[sb])
        res = {}
        for cls, heads in enumerate(classes):
            kpv, kcv, vpv, vcv = [(a if cls == 0 else roll64(a)).astype(BF16) for a in kv_arrays]
            qs = jnp.concatenate(
                [jnp.where(low if h % 2 == 0 else jnp.logical_not(low),
                           q[:, (h // 2) * 2 * HD_B:(h // 2 + 1) * 2 * HD_B], 0.0) for h in heads],
                axis=0).astype(BF16)
            sink = jnp.concatenate([jnp.full((TQ, 1), sinks_ref[h], F32) for h in heads], axis=0)
            sp = jnp.where(valid_prev, _dot_nt(qs, kpv), NEG_INF)
            sc = jnp.where(valid_cur, _dot_nt(qs, kcv), NEG_INF)
            mx = jnp.maximum(jnp.maximum(jnp.max(sp, axis=-1, keepdims=True),
                                         jnp.max(sc, axis=-1, keepdims=True)), sink)
            pp = jnp.exp(sp - mx)
            pc = jnp.exp(sc - mx)
            den = (jnp.sum(pp, axis=-1, keepdims=True) + jnp.sum(pc, axis=-1, keepdims=True)
                   + jnp.exp(sink - mx))
            o = (_dot(pp.astype(BF16), vpv) + _dot(pc.astype(BF16), vcv)) / den
            for idx, h in enumerate(heads):
                res[h] = o[idx * TQ:(idx + 1) * TQ]
        o_ref[sb] = jnp.concatenate([jnp.where(low, res[2 * j], res[2 * j + 1]) for j in range(H_B // 2)], axis=1)


def _swa(q, k_prev, k_cur, v_prev, v_cur, sinks, TQ, SB, prev_is_same_array):
    bsz, seq, _ = q.shape
    nb = seq // TQ
    kvw = KV_B * HD_B
    if prev_is_same_array:
        prev_map = lambda b, n: (b, jnp.maximum(n - 1, 0), 0)
    else:
        prev_map = lambda b, n: (b, 0, 0)
    kern = functools.partial(_swa_kernel, SB=SB, TQ=TQ, first_has_prev=not prev_is_same_array)
    return pl.pallas_call(
        kern,
        out_shape=jax.ShapeDtypeStruct((bsz, seq, H_B * HD_B), F32),
        grid=(bsz // SB, nb),
        in_specs=[pl.BlockSpec(memory_space=pltpu.SMEM),
                  pl.BlockSpec((SB, TQ, H_B * HD_B), lambda b, n: (b, n, 0)),
                  pl.BlockSpec((SB, WINDOW, kvw), prev_map),
                  pl.BlockSpec((SB, TQ, kvw), lambda b, n: (b, n, 0)),
                  pl.BlockSpec((SB, WINDOW, kvw), prev_map),
                  pl.BlockSpec((SB, TQ, kvw), lambda b, n: (b, n, 0))],
        out_specs=pl.BlockSpec((SB, TQ, H_B * HD_B), lambda b, n: (b, n, 0)),
        compiler_params=_params("parallel", "parallel"),
        name="swa_attn",
    )(sinks, q, k_prev, k_cur, v_prev, v_cur)


def _gla_kernel(*refs, SB, T, blk, layer, has_state, n_chunks):
    if has_state:
        (qr_ref, fr_ref, ir_ref, gr_ref, lbc_ref, gw_ref, s0_ref, o_ref, s1_ref,
         st_scr, qd_scr, kd_scr, el_scr, v_scr, od_scr, oi_scr) = refs
    else:
        (qr_ref, fr_ref, ir_ref, gr_ref, lbc_ref, gw_ref, o_ref, s1_ref,
         st_scr, qd_scr, kd_scr, el_scr, v_scr, od_scr, oi_scr) = refs
    c = pl.program_id(1)
    n_blk = T // blk

    @pl.when(c == 0)
    def _():
        if has_state:
            for sb in range(SB):
                for h in range(H_C):
                    st_scr[sb * H_C + h] = s0_ref[sb, h].T
        else:
            st_scr[...] = jnp.zeros_like(st_scr)

    lbc = lbc_ref[...]
    ex = jnp.exp(lbc - jnp.max(lbc, axis=0, keepdims=True))
    sm = ex / jnp.sum(ex, axis=0, keepdims=True)
    lb = jnp.sum(sm[1:layer + 1], axis=0, keepdims=True) if layer >= 1 else jnp.zeros_like(sm[0:1])
    rr = lax.broadcasted_iota(jnp.int32, (T, T), 0)
    cc = lax.broadcasted_iota(jnp.int32, (T, T), 1)
    same_blk = (rr // blk) == (cc // blk)
    tri_blk = jnp.where(jnp.logical_and(same_blk, rr >= cc), 1.0, 0.0).astype(BF16)
    ones_blk = jnp.where(same_blk, 1.0, 0.0).astype(BF16)
    t_in_blk = lax.broadcasted_iota(jnp.int32, (T, 1), 0) % blk

    seq_heads = [(sb, h) for sb in range(SB) for h in range(H_C)]
    for sb, h in seq_heads:
        sl = slice(h * DK_C, (h + 1) * DK_C)
        q = _silu(qr_ref[sb, :, sl])
        fg = lb[:, sl] + (1.0 - lb[:, sl]) * jax.nn.sigmoid(fr_ref[sb, :, sl])
        lf = jnp.log(fg)
        key = 1.0 - fg
        v = ir_ref[sb, :, sl]
        cr = _exact_dot_rhs(tri_blk, lf)
        last = _exact_dot_rhs(ones_blk, lf)
        od = jnp.sum(q * key, axis=-1, keepdims=True) * v
        for off in range(1, blk):
            dec = jnp.exp(jnp.where(t_in_blk >= off, cr - pltpu.roll(cr, off, axis=0), NEG_INF))
            p = jnp.sum(q * dec * pltpu.roll(key, off, axis=0), axis=-1, keepdims=True)
            od = od + p * pltpu.roll(v, off, axis=0)
        od_scr[sb, :, sl] = od
        qd_scr[sb, :, sl] = q * jnp.exp(cr)
        kd_scr[sb, :, sl] = key * jnp.exp(last - cr)
        el_scr[sb, :, sl] = jnp.exp(last)
        v_scr[sb, :, sl] = v

    def blk_step(i):
        r0 = i * blk if isinstance(i, int) else pl.multiple_of(i * blk, blk)
        rows = pl.ds(r0, blk)
        for sb, h in seq_heads:
            sl = slice(h * DK_C, (h + 1) * DK_C)
            st = st_scr[sb * H_C + h]
            oi_scr[sb, rows, sl] = _dot_nt(qd_scr[sb, rows, sl].astype(BF16), st.astype(BF16))
            st_scr[sb * H_C + h] = (el_scr[sb, pl.ds(r0, 1), sl] * st
                                    + _dot_tn(v_scr[sb, rows, sl].astype(BF16), kd_scr[sb, rows, sl].astype(BF16)))

    if n_blk == 1:
        blk_step(0)
    else:
        def body(i, carry):
            blk_step(i)
            return carry
        lax.fori_loop(0, n_blk, body, 0, unroll=2)

    for sb, h in seq_heads:
        sl = slice(h * DK_C, (h + 1) * DK_C)
        o = od_scr[sb, :, sl] + oi_scr[sb, :, sl]
        o = o * lax.rsqrt(jnp.mean(o * o, axis=-1, keepdims=True) + RMS_EPS) * gw_ref[:, sl]
        o_ref[sb, :, sl] = o * _silu(gr_ref[sb, :, sl])

    @pl.when(c == n_chunks - 1)
    def _():
        for sb, h in seq_heads:
            s1_ref[sb, h] = st_scr[sb * H_C + h].T


def _gla(qr, fr, ir, gr, lb_c, gnorm_w, s0, layer, T):
    bsz, seq, w = qr.shape
    n_chunks = seq // T
    has_state = s0 is not None
    blk = min(GLA_BLOCK, T)
    SB = min(GLA_SAMPLE_SEQS, bsz) if has_state else 1
    kern = functools.partial(_gla_kernel, SB=SB, T=T, blk=blk, layer=layer, has_state=has_state,
                             n_chunks=n_chunks)
    tok = pl.BlockSpec((SB, T, w), lambda b, c: (b, c, 0))
    st_spec = pl.BlockSpec((SB, H_C, DK_C, DV_C), lambda b, c: (b, 0, 0, 0))
    in_specs = [tok, tok, tok, tok,
                pl.BlockSpec((DEPTH, w), lambda b, c: (0, 0)),
                pl.BlockSpec((1, w), lambda b, c: (0, 0))]
    args = [qr, fr, ir, gr, lb_c, gnorm_w[None, :]]
    if has_state:
        in_specs.append(st_spec)
        args.append(s0)
    return pl.pallas_call(
        kern,
        out_shape=[jax.ShapeDtypeStruct((bsz, seq, w), F32),
                   jax.ShapeDtypeStruct((bsz, H_C, DK_C, DV_C), F32)],
        grid=(bsz // SB, n_chunks),
        in_specs=in_specs,
        out_specs=[tok, st_spec],
        scratch_shapes=([pltpu.VMEM((SB * H_C, DV_C, DK_C), F32)]
                        + [pltpu.VMEM((SB, T, w), F32) for _ in range(6)]),
        compiler_params=_params("parallel", "arbitrary"),
        name="gla_hgrn2",
    )(*args)


BIAS_TERMS = 3
BIAS_SELF_LANE = BIAS_TERMS * H_D


def _fox_prep_kernel(q_ref, k_ref, v_ref, fd_ref, fb_ref, lf_ref, qm_ref, k2_ref, v2_ref, carry_scr, *, T):
    c = pl.program_id(1)

    @pl.when(c == 0)
    def _():
        carry_scr[...] = jnp.zeros_like(carry_scr)

    lf = -_softplus(-(fd_ref[0] + fb_ref[...]))
    lf_ref[0] = lf
    cum = _exact_dot_rhs(_lower_tri_bf16(T), lf) + carry_scr[...]
    carry_scr[...] = cum[T - 1:T, :]
    hi, mid, lo = [t.astype(F32) for t in _split3(cum * LOG2E)]

    lane = lax.broadcasted_iota(jnp.int32, (1, 2 * HD_D), 1)
    low = lane < HD_D
    roll_half = lambda a: pltpu.roll(a, HD_D, axis=1)
    xk_low = jnp.where(lane < H_D, -hi,
                       jnp.where(lane < 2 * H_D, pltpu.roll(-mid, H_D, axis=1),
                                 jnp.where(lane < BIAS_SELF_LANE, pltpu.roll(-lo, 2 * H_D, axis=1),
                                           jnp.where(lane < BIAS_SELF_LANE + BIAS_TERMS, 1.0, 0.0))))
    xk_high = roll_half(xk_low)
    for i in range(KV_D // 2):
        blk = slice(i * 2 * HD_D, (i + 1) * 2 * HD_D)
        kb = k_ref[0, :, blk]
        vb = v_ref[0, :, blk]
        kr = roll_half(kb)
        vr = roll_half(vb)
        for par in (0, 1):
            kv = 2 * i + par
            k_low, k_high = (kb, kr) if par == 0 else (kr, kb)
            k2_ref[0, 2 * kv] = jnp.where(low, k_low, xk_high).astype(BF16)
            k2_ref[0, 2 * kv + 1] = jnp.where(low, xk_low, k_high).astype(BF16)
            v_low, v_high = (vb, vr) if par == 0 else (vr, vb)
            v2_ref[0, 2 * kv] = jnp.where(low, v_low, jnp.where(lane == HD_D, 1.0, 0.0)).astype(BF16)
            v2_ref[0, 2 * kv + 1] = jnp.where(low, jnp.where(lane == 0, 1.0, 0.0), v_high).astype(BF16)
    for h in range(H_D):
        qp = q_ref[0, :, (h // 2) * 2 * HD_D:(h // 2 + 1) * 2 * HD_D] * (SCALE_D * LOG2E)
        own =jnp.logical_or(jnp.logical_or(lane == h, lane == H_D + h), lane == 2 * H_D + h)
        xq = jnp.where(own, 1.0,
                       jnp.where(lane == BIAS_SELF_LANE, hi[:, h:h + 1],
                                 jnp.where(lane == BIAS_SELF_LANE + 1, mid[:, h:h + 1],
                                           jnp.where(lane == BIAS_SELF_LANE + 2, lo[:, h:h + 1], 0.0))))
        qm = jnp.where(low, qp, roll_half(xq)) if h % 2 == 0 else jnp.where(low, xq, qp)
        qm_ref[0, h] = qm.astype(BF16)


def _fox_prep(q, k, v, fd, fbias_pad, T):
    bsz, seq, _ = q.shape
    kern = functools.partial(_fox_prep_kernel, T=T)
    tok = lambda w: pl.BlockSpec((1, T, w), lambda b, c: (b, c, 0))
    heads = lambda n: pl.BlockSpec((1, n, T, 2 * HD_D), lambda b, c: (b, 0, c, 0))
    return pl.pallas_call(
        kern,
        out_shape=[jax.ShapeDtypeStruct((bsz, seq, LANES), F32),
                   jax.ShapeDtypeStruct((bsz, H_D, seq, 2 * HD_D), BF16),
                   jax.ShapeDtypeStruct((bsz, 2 * KV_D, seq, 2 * HD_D), BF16),
                   jax.ShapeDtypeStruct((bsz, 2 * KV_D, seq, 2 * HD_D), BF16)],
        grid=(bsz, seq // T),
        in_specs=[tok(H_D * HD_D), tok(KV_D * HD_D), tok(KV_D * HD_D), tok(LANES),
                  pl.BlockSpec((1, LANES), lambda b, c: (0, 0))],
        out_specs=[tok(LANES), heads(H_D), heads(2 * KV_D), heads(2 * KV_D)],
        scratch_shapes=[pltpu.VMEM((1, LANES), F32)],
        compiler_params=_params("parallel", "arbitrary"),
        name="fox_prep",
    )(q, k, v, fd, fbias_pad)


def _fox_flash_kernel(qi_ref, ki_ref, qm_ref, k2_ref, v2_ref, o_ref, m_scr, acc_scr, *, TQ, TK, HG):
    s_id = pl.program_id(2)
    qi = qi_ref[s_id]
    ki = ki_ref[s_id]
    group = H_D // KV_D
    n_heads = H_D // HG

    @pl.when(ki == 0)
    def _():
        m_scr[...] = jnp.full_like(m_scr, NEG_INF)
        acc_scr[...] = jnp.zeros_like(acc_scr)

    def head_step(h, masked):
        kv = h // group
        s = _dot_nt(qm_ref[0, h], k2_ref[0, 2 * kv + h % 2])
        if masked:
            r = lax.broadcasted_iota(jnp.int32, (TQ, TK), 0)
            c = lax.broadcasted_iota(jnp.int32, (TQ, TK), 1)
            s = jnp.where(c - r <= qi * TQ - ki * TK, s, NEG_INF)
        m_prev = m_scr[h]
        m_new = jnp.maximum(m_prev, jnp.max(s, axis=-1, keepdims=True))
        alpha = jnp.exp2(m_prev - m_new)
        p = jnp.exp2(s - jnp.concatenate([m_new] * (TK // LANES), axis=1))
        acc_scr[h] = alpha * acc_scr[h] + _dot(p.astype(BF16), v2_ref[0, 2 * kv + h % 2])
        m_scr[h] = m_new

    def all_heads(masked):
        for h in range(n_heads):
            head_step(h, masked)

    ratio = TQ // TK
    @pl.when(ki < qi * ratio)
    def _():
        all_heads(False)

    @pl.when(ki >= qi * ratio)
    def _():
        all_heads(True)

    @pl.when(ki == (qi + 1) * ratio - 1)
    def _():
        lane = lax.broadcasted_iota(jnp.int32, (1, 2 * HD_D), 1)
        low = lane < HD_D
        for j in range(n_heads // 2):
            a0 = acc_scr[2 * j]
            a1 = acc_scr[2 * j + 1]
            o0 = a0 / a0[:, HD_D:HD_D + 1]
            o1 = a1 / a1[:, 0:1]
            o_ref[0, :, j * 2 * HD_D:(j + 1) * 2 * HD_D] = jnp.where(low, o0, o1)


def _fox_flash(qm, k2, v2, TQ, TK):
    bsz, _, seq, _ = qm.shape
    assert TQ % TK == 0
    nq = seq // TQ
    pairs = [(a, b) for a in range(nq) for b in range((a + 1) * (TQ // TK))]
    qi_tab = jnp.asarray([p[0] for p in pairs], jnp.int32)
    ki_tab = jnp.asarray([p[1] for p in pairs], jnp.int32)
    HG = FOX_HEAD_GROUPS
    nh, nkv = H_D // HG, KV_D // HG
    kern = functools.partial(_fox_flash_kernel, TQ=TQ, TK=TK, HG=HG)
    qw = H_D * HD_D
    grid_spec = pltpu.PrefetchScalarGridSpec(
        num_scalar_prefetch=2,
        grid=(bsz, HG, len(pairs)),
        in_specs=[pl.BlockSpec((1, nh, TQ, 2 * HD_D), lambda b, g, s, qt, kt: (b, g, qt[s], 0)),
                  pl.BlockSpec((1, 2 * nkv, TK, 2 * HD_D), lambda b, g, s, qt, kt: (b, g, kt[s], 0)),
                  pl.BlockSpec((1, 2 * nkv, TK, 2 * HD_D), lambda b, g, s, qt, kt: (b, g, kt[s], 0))],
        out_specs=pl.BlockSpec((1, TQ, qw // HG), lambda b, g, s, qt, kt: (b, qt[s], g)),
        scratch_shapes=[pltpu.VMEM((nh, TQ, LANES), F32),
                        pltpu.VMEM((nh, TQ, 2 * HD_D), F32)])
    return pl.pallas_call(
        kern,
        out_shape=jax.ShapeDtypeStruct((bsz, seq, qw), F32),
        grid_spec=grid_spec,
        compiler_params=_params("parallel", "parallel", "arbitrary"),
        name="fox_flash",
    )(qi_tab, ki_tab, qm, k2, v2)


def _fox_paged_kernel(pt_ref, q_ref, kn_ref, vn_ref, fdn_ref, fb_ref, kt_hbm, vt_hbm, lft_hbm, o_ref, lfn_ref,
                      kbuf, vbuf, lbuf, sem, qall_scr, m_scr, l_scr, acc_scr, carry_scr,
                      *, SEQS, PP, NG, base, n_pages):
    b = pl.program_id(0)
    g = pl.program_id(1)
    n_steps = pl.num_programs(0) * NG
    t = b * NG + g
    slot = t % 2
    L = q_ref.shape[1]
    rows = H_D * L
    group = H_D // KV_D
    W = PP * PAGE_SIZE

    def page_copies(bb, gg, sl):
        cps = []
        for sq in range(SEQS):
            for i in range(PP):
                pg = base + pt_ref[bb * SEQS + sq, n_pages - 1 - (gg * PP + i)]
                dst = sq * PP + i
                cps.append(pltpu.make_async_copy(kt_hbm.at[pg], kbuf.at[sl, dst], sem.at[0, sl]))
                cps.append(pltpu.make_async_copy(vt_hbm.at[pg], vbuf.at[sl, dst], sem.at[1, sl]))
                cps.append(pltpu.make_async_copy(lft_hbm.at[pg], lbuf.at[sl, dst], sem.at[2, sl]))
        return cps

    @pl.when(t == 0)
    def _():
        for cp in page_copies(b, g, slot):
            cp.start()

    @pl.when(t + 1 < n_steps)
    def _():
        t1 = t + 1
        for cp in page_copies(t1 // NG, t1 % NG, 1 - slot):
            cp.start()

    @pl.when(g == 0)
    def _():
        for sq in range(SEQS):
            q = q_ref[sq] * SCALE_D
            for h in range(H_D):
                kv = h // group
                piece = q[:, h * HD_D:(h + 1) * HD_D]
                parts = []
                if kv > 0:
                    parts.append(jnp.zeros((L, kv * HD_D), F32))
                parts.append(piece)
                if kv < KV_D - 1:
                    parts.append(jnp.zeros((L, (KV_D - 1 - kv) * HD_D), F32))
                qall_scr[sq, h * L:(h + 1) * L, :] = jnp.concatenate(parts, axis=1).astype(BF16)
            lfn = -_softplus(-(fdn_ref[sq] + fb_ref[...]))
            lfn_ref[sq] = lfn
            cumn = _exact_dot_rhs(_lower_tri_bf16(L), lfn)
            cumn_t = cumn.T[0:H_D, :]
            bias = jnp.broadcast_to((-cumn_t)[:, None, :], (H_D, L, L)).reshape(rows, L)
            s = _dot_nt(qall_scr[sq], kn_ref[sq].astype(BF16)) + bias
            qidx = lax.broadcasted_iota(jnp.int32, (rows, L), 0) % L
            kidx = lax.broadcasted_iota(jnp.int32, (rows, L), 1)
            s = jnp.where(kidx <= qidx, s, NEG_INF)
            mx = jnp.max(s, axis=-1, keepdims=True)
            p = jnp.exp(s - mx)
            m_scr[sq] = mx
            l_scr[sq] = jnp.sum(p, axis=-1, keepdims=True)
            acc_scr[sq] = _dot(p.astype(BF16), vn_ref[sq].astype(BF16))
        carry_scr[...] = jnp.zeros_like(carry_scr)

    for cp in page_copies(b, g, slot):
        cp.wait()
    lane_in_page = lax.broadcasted_iota(jnp.int32, (1, W), 1) % PAGE_SIZE
    for sq in range(SEQS):
        pages = range(sq * PP, (sq + 1) * PP)
        kcat = jnp.concatenate([kbuf[slot, i] for i in pages], axis=1).astype(BF16)
        vcat = jnp.concatenate([vbuf[slot, i] for i in pages], axis=1).astype(BF16)
        s = _dot(qall_scr[sq], kcat)
        lft = jnp.concatenate([lbuf[slot, i] for i in pages], axis=1)
        y = lft
        step = 1
        while step < PAGE_SIZE:
            y = y + jnp.where(lane_in_page < PAGE_SIZE - step, pltpu.roll(y, W - step, axis=1), 0.0)
            step *= 2
        d_local = y - lft
        carry = carry_scr[sq]
        pieces = []
        for i in range(PP):
            pieces.append(d_local[:, i * PAGE_SIZE:(i + 1) * PAGE_SIZE] + carry)
            carry = carry + y[:, i * PAGE_SIZE:i * PAGE_SIZE + 1]
        carry_scr[sq] = carry
        bias_t = jnp.concatenate(pieces, axis=1)
        s = s + jnp.broadcast_to(bias_t[:, None, :], (H_D, L, W)).reshape(rows, W)
        m_prev = m_scr[sq]
        m_new = jnp.maximum(m_prev, jnp.max(s, axis=-1, keepdims=True))
        alpha = jnp.exp(m_prev - m_new)
        p = jnp.exp(s - m_new)
        l_scr[sq] = alpha * l_scr[sq] + jnp.sum(p, axis=-1, keepdims=True)
        acc_scr[sq] = alpha * acc_scr[sq] + _dot_nt(p.astype(BF16), vcat)
        m_scr[sq] = m_new

    @pl.when(g == NG - 1)
    def _():
        for sq in range(SEQS):
            o = acc_scr[sq] / l_scr[sq]
            parts = []
            for h in range(H_D):
                kv = h // group
                parts.append(o[h * L:(h + 1) * L, kv * HD_D:(kv + 1) * HD_D])
            o_ref[sq] = jnp.concatenate(parts, axis=1)


def _fox_paged(q, k_new, v_new, fd_new, fbias_pad, pool_k, pool_v, pool_lf, page_table, layer, PP):
    bsz, L, qw = q.shape
    n_pool = pool_k.shape[1]
    n_pages = page_table.shape[1]
    NG = n_pages // PP
    kvw = KV_D * HD_D
    pkt = jnp.transpose(pool_k, (0, 1, 3, 4, 2)).reshape(pool_k.shape[0] * n_pool, kvw, PAGE_SIZE)
    pvt = jnp.transpose(pool_v, (0, 1, 3, 4, 2)).reshape(pool_v.shape[0] * n_pool, kvw, PAGE_SIZE)
    plft = jnp.transpose(pool_lf, (0, 1, 3, 2)).reshape(pool_lf.shape[0] * n_pool, H_D, PAGE_SIZE)
    base = layer * n_pool

    SEQS = min(FOX_SEQS_PER_STEP, bsz)
    seq_map = lambda b, g, pt: (b, 0, 0)
    in_specs = [pl.BlockSpec((SEQS, L, qw), seq_map),
                pl.BlockSpec((SEQS, L, kvw), seq_map),
                pl.BlockSpec((SEQS, L, kvw), seq_map),
                pl.BlockSpec((SEQS, L, LANES), seq_map),
                pl.BlockSpec((1, LANES), lambda b, g, pt: (0, 0)),
                pl.BlockSpec(memory_space=pl.ANY),
                pl.BlockSpec(memory_space=pl.ANY),
                pl.BlockSpec(memory_space=pl.ANY)]
    kern = functools.partial(_fox_paged_kernel, SEQS=SEQS, PP=PP, NG=NG, base=base, n_pages=n_pages)
    grid_spec = pltpu.PrefetchScalarGridSpec(
        num_scalar_prefetch=1,
        grid=(bsz // SEQS, NG),
        in_specs=in_specs,
        out_specs=[pl.BlockSpec((SEQS, L, qw), seq_map), pl.BlockSpec((SEQS, L, LANES), seq_map)],
        scratch_shapes=[pltpu.VMEM((2, SEQS * PP, kvw, PAGE_SIZE), F32),
                        pltpu.VMEM((2, SEQS * PP, kvw, PAGE_SIZE), F32),
                        pltpu.VMEM((2, SEQS * PP, H_D, PAGE_SIZE), F32),
                        pltpu.SemaphoreType.DMA((3, 2)),
                        pltpu.VMEM((SEQS, H_D * L, kvw), BF16),
                        pltpu.VMEM((SEQS, H_D * L, 1), F32),
                        pltpu.VMEM((SEQS, H_D * L, 1), F32),
                        pltpu.VMEM((SEQS, H_D * L, kvw), F32),
                        pltpu.VMEM((SEQS, H_D, 1), F32)])
    return pl.pallas_call(
        kern,
        out_shape=[jax.ShapeDtypeStruct((bsz, L, qw), F32), jax.ShapeDtypeStruct((bsz, L, LANES), F32)],
        grid_spec=grid_spec,
        compiler_params=_params("arbitrary", "arbitrary"),
        name="fox_paged",
    )(page_table, q, k_new, v_new, fd_new, fbias_pad, pkt, pvt, plft)


def _pad_cols(w, n):
    return jnp.pad(w, ((0, 0), (0, n - w.shape[1])))


def _prep_w_ab(w):
    z, xbc, dt, q, k, v = jnp.split(w, np.cumsum((D_INNER_A, CONV_DIM_A, H_A, H_B * HD_B, KV_B * HD_B, KV_B * HD_B))[:-1].tolist(), axis=1)
    return jnp.concatenate([z, xbc, q, k, v, _pad_cols(dt, LANES)], axis=1).astype(BF16)


AB_SIZES = (D_INNER_A, CONV_DIM_A, H_B * HD_B, KV_B * HD_B, KV_B * HD_B, LANES)


def _prep_w_cd(w):
    sizes = (H_C * DK_C, H_C * DK_C, H_C * DV_C, H_C * DV_C, H_D * HD_D, KV_D * HD_D, KV_D * HD_D, H_D)
    parts = jnp.split(w, np.cumsum(sizes)[:-1].tolist(), axis=1)
    parts[-1] = _pad_cols(parts[-1], LANES)
    return jnp.concatenate(parts, axis=1).astype(BF16)


CD_SIZES = (H_C * DK_C, H_C * DK_C, H_C * DV_C, H_C * DV_C, H_D * HD_D, KV_D * HD_D, KV_D * HD_D, LANES)


def _trunk(x, c, st, prm, page_table):
    sample = page_table is not None
    bsz, seq, _ = x.shape
    if sample:
        bb, tl, out_tl, ffn_tl = min(SAMPLE_BATCH_TILE, bsz), seq, seq, seq
    else:
        bb, tl, out_tl, ffn_tl = 1, min(ROW_TILE, seq), min(OUT_ROW_TILE, seq), min(FFN_ROW_TILE, seq)
    n_sub = DEPTH * 2
    m_all = _ada_all(c, prm['ada_w'].reshape(n_sub, D_MODEL, 3 * D_MODEL), prm['ada_b'].reshape(n_sub, 1, 3 * D_MODEL))
    new = {}
    for l in range(DEPTH):
        j = l // 2
        m_mix = m_all[2 * l][:, None, :]
        m_ffn = m_all[2 * l + 1][:, None, :]
        if l % 2 == 0:
            z, xbc, qb, kb, vb, dtp = _mod_matmul(x, m_mix, _prep_w_ab(prm['w_in_ab'][j]), AB_SIZES, bb, tl)
            if sample:
                conv0p = jnp.pad(st['conv_a'][j], ((0, 0), (SUBLANES - (CONV_A - 1), 0), (0, 0)))
                ssm0 = st['ssm_a'][j]
            else:
                conv0p, ssm0 = None, None
            ya, conv1p, ssm1 = _mamba(z, xbc, dtp, conv0p, ssm0, prm['conv_w_a'][j], prm['conv_b_a'][j],
                                      prm['dt_bias_a'][j], prm['a_log_a'][j], prm['d_skip_a'][j],
                                      prm['norm_w_a'][j], T=min(SSD_CHUNK, seq))
            if sample:
                kbuf = st['swa_k'][j].reshape(bsz, WINDOW, KV_B * HD_B)
                vbuf = st['swa_v'][j].reshape(bsz, WINDOW, KV_B * HD_B)
                yb = _swa(qb, kbuf, kb, vbuf, vb, prm['sinks_b'][j], TQ=seq, SB=min(SWA_SAMPLE_SEQS, bsz),
                          prev_is_same_array=False)
                bk = jnp.concatenate([kbuf[:, seq:], kb], axis=1)
                bv = jnp.concatenate([vbuf[:, seq:], vb], axis=1)
            else:
                yb = _swa(qb, kb, kb, vb, vb, prm['sinks_b'][j], TQ=WINDOW, SB=1, prev_is_same_array=True)
                bk, bv = kb[:, -WINDOW:], vb[:, -WINDOW:]
            w_out = prm['w_out_ab'][j].astype(BF16)
            x = _mm_res_ln([ya, yb], [w_out[:D_INNER_A], w_out[D_INNER_A:]], x, m_mix,
                           prm['ln_g'][l, 0][None, :], prm['ln_b'][l, 0][None, :], bb, out_tl)
            new.setdefault('ssm_a', []).append(ssm1)
            new.setdefault('conv_a', []).append(conv1p[:, SUBLANES - (CONV_A - 1):, :])
            new.setdefault('swa_k', []).append(bk.reshape(bsz, WINDOW, KV_B, HD_B))
            new.setdefault('swa_v', []).append(bv.reshape(bsz, WINDOW, KV_B, HD_B))
        else:
            qc, fc, ic, gc, qd, kd, vd, fdp = _mod_matmul(x, m_mix, _prep_w_cd(prm['w_in_cd'][j]), CD_SIZES, bb, tl)
            s0 = st['hgrn_c'][j] if sample else None
            yc, s1 = _gla(qc, fc, ic, gc, prm['lb_c'], prm['gnorm_c'][j], s0, layer=l, T=min(GLA_CHUNK, seq))
            fbias_pad = jnp.zeros((1, LANES), F32).at[0, :H_D].set(prm['fbias_d'][j])
            if sample:
                yd, lfp = _fox_paged(qd, kd, vd, fdp, fbias_pad, st['fox_k'], st['fox_v'], st['fox_logf'],
                                     page_table, j, FOX_PAGES_PER_STEP)
            else:
                lfp, qm, k2, v2 = _fox_prep(qd, kd, vd, fdp, fbias_pad, min(FOX_PREP_T, seq))
                yd = _fox_flash(qm, k2, v2, min(FOX_TQ, seq), min(FOX_TK, seq))
            w_out = prm['w_out_cd'][j].astype(BF16)
            x = _mm_res_ln([yc, yd], [w_out[:H_C * DV_C], w_out[H_C * DV_C:]], x, m_mix,
                           prm['ln_g'][l, 0][None, :], prm['ln_b'][l, 0][None, :], bb, out_tl)
            new.setdefault('hgrn_c', []).append(s1)
            new.setdefault('fox_k', []).append(kd.reshape(bsz, seq, KV_D, HD_D))
            new.setdefault('fox_v', []).append(vd.reshape(bsz, seq, KV_D, HD_D))
            new.setdefault('fox_logf', []).append(lfp[:, :, :H_D])
        buf0 = st['ffn_conv'][l] if sample else jnp.zeros((bsz, FFN_CONV - 1, D_FF), F32)
        a, buf1 = _ffn_in(x, m_ffn, prm['ffn_w_in'][l].astype(BF16), buf0, prm['ffn_conv_w'][l],
                          prm['ffn_conv_b'][l][None, :], bb, ffn_tl)
        x = _mm_res_ln([a], [prm['ffn_w_out'][l].astype(BF16)], x, m_ffn,
                       prm['ln_g'][l, 1][None, :], prm['ln_b'][l, 1][None, :], bb, ffn_tl)
        new.setdefault('ffn_conv', []).append(buf1)
    return x, {name: jnp.stack(rows, axis=0) for name, rows in new.items()}


def kernel(x_prompt, x_sample, state_ssm_a, state_conv_a, cache_swa_k, cache_swa_v, state_hgrn_c, cache_fox_k, cache_fox_v, cache_fox_logf, state_ffn_conv, page_table, c_prompt, c_sample, ada_w, ada_b, ln_g, ln_b, w_in_ab, w_out_ab, conv_w_a, conv_b_a, dt_bias_a, a_log_a, d_skip_a, norm_w_a, sinks_b, w_in_cd, w_out_cd, lb_c, gnorm_c, fbias_d, ffn_w_in, ffn_conv_w, ffn_conv_b, ffn_w_out):
    prm = dict(ada_w=ada_w, ada_b=ada_b, ln_g=ln_g, ln_b=ln_b, w_in_ab=w_in_ab, w_out_ab=w_out_ab,
               conv_w_a=conv_w_a, conv_b_a=conv_b_a, dt_bias_a=dt_bias_a, a_log_a=a_log_a,
               d_skip_a=d_skip_a, norm_w_a=norm_w_a, sinks_b=sinks_b, w_in_cd=w_in_cd, w_out_cd=w_out_cd,
               lb_c=lb_c, gnorm_c=gnorm_c, fbias_d=fbias_d, ffn_w_in=ffn_w_in, ffn_conv_w=ffn_conv_w,
               ffn_conv_b=ffn_conv_b, ffn_w_out=ffn_w_out)
    st = dict(ssm_a=state_ssm_a, conv_a=state_conv_a, swa_k=cache_swa_k, swa_v=cache_swa_v,
              hgrn_c=state_hgrn_c, fox_k=cache_fox_k, fox_v=cache_fox_v, fox_logf=cache_fox_logf,
              ffn_conv=state_ffn_conv)
    y_p, sp = _trunk(x_prompt, c_prompt, None, prm, None)
    y_s, ss = _trunk(x_sample, c_sample, st, prm, page_table)
    return (y_p, y_s,
            sp['ssm_a'], ss['ssm_a'], sp['conv_a'], ss['conv_a'],
            sp['swa_k'], ss['swa_k'], sp['swa_v'], ss['swa_v'],
            sp['hgrn_c'], ss['hgrn_c'],
            sp['fox_k'], ss['fox_k'], sp['fox_v'], ss['fox_v'], sp['fox_logf'], ss['fox_logf'],
            sp['ffn_conv'], ss['ffn_conv'])
```
